```python
import jax
import jax.numpy as jnp
from jax import lax
import numpy as np

D_MODEL = 4096
BATCH = 2
SEQ = 4096
DEPTH = 2

GROUP_W = D_MODEL // 4
HEAD_DIM = 128
N_HEADS = GROUP_W // HEAD_DIM
N_KV = N_HEADS // 4
HPG = N_HEADS // N_KV
ROPE_THETA = 500000.0
NORM_EPS = 1e-6
NEG = -1e30
FORCE = 1e6

LRU_BLOCKS = N_HEADS
LRU_C = 8.0
CONV_A = 4
CONV_D = 3

CMP_LEN = 32
CMP_STRIDE = 16
CMP_HID = HEAD_DIM
SLC_BLK = 64
SLC_TOPN = 16
WINDOW = 512
WIN_QBLK = 128
GATHER_QBLK = 64

IDX_HEADS = 8
IDX_DIM = 64
DSA_TOPK = 256

SPLIT_SIZES = (
    GROUP_W, GROUP_W,
    N_HEADS * HEAD_DIM, GROUP_W, 6 * N_KV * HEAD_DIM, 3 * N_HEADS,
    N_HEADS * HEAD_DIM, GROUP_W, 2 * N_KV * HEAD_DIM,
    IDX_HEADS * IDX_DIM, IDX_DIM, IDX_HEADS,
    GROUP_W, GROUP_W, GROUP_W, GROUP_W,
)
IN_WIDTH = sum(SPLIT_SIZES)
MIX_W = 4 * GROUP_W

kernel_name = 'hybrid_rglru_nsa_dsa_shortconv'


def _rmsnorm(x, g):
    xf = x.astype(jnp.float32)
    y = xf * lax.rsqrt(jnp.mean(xf * xf, axis=-1, keepdims=True) + NORM_EPS)
    return (y * g.astype(jnp.float32)).astype(x.dtype)


def _rope(x, pos):
    d = x.shape[-1]
    r = d // 4
    half = r // 2
    inv = ROPE_THETA ** (-jnp.arange(half, dtype=jnp.float32) * 2.0 / r)
    ang = pos.astype(jnp.float32)[:, None] * inv[None, :]
    cos = jnp.cos(ang)[None, :, None, :]
    sin = jnp.sin(ang)[None, :, None, :]
    xf = x.astype(jnp.float32)
    x1 = xf[..., :half]
    x2 = xf[..., half:r]
    out = jnp.concatenate([x1 * cos - x2 * sin, x2 * cos + x1 * sin, xf[..., r:]], axis=-1)
    return out.astype(x.dtype)


def _causal_dwconv(x, w):
    K, C = w.shape
    return lax.conv_general_dilated(x, w[:, None, :], window_strides=(1,), padding=[(K - 1, 0)],
                                    dimension_numbers=('NWC', 'WIO', 'NWC'), feature_group_count=C)


def _masked_softmax(s, m):
    return jax.nn.softmax(jnp.where(m, s, NEG), axis=-1)


def _rglru(xa, conv_w, conv_b, wa, ba, wx, bx, lam):
    B, S, C = xa.shape
    u = _causal_dwconv(xa, conv_w) + conv_b
    ub = u.reshape(B, S, LRU_BLOCKS, C // LRU_BLOCKS)
    r = jax.nn.sigmoid((jnp.einsum('bshi,hij->bshj', ub, wa).reshape(B, S, C) + ba).astype(jnp.float32))
    i = jax.nn.sigmoid((jnp.einsum('bshi,hij->bshj', ub, wx).reshape(B, S, C) + bx).astype(jnp.float32))
    log_a = -LRU_C * r * jax.nn.softplus(-lam.astype(jnp.float32))
    a = jnp.exp(log_a)
    b = jnp.sqrt(-jnp.expm1(2.0 * log_a)) * (i * u.astype(jnp.float32))

    def comb(c1, c2):
        a1, b1 = c1
        a2, b2 = c2
        return a1 * a2, a2 * b1 + b2

    _, h = lax.associative_scan(comb, (a, b), axis=1)
    return h.astype(xa.dtype)


def _compress(k, pe, w1, w2):
    B, S, G, d = k.shape
    nc = S // CMP_STRIDE
    ch = k.reshape(B, nc, CMP_STRIDE, G, d)
    blocks = jnp.concatenate([ch[:, :-1], ch[:, 1:]], axis=2) + pe[None, None, :, None, :]
    h = jax.nn.silu(jnp.einsum('bnlgd,lde->bnge', blocks, w1))
    return jnp.einsum('bnge,ef->bngf', h, w2)


def _selected_attn(qg, k, v, sel, pos, scale):
    B, S, G, hpg, d = qg.shape
    nblk = S // SLC_BLK
    n_top = sel.shape[-1]
    kb = k.reshape(B, nblk, SLC_BLK, G, d).transpose(0, 3, 1, 2, 4)
    vb = v.reshape(B, nblk, SLC_BLK, G, d).transpose(0, 3, 1, 2, 4)
    Q = GATHER_QBLK
    nq = S // Q
    bidx = jnp.arange(B)[:, None, None, None]
    gidx = jnp.arange(G)[None, None, :, None]
    off = jnp.arange(SLC_BLK)

    def chunk(t):
        return t.reshape((B, nq, Q) + t.shape[2:]).swapaxes(0, 1)

    def body(args):
        qc, sc, tpos = args
        kg = kb[bidx, gidx, sc].reshape(B, Q, G, n_top * SLC_BLK, d)
        vg = vb[bidx, gidx, sc].reshape(B, Q, G, n_top * SLC_BLK, d)
        kpos = (sc[..., None] * SLC_BLK + off).reshape(B, Q, G, n_top * SLC_BLK)
        m = (kpos <= tpos[None, :, None, None])[:, :, :, None, :]
        s = jnp.einsum('bqghd,bqgkd->bqghk', qc, kg).astype(jnp.float32) * scale
        p = _masked_softmax(s, m)
        return jnp.einsum('bqghk,bqgkd->bqghd', p.astype(vg.dtype), vg)

    out = lax.map(body, (chunk(qg), chunk(sel), pos.reshape(nq, Q)))
    return out.swapaxes(0, 1).reshape(B, S, G, hpg, d)


def _window_attn(qg, k, v, pos, scale):
    B, S, G, hpg, d = qg.shape
    QB = WIN_QBLK
    nb = S // QB
    nw = WINDOW // QB + 1
    KL = nw * QB
    pad = ((0, 0), (WINDOW, 0), (0, 0), (0, 0))
    kp = jnp.pad(k, pad).reshape(B, nb + nw - 1, QB, G, d)
    vp = jnp.pad(v, pad).reshape(B, nb + nw - 1, QB, G, d)
    bi = jnp.arange(nb)[:, None] + jnp.arange(nw)[None, :]
    kb = kp[:, bi].reshape(B, nb, KL, G, d)
    vb = vp[:, bi].reshape(B, nb, KL, G, d)
    qb = qg.reshape(B, nb, QB, G, hpg, d)
    s = jnp.einsum('bnqghd,bnkgd->bnqghk', qb, kb).astype(jnp.float32) * scale
    tq = pos.reshape(nb, QB)
    sk = (jnp.arange(nb) * QB - WINDOW)[:, None] + jnp.arange(KL)[None, :]
    diff = tq[:, :, None] - sk[:, None, :]
    m = ((diff >= 0) & (diff < WINDOW) & (sk[:, None, :] >= 0))[None, :, :, None, None, :]
    p = _masked_softmax(s, m)
    o = jnp.einsum('bnqghk,bnkgd->bnqghd', p.astype(vb.dtype), vb)
    return o.reshape(B, S, G, hpg, d)


def _nsa(q, kv, gate, pos, q_gain, k_gain, pe_k, w1_k, w2_k, pe_v, w1_v, w2_v):
    B, S, _ = q.shape
    H, G, d = N_HEADS, N_KV, HEAD_DIM
    scale = d ** -0.5
    q = _rmsnorm(q.reshape(B, S, H, d), q_gain)
    kv = kv.reshape(B, S, 6, G, d)
    k_c, v_c, k_s, v_s, k_w, v_w = [kv[:, :, j] for j in range(6)]
    kc = _rmsnorm(_compress(k_c, pe_k, w1_k, w2_k), k_gain)
    vc = _compress(v_c, pe_v, w1_v, w2_v)
    n_cmp = kc.shape[1]
    qg = q.reshape(B, S, G, HPG, d)
    s = jnp.einsum('bsghd,bngd->bsghn', qg, kc).astype(jnp.float32) * scale
    cmp_start = jnp.arange(n_cmp) * CMP_STRIDE
    cm = ((cmp_start + CMP_LEN - 1)[None, :] <= pos[:, None])[None, :, None, None, :]
    p = _masked_softmax(s, cm) * cm
    o_cmp = jnp.einsum('bsghn,bngd->bsghd', p.astype(vc.dtype), vc)
    n_slc = S // SLC_BLK
    slc_start = jnp.arange(n_slc) * SLC_BLK
    ov = jnp.clip(jnp.minimum(cmp_start[:, None] + CMP_LEN, slc_start[None, :] + SLC_BLK)
                  - jnp.maximum(cmp_start[:, None], slc_start[None, :]), 0, None).astype(jnp.float32) / CMP_LEN
    imp = jnp.einsum('bsgn,nk->bsgk', p.sum(axis=3), ov)
    blk_t = pos // SLC_BLK
    kk = jnp.arange(n_slc)[None, :]
    valid = kk <= blk_t[:, None]
    force = (kk == 0) | (kk == blk_t[:, None]) | (kk == blk_t[:, None] - 1)
    imp = jnp.where(force[None, :, None, :], FORCE, jnp.where(valid[None, :, None, :], imp, -FORCE))
    n_top = min(SLC_TOPN, n_slc)
    _, sel = lax.top_k(imp, n_top)
    qr = _rope(q, pos).reshape(B, S, G, HPG, d)
    ks = _rope(_rmsnorm(k_s, k_gain), pos)
    kw = _rope(_rmsnorm(k_w, k_gain), pos)
    o_slc = _selected_attn(qr, ks, v_s, sel, pos, scale)
    o_win = _window_attn(qr, kw, v_w, pos, scale)
    g = jax.nn.sigmoid(gate.astype(jnp.float32)).reshape(B, S, 3, H)
    f32 = jnp.float32
    out = (g[:, :, 0, :, None] * o_cmp.reshape(B, S, H, d).astype(f32)
           + g[:, :, 1, :, None] * o_slc.reshape(B, S, H, d).astype(f32)
           + g[:, :, 2, :, None] * o_win.reshape(B, S, H, d).astype(f32))
    return out.reshape(B, S, H * d).astype(q.dtype)


def _dsa(q, kv, iq, ik, iw, pos, q_gain, k_gain):
    B, S, _ = q.shape
    H, G, d = N_HEADS, N_KV, HEAD_DIM
    scale = d ** -0.5
    q = _rope(_rmsnorm(q.reshape(B, S, H, d), q_gain), pos).reshape(B, S, G, HPG, d)
    kv = kv.reshape(B, S, 2, G, d)
    k = _rope(_rmsnorm(kv[:, :, 0], k_gain), pos)
    v = kv[:, :, 1]
    iq = _rope(iq.reshape(B, S, IDX_HEADS, IDX_DIM), pos)
    ik = _rope(ik[:, :, None, :], pos)[:, :, 0]
    iw = iw.astype(jnp.float32) * (IDX_HEADS ** -0.5 * IDX_DIM ** -0.5)
    k_sel = min(DSA_TOPK, S // 4)
    Q = GATHER_QBLK
    nq = S // Q
    bidx = jnp.arange(B)[:, None, None]
    spos = jnp.arange(S)

    def chunk(t):
        return t.reshape((B, nq, Q) + t.shape[2:]).swapaxes(0, 1)

    def body(args):
        qc, iqc, iwc, tpos = args
        rel = jax.nn.relu(jnp.einsum('bqhe,bse->bqhs', iqc, ik).astype(jnp.float32))
        score = jnp.einsum('bqhs,bqh->bqs', rel, iwc)
        score = jnp.where((spos[None, :] <= tpos[:, None])[None], score, NEG)
        _, idx = lax.top_k(score, k_sel)
        kg = k[bidx, idx]
        vg = v[bidx, idx]
        m = (idx <= tpos[None, :, None])[:, :, None, None, :]
        s = jnp.einsum('bqghd,bqkgd->bqghk', qc, kg).astype(jnp.float32) * scale
        p = _masked_softmax(s, m)
        return jnp.einsum('bqghk,bqkgd->bqghd', p.astype(vg.dtype), vg)

    out = lax.map(body, (chunk(q), chunk(iq), chunk(iw), pos.reshape(nq, Q)))
    return out.swapaxes(0, 1).reshape(B, S, H * d)


def _short_conv(x_in, b_gate, c_gate, w):
    return b_gate * _causal_dwconv(c_gate * x_in, w)


def setup_inputs(seed: int = 0) -> dict:
    key = jax.random.key(seed)
    ks = jax.random.split(key, 24)
    L = DEPTH
    f32 = jnp.float32

    def nrm(k, shape, scale):
        return jax.random.normal(k, shape, f32) * scale

    u = jax.random.uniform(ks[10], (L, GROUP_W), f32, 0.9, 0.999)
    sa = u ** (1.0 / LRU_C)
    lam = jnp.log(sa) - jnp.log1p(-sa)
    blk = GROUP_W // LRU_BLOCKS
    return {
        'x': nrm(ks[0], (BATCH, SEQ, D_MODEL), 1.0),
        'norm_g': 1.0 + nrm(ks[1], (L, D_MODEL), 0.02),
        'w_in': nrm(ks[2], (L, D_MODEL, IN_WIDTH), D_MODEL ** -0.5),
        'w_out': nrm(ks[3], (L, MIX_W, D_MODEL), MIX_W ** -0.5),
        'lru_conv_w': nrm(ks[4], (L, CONV_A, GROUP_W), CONV_A ** -0.5),
        'lru_conv_b': nrm(ks[5], (L, GROUP_W), 0.01),
        'lru_wa': nrm(ks[6], (L, LRU_BLOCKS, blk, blk), blk ** -0.5),
        'lru_ba': nrm(ks[7], (L, GROUP_W), 0.01),
        'lru_wx': nrm(ks[8], (L, LRU_BLOCKS, blk, blk), blk ** -0.5),
        'lru_bx': nrm(ks[9], (L, GROUP_W), 0.01),
        'lru_lambda': lam,
        'nsa_q_gain': 1.0 + nrm(ks[11], (L, HEAD_DIM), 0.02),
        'nsa_k_gain': 1.0 + nrm(ks[12], (L, HEAD_DIM), 0.02),
        'cmp_pe_k': nrm(ks[13], (L, CMP_LEN, HEAD_DIM), 0.1),
        'cmp_w1_k': nrm(ks[14], (L, CMP_LEN, HEAD_DIM, CMP_HID), (CMP_LEN * HEAD_DIM) ** -0.5),
        'cmp_w2_k': nrm(ks[15], (L, CMP_HID, HEAD_DIM), CMP_HID ** -0.5),
        'cmp_pe_v': nrm(ks[16], (L, CMP_LEN, HEAD_DIM), 0.1),
        'cmp_w1_v': nrm(ks[17], (L, CMP_LEN, HEAD_DIM, CMP_HID), (CMP_LEN * HEAD_DIM) ** -0.5),
        'cmp_w2_v': nrm(ks[18], (L, CMP_HID, HEAD_DIM), CMP_HID ** -0.5),
        'dsa_q_gain': 1.0 + nrm(ks[19], (L, HEAD_DIM), 0.02),
        'dsa_k_gain': 1.0 + nrm(ks[20], (L, HEAD_DIM), 0.02),
        'sc_conv_w': nrm(ks[21], (L, CONV_D, GROUP_W), CONV_D ** -0.5),
    }


def reference(x, norm_g, w_in, w_out, lru_conv_w, lru_conv_b, lru_wa, lru_ba, lru_wx, lru_bx, lru_lambda,
              nsa_q_gain, nsa_k_gain, cmp_pe_k, cmp_w1_k, cmp_w2_k, cmp_pe_v, cmp_w1_v, cmp_w2_v,
              dsa_q_gain, dsa_k_gain, sc_conv_w):
    S = x.shape[1]
    pos = jnp.arange(S, dtype=jnp.int32)
    offsets = [int(o) for o in np.cumsum(SPLIT_SIZES)[:-1]]
    for l in range(DEPTH):
        h = _rmsnorm(x, norm_g[l])
        z = h @ w_in[l]
        (a_x, a_g, b_q, b_g, b_kv, b_gate, c_q, c_g, c_kv, c_iq, c_ik, c_iw,
         d_in, d_b, d_c, d_g) = jnp.split(z, offsets, axis=-1)
        y_a = _rglru(a_x, lru_conv_w[l], lru_conv_b[l], lru_wa[l], lru_ba[l], lru_wx[l], lru_bx[l],
                     lru_lambda[l]) * jax.nn.silu(a_g)
        y_b = _nsa(b_q, b_kv, b_gate, pos, nsa_q_gain[l], nsa_k_gain[l], cmp_pe_k[l], cmp_w1_k[l], cmp_w2_k[l],
                   cmp_pe_v[l], cmp_w1_v[l], cmp_w2_v[l]) * jax.nn.silu(b_g)
        y_c = _dsa(c_q, c_kv, c_iq, c_ik, c_iw, pos, dsa_q_gain[l], dsa_k_gain[l]) * jax.nn.silu(c_g)
        y_d = _short_conv(d_in, d_b, d_c, sc_conv_w[l]) * jax.nn.silu(d_g)
        y = jnp.concatenate([y_a, y_b, y_c, y_d], axis=-1)
        x = x + y @ w_out[l]
    return x
```

```python
import functools

import jax
import jax.numpy as jnp
from jax import lax
from jax.experimental import pallas as pl
from jax.experimental.pallas import tpu as pltpu

F32 = jnp.float32
BF16 = jnp.bfloat16
I32 = jnp.int32

D_MODEL = 4096
BATCH = 2
SEQ = 4096
DEPTH = 2
GROUP_W = D_MODEL // 4
HEAD_DIM = 128
N_HEADS = GROUP_W // HEAD_DIM
N_KV = N_HEADS // 4
HPG = N_HEADS // N_KV
ROPE_THETA = 500000.0
NORM_EPS = 1e-6
NEG = -1e30
FORCE = 1e6
LRU_BLOCKS = N_HEADS
LRU_C = 8.0
CONV_A = 4
CONV_D = 3
CMP_LEN = 32
CMP_STRIDE = 16
SLC_BLK = 64
SLC_TOPN = 16
WINDOW = 512
IDX_HEADS = 8
IDX_DIM = 64
DSA_TOPK = 256
SCALE = HEAD_DIM ** -0.5
IW_SCALE = IDX_HEADS ** -0.5 * IDX_DIM ** -0.5

SPLIT_SIZES = (
    GROUP_W, GROUP_W,
    N_HEADS * HEAD_DIM, GROUP_W, 6 * N_KV * HEAD_DIM, 3 * N_HEADS,
    N_HEADS * HEAD_DIM, GROUP_W, 2 * N_KV * HEAD_DIM,
    IDX_HEADS * IDX_DIM, IDX_DIM, IDX_HEADS,
    GROUP_W, GROUP_W, GROUP_W, GROUP_W,
)

LANES = 128
COL_A_X, COL_A_G, COL_B_Q, COL_B_G, COL_C_Q, COL_C_G = 0, 1024, 2048, 3072, 4096, 5120
COL_D_IN, COL_D_B, COL_D_C, COL_D_G = 6144, 7168, 8192, 9216
COL_B_KV, COL_C_KV, COL_C_IQ = 10240, 11776, 12288
COL_B_GATE, COL_C_IK, COL_C_IW = 12800, 12928, 13056
PROJ_W = 13312

N_CMP_PAD = SEQ // CMP_STRIDE
N_SLC = SEQ // SLC_BLK
KV_TILE = 128
M_FLOOR = -1e20

VMEM_LIMIT_BYTES = 56 * 1024 * 1024

TM_IN, TN_IN = 512, 1024
TM_OUT, TN_OUT = 1024, 512
T_TOK = 256
TQ = 256

NT_DIMS = (((1,), (1,)), ((), ()))


def _params(n_axes):
    return pltpu.CompilerParams(dimension_semantics=("arbitrary",) * n_axes,
                                vmem_limit_bytes=VMEM_LIMIT_BYTES)


def _silu(x):
    return x * jax.nn.sigmoid(x)


def _fold8(x):
    parts = [x[r:r + 8] for r in range(0, x.shape[0], 8)]
    while len(parts) > 1:
        nxt = [parts[a] + parts[a + 1] for a in range(0, len(parts) - 1, 2)]
        if len(parts) % 2:
            nxt.append(parts[-1])
        parts = nxt
    return parts[0]


def _inproj_body(x_ref, g_ref, w_ref, o_ref, h_ref):
    @pl.when(pl.program_id(1) == 0)
    def _():
        def rows(r, carry):
            r0 = pl.multiple_of(r * 64, 64)
            x = x_ref[pl.ds(r0, 64), :]
            ms = jnp.mean(x * x, axis=-1, keepdims=True)
            h_ref[pl.ds(r0, 64), :] = ((x * lax.rsqrt(ms + NORM_EPS)) * g_ref[...]).astype(BF16)
            return carry
        lax.fori_loop(0, TM_IN // 64, rows, 0)

    o_ref[...] = jnp.dot(h_ref[...], w_ref[...], preferred_element_type=F32)


def _inproj(x2, g, w):
    m = x2.shape[0]
    return pl.pallas_call(
        _inproj_body,
        grid=(m // TM_IN, PROJ_W // TN_IN),
        in_specs=[pl.BlockSpec((TM_IN, D_MODEL), lambda i, j: (i, 0)),
                  pl.BlockSpec((1, D_MODEL), lambda i, j: (0, 0)),
                  pl.BlockSpec((D_MODEL, TN_IN), lambda i, j: (0, j))],
        out_specs=pl.BlockSpec((TM_IN, TN_IN), lambda i, j: (i, j)),
        out_shape=jax.ShapeDtypeStruct((m, PROJ_W), F32),
        scratch_shapes=[pltpu.VMEM((TM_IN, D_MODEL), BF16)],
        compiler_params=_params(2),
        name="inproj",
    )(x2, g, w)


def _outproj_body(ya, yb, yc, yd, wa, wb, wc, wd, x_ref, o_ref):
    acc = jnp.dot(ya[...], wa[...], preferred_element_type=F32)
    acc = acc + jnp.dot(yb[...], wb[...], preferred_element_type=F32)
    acc = acc + jnp.dot(yc[...], wc[...], preferred_element_type=F32)
    acc = acc + jnp.dot(yd[...], wd[...], preferred_element_type=F32)
    o_ref[...] = x_ref[...] + acc


def _outproj(ys, w, x2):
    m = x2.shape[0]
    yspec = pl.BlockSpec((TM_OUT, GROUP_W), lambda i, j: (i, 0))
    wspecs = [pl.BlockSpec((GROUP_W, TN_OUT), functools.partial(lambda i, j, k: (k, j), k=k)) for k in range(4)]
    xspec = pl.BlockSpec((TM_OUT, TN_OUT), lambda i, j: (i, j))
    return pl.pallas_call(
        _outproj_body,
        grid=(m // TM_OUT, D_MODEL // TN_OUT),
        in_specs=[yspec] * 4 + wspecs + [xspec],
        out_specs=xspec,
        out_shape=jax.ShapeDtypeStruct((m, D_MODEL), F32),
        compiler_params=_params(2),
        name="outproj",
    )(*ys, w, w, w, w, x2)


def _mix_ad_body(ax, ag, din, db, dc, dg, cw, cb, wa, ba, wx, bx, lam, scw,
                 ya, yd, xbuf, vbuf, a_s, b_s, hst):
    t = T_TOK

    @pl.when(pl.program_id(1) == 0)
    def _():
        xbuf[0:8, :] = jnp.zeros((8, GROUP_W), F32)
        vbuf[0:8, :] = jnp.zeros((8, GROUP_W), F32)
        hst[...] = jnp.zeros((8, GROUP_W), F32)

    xbuf[8:8 + t, :] = ax[...]
    for blk in range(LRU_BLOCKS):
        c = slice(blk * LANES, (blk + 1) * LANES)
        u = (cw[0:1, c] * xbuf[5:5 + t, c] + cw[1:2, c] * xbuf[6:6 + t, c]
             + cw[2:3, c] * xbuf[7:7 + t, c] + cw[3:4, c] * xbuf[8:8 + t, c]) + cb[:, c]
        ub = u.astype(BF16)
        r = jax.nn.sigmoid(jnp.dot(ub, wa[blk], preferred_element_type=F32) + ba[:, c])
        ig = jax.nn.sigmoid(jnp.dot(ub, wx[blk], preferred_element_type=F32) + bx[:, c])
        nl = -lam[:, c]
        sp = jnp.maximum(nl, 0.0) + jnp.log1p(jnp.exp(-jnp.abs(nl)))
        log_a = (-LRU_C * r) * sp
        a_s[:, c] = jnp.exp(log_a)
        y2 = 2.0 * log_a
        em1 = jnp.tanh(0.5 * y2) * (jnp.exp(y2) + 1.0)
        b_s[:, c] = jnp.sqrt(-em1) * (ig * u)
    xbuf[0:8, :] = xbuf[t:t + 8, :]

    row = lax.broadcasted_iota(I32, (8, GROUP_W), 0)

    def group(gi, hprev):
        r0 = pl.multiple_of(gi * 8, 8)
        a = a_s[pl.ds(r0, 8), :]
        b = b_s[pl.ds(r0, 8), :]
        for d in (1, 2, 4):
            ok = row >= d
            b = jnp.where(ok, a * pltpu.roll(b, d, axis=0) + b, b)
            a = jnp.where(ok, a * pltpu.roll(a, d, axis=0), a)
        h = a * hprev + b
        b_s[pl.ds(r0, 8), :] = h
        return jnp.broadcast_to(h[7:8, :], (8, GROUP_W))

    hst[...] = lax.fori_loop(0, t // 8, group, hst[...])

    vbuf[8:8 + t, :] = dc[...] * din[...]
    for blk in range(LRU_BLOCKS):
        c = slice(blk * LANES, (blk + 1) * LANES)
        ya[:, c] = (b_s[:, c] * _silu(ag[:, c])).astype(BF16)
        conv = (scw[0:1, c] * vbuf[6:6 + t, c] + scw[1:2, c] * vbuf[7:7 + t, c]
                + scw[2:3, c] * vbuf[8:8 + t, c])
        yd[:, c] = ((db[:, c] * conv) * _silu(dg[:, c])).astype(BF16)
    vbuf[0:8, :] = vbuf[t:t + 8, :]


def _mix_ad(z, cw, cb, wa, ba, wx, bx, lam, scw):
    nt = SEQ // T_TOK

    def zspec(col):
        return pl.BlockSpec((T_TOK, GROUP_W), lambda b, i: (b * nt + i, col // GROUP_W))

    def full(a):
        return pl.BlockSpec(a.shape, lambda b, i: (0,) * a.ndim)

    small = (cw, cb, wa, ba, wx, bx, lam, scw)
    yspec = pl.BlockSpec((T_TOK, GROUP_W), lambda b, i: (b * nt + i, 0))
    return pl.pallas_call(
        _mix_ad_body,
        grid=(BATCH, nt),
        in_specs=[zspec(COL_A_X), zspec(COL_A_G), zspec(COL_D_IN), zspec(COL_D_B), zspec(COL_D_C),
                  zspec(COL_D_G)] + [full(a) for a in small],
        out_specs=[yspec, yspec],
        out_shape=[jax.ShapeDtypeStruct((BATCH * SEQ, GROUP_W), BF16)] * 2,
        scratch_shapes=[pltpu.VMEM((T_TOK + 8, GROUP_W), F32), pltpu.VMEM((T_TOK + 8, GROUP_W), F32),
                        pltpu.VMEM((T_TOK, GROUP_W), F32), pltpu.VMEM((T_TOK, GROUP_W), F32),
                        pltpu.VMEM((8, GROUP_W), F32)],
        compiler_params=_params(2),
        name="mix_ad",
    )(z, z, z, z, z, z, *small)


def _rms_head(x, gain):
    ms = jnp.mean(x * x, axis=-1, keepdims=True)
    return (x * lax.rsqrt(ms + NORM_EPS)) * gain


def _rope(y, cos, nsin, psin, half):
    return y * cos + pltpu.roll(y, LANES - half, axis=1) * nsin + pltpu.roll(y, half, axis=1) * psin


def _store_vt(dst, g, v):
    for cc in range(T_TOK // KV_TILE):
        dst[0, g, cc] = v[cc * KV_TILE:(cc + 1) * KV_TILE, :].T.astype(BF16)


def _prep_body(bq, bkv_s, bkv_w, cq, ckv, ciq, cik, c128, n128, p128, c64, n64, p64,
               qgn, kgn, qgd, kgd,
               qn_o, qr_o, ks_o, vst_o, kw_o, vwt_o, qd_o, kd_o, vdt_o, iq_o, ik_o):
    r128 = (c128[...], n128[...], p128[...], HEAD_DIM // 8)
    r64 = (c64[...], n64[...], p64[...], IDX_DIM // 8)
    for h in range(N_HEADS):
        c = slice(h * LANES, (h + 1) * LANES)
        y = _rms_head(bq[:, c], qgn[...])
        qn_o[:, c] = y.astype(BF16)
        qr_o[:, c] = _rope(y, *r128).astype(BF16)
        qd_o[:, c] = _rope(_rms_head(cq[:, c], qgd[...]), *r128).astype(BF16)
    for g in range(N_KV):
        ck = slice(g * LANES, (g + 1) * LANES)
        cv = slice((N_KV + g) * LANES, (N_KV + g + 1) * LANES)
        ks_o[0, g] = _rope(_rms_head(bkv_s[:, ck], kgn[...]), *r128).astype(BF16)
        _store_vt(vst_o, g, bkv_s[:, cv])
        kw_o[0, g] = _rope(_rms_head(bkv_w[:, ck], kgn[...]), *r128).astype(BF16)
        _store_vt(vwt_o, g, bkv_w[:, cv])
        kd_o[0, g] = _rope(_rms_head(ckv[:, ck], kgd[...]), *r128).astype(BF16)
        _store_vt(vdt_o, g, ckv[:, cv])

    low = lax.broadcasted_iota(I32, (T_TOK, LANES), 1) < IDX_DIM
    ik = _rope(cik[...], *r64)
    hi = ik.astype(BF16).astype(F32)
    ik_o[0, :, 0:LANES] = (hi + pltpu.roll(ik - hi, IDX_DIM, axis=1)).astype(BF16)
    ik_o[0, :, LANES:2 * LANES] = hi.astype(BF16)
    for v in range(IDX_HEADS // 2):
        x = _rope(ciq[:, v * LANES:(v + 1) * LANES], *r64)
        for half in range(2):
            h = 2 * v + half
            xs = x if half == 0 else pltpu.roll(x, IDX_DIM, axis=1)
            xh = jnp.where(low, xs, 0.0)
            hi = xh.astype(BF16).astype(F32)
            iq_o[:, 2 * h * LANES:(2 * h + 1) * LANES] = (hi + pltpu.roll(hi, IDX_DIM, axis=1)).astype(BF16)
            iq_o[:, (2 * h + 1) * LANES:(2 * h + 2) * LANES] = (xh - hi).astype(BF16)


def _prep(z, tabs128, tabs64, qgn, kgn, qgd, kgd):
    nt = SEQ // T_TOK

    def zspec(col, width):
        return pl.BlockSpec((T_TOK, width), lambda b, i: (b * nt + i, col // width))

    tab = pl.BlockSpec((T_TOK, LANES), lambda b, i: (i, 0))
    gain = pl.BlockSpec((1, LANES), lambda b, i: (0, 0))
    qspec = pl.BlockSpec((T_TOK, GROUP_W), lambda b, i: (b * nt + i, 0))
    kspec = pl.BlockSpec((1, N_KV, T_TOK, LANES), lambda b, i: (b, 0, i, 0))
    vtspec = pl.BlockSpec((1, N_KV, T_TOK // KV_TILE, LANES, KV_TILE), lambda b, i: (b, 0, i, 0, 0))
    qshape = jax.ShapeDtypeStruct((BATCH * SEQ, GROUP_W), BF16)
    kshape = jax.ShapeDtypeStruct((BATCH, N_KV, SEQ, LANES), BF16)
    vtshape = jax.ShapeDtypeStruct((BATCH, N_KV, SEQ // KV_TILE, LANES, KV_TILE), BF16)
    return pl.pallas_call(
        _prep_body,
        grid=(BATCH, nt),
        in_specs=[zspec(COL_B_Q, GROUP_W), zspec(COL_B_KV + 512, 512), zspec(COL_B_KV + 1024, 512),
                  zspec(COL_C_Q, GROUP_W), zspec(COL_C_KV, 512), zspec(COL_C_IQ, 512), zspec(COL_C_IK, LANES)]
                 + [tab] * 6 + [gain] * 4,
        out_specs=[qspec, qspec, kspec, vtspec, kspec, vtspec, qspec, kspec, vtspec,
                   pl.BlockSpec((T_TOK, 2 * LANES * IDX_HEADS), lambda b, i: (b * nt + i, 0)),
                   pl.BlockSpec((1, T_TOK, 2 * LANES), lambda b, i: (b, i, 0))],
        out_shape=[qshape, qshape, kshape, vtshape, kshape, vtshape, qshape, kshape, vtshape,
                   jax.ShapeDtypeStruct((BATCH * SEQ, 2 * LANES * IDX_HEADS), BF16),
                   jax.ShapeDtypeStruct((BATCH, SEQ, 2 * LANES), BF16)],
        compiler_params=_params(2),
        name="attn_prep",
    )(z, z, z, z, z, z, z, *tabs128, *tabs64, qgn, kgn, qgd, kgd)


def _cmp_body(zc, pe, w1, w2, kg, o, ot, kbuf):
    kbuf[0:SEQ, :] = zc[...]
    kbuf[SEQ:SEQ + LANES, :] = jnp.zeros((LANES, LANES), F32)
    acc = jnp.zeros((N_CMP_PAD, LANES), F32)
    for l in range(CMP_LEN):
        rows = kbuf[pl.ds(l, N_CMP_PAD, stride=CMP_STRIDE), :] + pe[0, l:l + 1, :]
        acc = acc + jnp.dot(rows.astype(BF16), w1[0, l], preferred_element_type=F32)
    out = jnp.dot(_silu(acc).astype(BF16), w2[0], preferred_element_type=F32)
    res = jnp.where(pl.program_id(1) == 0, _rms_head(out, kg[...]), out)
    o[0, 0, 0] = res.astype(BF16)
    ot[0, 0, 0] = res.T.astype(BF16)


def _compress(z, pe, w1, w2, kg):
    base = COL_B_KV // LANES
    return pl.pallas_call(
        _cmp_body,
        grid=(BATCH, 2, N_KV),
        in_specs=[pl.BlockSpec((SEQ, LANES), lambda b, kv, g: (b, base + kv * N_KV + g)),
                  pl.BlockSpec((1, CMP_LEN, LANES), lambda b, kv, g: (kv, 0, 0)),
                  pl.BlockSpec((1, CMP_LEN, LANES, LANES), lambda b, kv, g: (kv, 0, 0, 0)),
                  pl.BlockSpec((1, LANES, LANES), lambda b, kv, g: (kv, 0, 0)),
                  pl.BlockSpec((1, LANES), lambda b, kv, g: (0, 0))],
        out_specs=[pl.BlockSpec((1, 1, 1, N_CMP_PAD, LANES), lambda b, kv, g: (b, kv, g, 0, 0)),
                   pl.BlockSpec((1, 1, 1, LANES, N_CMP_PAD), lambda b, kv, g: (b, kv, g, 0, 0))],
        out_shape=[jax.ShapeDtypeStruct((BATCH, 2, N_KV, N_CMP_PAD, LANES), BF16),
                   jax.ShapeDtypeStruct((BATCH, 2, N_KV, LANES, N_CMP_PAD), BF16)],
        scratch_shapes=[pltpu.VMEM((SEQ + LANES, LANES), F32)],
        compiler_params=_params(3),
        name="nsa_compress",
    )(z, pe, w1, w2, kg)


def _flash_t(q, k_ref, vt_ref, jlo, jhi, bias_fn):
    tq = q.shape[0]

    def body(j, carry):
        m, l, acc = carry
        r0 = pl.multiple_of(j * KV_TILE, KV_TILE)
        k = k_ref[pl.ds(r0, KV_TILE), :]
        s = lax.dot_general(k, q, NT_DIMS, preferred_element_type=F32) * SCALE + bias_fn(j)
        m_new = jnp.maximum(m, jnp.max(s, axis=0, keepdims=True))
        m_safe = jnp.maximum(m_new, M_FLOOR)
        alpha = jnp.exp(m - m_safe)
        p = jnp.exp(s - m_safe)
        l = alpha * l + jnp.sum(p, axis=0, keepdims=True)
        acc = alpha * acc + jnp.dot(vt_ref[j], p.astype(BF16), preferred_element_type=F32)
        return m_new, l, acc

    init = (jnp.full((1, tq), NEG, F32), jnp.zeros((1, tq), F32), jnp.zeros((HEAD_DIM, tq), F32))
    _, l, acc = lax.fori_loop(jlo, jhi, body, init)
    return acc * (1.0 / l)


def _nsa_body(qn, qr, kc, vct, ks, vst, kw, vwt, gate, bg, ovt, yb, imp_s, bias_s, oc_s):
    i = pl.program_id(1)
    tq = i * TQ + lax.broadcasted_iota(I32, (1, TQ), 1)
    n_kv = 2 * i + 2
    sg = jax.nn.sigmoid(gate[...].T[0:3 * N_HEADS, :])
    cm = (lax.broadcasted_iota(I32, (N_CMP_PAD, TQ), 0) * CMP_STRIDE + (CMP_LEN - 1)) <= tq
    kk = lax.broadcasted_iota(I32, (N_SLC, TQ), 0)
    blk_t = lax.shift_right_logical(tq, 6)
    valid = kk <= blk_t
    force = (kk == 0) | (kk == blk_t) | (kk == blk_t - 1)
    row128 = lax.broadcasted_iota(I32, (KV_TILE, TQ), 0)

    def win_bias(j):
        d = tq - (j * KV_TILE + row128)
        return jnp.where((d >= 0) & (d < WINDOW), 0.0, NEG)

    def slc_bias(j):
        return bias_s[pl.ds(pl.multiple_of(j * KV_TILE, KV_TILE), KV_TILE), :]

    for g in range(N_KV):
        kcg = kc[0, 0, g]
        vcg = vct[0, 0, g]
        psum = jnp.zeros((N_CMP_PAD, TQ), F32)
        for hh in range(HPG):
            h = HPG * g + hh
            s = lax.dot_general(kcg, qn[:, h * LANES:(h + 1) * LANES], NT_DIMS,
                                preferred_element_type=F32) * SCALE
            s = jnp.where(cm, s, NEG)
            e = jnp.exp(s - jnp.max(s, axis=0, keepdims=True))
            p = jnp.where(cm, e * (1.0 / jnp.sum(e, axis=0, keepdims=True)), 0.0)
            psum = psum + p
            oc_s[hh] = jnp.dot(vcg, p.astype(BF16), preferred_element_type=F32)

        p_hi = psum.astype(BF16)
        p_lo = (psum - p_hi.astype(F32)).astype(BF16)
        imp = (jnp.dot(ovt[...], p_hi, preferred_element_type=F32)
               + jnp.dot(ovt[...], p_lo, preferred_element_type=F32))
        imp = jnp.where(force, FORCE, jnp.where(valid, imp, -FORCE))
        imp_s[...] = imp
        rank = jnp.zeros((N_SLC, TQ), F32)
        for j in range(N_SLC):
            vj = imp_s[j:j + 1, :]
            rank = rank + jnp.where((vj > imp) | ((vj == imp) & (kk > j)), 1.0, 0.0)
        imp_s[...] = jnp.where(rank < SLC_TOPN, 0.0, NEG)

        def fill(j, carry):
            r0 = pl.multiple_of(j * KV_TILE, KV_TILE)
            b0 = jnp.broadcast_to(imp_s[pl.ds(2 * j, 1), :], (SLC_BLK, TQ))
            b1 = jnp.broadcast_to(imp_s[pl.ds(2 * j + 1, 1), :], (SLC_BLK, TQ))
            tile = jnp.concatenate([b0, b1], axis=0)
            bias_s[pl.ds(r0, KV_TILE), :] = jnp.where(r0 + row128 <= tq, tile, NEG)
            return carry
        lax.fori_loop(0, n_kv, fill, 0)

        for hh in range(HPG):
            h = HPG * g + hh
            c = slice(h * LANES, (h + 1) * LANES)
            q = qr[:, c]
            o_slc = _flash_t(q, ks.at[0, g], vst.at[0, g], 0, n_kv, slc_bias)
            o_win = _flash_t(q, kw.at[0, g], vwt.at[0, g], jnp.maximum(n_kv - 6, 0), n_kv, win_bias)
            tot = (sg[h:h + 1] * oc_s[hh] + sg[N_HEADS + h:N_HEADS + h + 1] * o_slc
                   + sg[2 * N_HEADS + h:2 * N_HEADS + h + 1] * o_win)
            yb[:, c] = (tot.T * _silu(bg[:, c])).astype(BF16)


def _nsa(z, qn, qr, cmp_, cmpt, ks, vst, kw, vwt, ovt):
    nq = SEQ // TQ
    qspec = pl.BlockSpec((TQ, GROUP_W), lambda b, i: (b * nq + i, 0))
    kspec = pl.BlockSpec((1, N_KV, SEQ, LANES), lambda b, i: (b, 0, 0, 0))
    vtspec = pl.BlockSpec((1, N_KV, SEQ // KV_TILE, LANES, KV_TILE), lambda b, i: (b, 0, 0, 0, 0))
    return pl.pallas_call(
        _nsa_body,
        grid=(BATCH, nq),
        in_specs=[qspec, qspec,
                  pl.BlockSpec((1, 1, N_KV, N_CMP_PAD, LANES), lambda b, i: (b, 0, 0, 0, 0)),
                  pl.BlockSpec((1, 1, N_KV, LANES, N_CMP_PAD), lambda b, i: (b, 1, 0, 0, 0)),
                  kspec, vtspec, kspec, vtspec,
                  pl.BlockSpec((TQ, LANES), lambda b, i: (b * nq + i, COL_B_GATE // LANES)),
                  pl.BlockSpec((TQ, GROUP_W), lambda b, i: (b * nq + i, COL_B_G // GROUP_W)),
                  pl.BlockSpec((N_SLC, N_CMP_PAD), lambda b, i: (0, 0))],
        out_specs=qspec,
        out_shape=jax.ShapeDtypeStruct((BATCH * SEQ, GROUP_W), BF16),
        scratch_shapes=[pltpu.VMEM((N_SLC, TQ), F32), pltpu.VMEM((SEQ, TQ), F32),
                        pltpu.VMEM((HPG, HEAD_DIM, TQ), F32)],
        compiler_params=_params(2),
        name="nsa_attn",
    )(qn, qr, cmp_, cmpt, ks, vst, kw, vwt, z, z, ovt)


def _dsa_body(qd, kd, vdt, iq, ik, iw, cg, yc, sc_s):
    i = pl.program_id(1)
    tq = i * TQ + lax.broadcasted_iota(I32, (1, TQ), 1)
    n_kv = 2 * i + 2
    n_ch = i + 1
    iwt = iw[...].T[0:IDX_HEADS, :] * IW_SCALE
    row128 = lax.broadcasted_iota(I32, (KV_TILE, TQ), 0)
    row256 = lax.broadcasted_iota(I32, (2 * KV_TILE, TQ), 0)
    kf = float(DSA_TOPK)

    def fill(j, carry):
        mn, mx = carry
        r0 = pl.multiple_of(j * KV_TILE, KV_TILE)
        ikc = ik[0, pl.ds(r0, KV_TILE), :]
        sc = jnp.zeros((KV_TILE, TQ), F32)
        for h in range(IDX_HEADS):
            r = lax.dot_general(ikc, iq[:, 2 * h * LANES:(2 * h + 2) * LANES], NT_DIMS,
                                preferred_element_type=F32)
            sc = sc + jnp.maximum(r, 0.0) * iwt[h:h + 1, :]
        ok = r0 + row128 <= tq
        sc_s[pl.ds(r0, KV_TILE), :] = jnp.where(ok, sc, -jnp.inf)
        mn = jnp.minimum(mn, jnp.min(jnp.where(ok, sc, jnp.inf), axis=0, keepdims=True))
        mx = jnp.maximum(mx, jnp.max(jnp.where(ok, sc, -jnp.inf), axis=0, keepdims=True))
        return mn, mx

    lo0, hi0 = lax.fori_loop(0, n_kv, fill,
                             (jnp.full((1, TQ), jnp.inf, F32), jnp.full((1, TQ), -jnp.inf, F32)))

    def sweep(fn, n_out):
        def body(c, accs):
            r0 = pl.multiple_of(c * 2 * KV_TILE, 2 * KV_TILE)
            inds = fn(sc_s[pl.ds(r0, 2 * KV_TILE), :], r0 + row256)
            return tuple(a + _fold8(jnp.where(ind, 1.0, 0.0)) for a, ind in zip(accs, inds))
        accs = lax.fori_loop(0, n_ch, body, tuple(jnp.zeros((8, TQ), F32) for _ in range(n_out)))
        return tuple(jnp.sum(a, axis=0, keepdims=True) for a in accs)

    nvalid = (tq + 1).astype(F32)
    act0 = jnp.where(nvalid > kf, 1.0, 0.0)

    def cond(c):
        return c[0] > 0.0

    def step(c):
        _, lo, hi, clo, act = c
        mid = lo + (hi - lo) * 0.5
        exh = (mid <= lo) | (mid >= hi)
        p = jnp.where(exh, hi, mid)
        (cnt,) = sweep(lambda x, srow: (x >= p,), 1)
        on = act > 0.0
        up = on & (cnt >= kf)
        dn = on & (cnt < kf)
        lo = jnp.where(up, p, lo)
        hi = jnp.where(dn, p, hi)
        clo = jnp.where(up, cnt, clo)
        act = jnp.where(on & (clo != kf) & jnp.logical_not(exh), 1.0, 0.0)
        return jnp.sum(act), lo, hi, clo, act

    _, thr, _, _, _ = lax.while_loop(cond, step, (jnp.sum(act0), lo0, hi0, nvalid, act0))

    cnt_g, cnt_e = sweep(lambda x, srow: (x > thr, x == thr), 2)
    need = kf - cnt_g
    excess = (cnt_g + cnt_e) > kf

    def tie_break():
        def it(_, c):
            jl, jh = c
            jm = lax.shift_right_arithmetic(jl + jh, 1)
            (cnt,) = sweep(lambda x, srow: ((x == thr) & (srow <= jm),), 1)
            ge = cnt >= need
            return jnp.where(ge, jl, jm), jnp.where(ge, jm, jh)
        _, jh = lax.fori_loop(0, 12, it, (jnp.full((1, TQ), -1, I32), jnp.full((1, TQ), SEQ - 1, I32)))
        return jnp.where(excess, jh, SEQ)

    last = lax.cond(jnp.sum(jnp.where(excess, 1.0, 0.0)) > 0.0, tie_break,
                    lambda: jnp.full((1, TQ), SEQ, I32))

    def to_bias(c, carry):
        r0 = pl.multiple_of(c * 2 * KV_TILE, 2 * KV_TILE)
        x = sc_s[pl.ds(r0, 2 * KV_TILE), :]
        sel = (x > thr) | ((x == thr) & (r0 + row256 <= last))
        sc_s[pl.ds(r0, 2 * KV_TILE), :] = jnp.where(sel, 0.0, NEG)
        return carry
    lax.fori_loop(0, n_ch, to_bias, 0)

    def bias(j):
        return sc_s[pl.ds(pl.multiple_of(j * KV_TILE, KV_TILE), KV_TILE), :]

    for h in range(N_HEADS):
        g = h // HPG
        c = slice(h * LANES, (h + 1) * LANES)
        o = _flash_t(qd[:, c], kd.at[0, g], vdt.at[0, g], 0, n_kv, bias)
        yc[:, c] = (o.T * _silu(cg[:, c])).astype(BF16)


def _dsa(z, qd, kd, vdt, iq, ik):
    nq = SEQ // TQ
    qspec = pl.BlockSpec((TQ, GROUP_W), lambda b, i: (b * nq + i, 0))
    return pl.pallas_call(
        _dsa_body,
        grid=(BATCH, nq),
        in_specs=[qspec,
                  pl.BlockSpec((1, N_KV, SEQ, LANES), lambda b, i: (b, 0, 0, 0)),
                  pl.BlockSpec((1, N_KV, SEQ // KV_TILE, LANES, KV_TILE), lambda b, i: (b, 0, 0, 0, 0)),
                  pl.BlockSpec((TQ, 2 * LANES * IDX_HEADS), lambda b, i: (b * nq + i, 0)),
                  pl.BlockSpec((1, SEQ, 2 * LANES), lambda b, i: (b, 0, 0)),
                  pl.BlockSpec((TQ, LANES), lambda b, i: (b * nq + i, COL_C_IW // LANES)),
                  pl.BlockSpec((TQ, GROUP_W), lambda b, i: (b * nq + i, COL_C_G // GROUP_W))],
        out_specs=qspec,
        out_shape=jax.ShapeDtypeStruct((BATCH * SEQ, GROUP_W), BF16),
        scratch_shapes=[pltpu.VMEM((SEQ, TQ), F32)],
        compiler_params=_params(2),
        name="dsa_attn",
    )(qd, kd, vdt, iq, ik, z, z)


def _proj_weight(w):
    offs = []
    o = 0
    for s in SPLIT_SIZES:
        offs.append((o, s))
        o += s
    (a_x, a_g, b_q, b_g, b_kv, b_gate, c_q, c_g, c_kv, c_iq, c_ik, c_iw,
     d_in, d_b, d_c, d_g) = [w[:, o:o + s] for o, s in offs]

    def pad(s):
        return jnp.pad(s, ((0, 0), (0, LANES - s.shape[1])))

    cols = [a_x, a_g, b_q, b_g, c_q, c_g, d_in, d_b, d_c, d_g, b_kv, c_kv, c_iq,
            pad(b_gate), pad(c_ik), pad(c_iw), jnp.zeros((w.shape[0], PROJ_W - COL_C_IW - LANES), w.dtype)]
    return jnp.concatenate(cols, axis=1).astype(BF16)


def _rope_tables(head_dim):
    r = head_dim // 4
    half = r // 2
    pos = jnp.arange(SEQ, dtype=jnp.int32)
    inv = ROPE_THETA ** (-jnp.arange(half, dtype=F32) * 2.0 / r)
    ang = pos.astype(F32)[:, None] * inv[None, :]
    cos, sin = jnp.cos(ang), jnp.sin(ang)
    one = jnp.ones((SEQ, head_dim - r), F32)
    zero = jnp.zeros((SEQ, head_dim - r), F32)
    zh = jnp.zeros((SEQ, half), F32)
    reps = LANES // head_dim
    return tuple(jnp.tile(t, (1, reps)) for t in (
        jnp.concatenate([cos, cos, one], axis=1),
        jnp.concatenate([-sin, zh, zero], axis=1),
        jnp.concatenate([zh, sin, zero], axis=1)))


def _overlap_t():
    cs = jnp.arange(N_CMP_PAD) * CMP_STRIDE
    ss = jnp.arange(N_SLC) * SLC_BLK
    ov = jnp.clip(jnp.minimum(cs[None, :] + CMP_LEN, ss[:, None] + SLC_BLK)
                  - jnp.maximum(cs[None, :], ss[:, None]), 0, None).astype(F32) / CMP_LEN
    return jnp.where(jnp.arange(N_CMP_PAD)[None, :] < N_CMP_PAD - 1, ov, 0.0).astype(BF16)


def kernel(x, norm_g, w_in, w_out, lru_conv_w, lru_conv_b, lru_wa, lru_ba, lru_wx, lru_bx, lru_lambda,
           nsa_q_gain, nsa_k_gain, cmp_pe_k, cmp_w1_k, cmp_w2_k, cmp_pe_v, cmp_w1_v, cmp_w2_v,
           dsa_q_gain, dsa_k_gain, sc_conv_w):
    b, s, d = x.shape
    assert (b, s, d) == (BATCH, SEQ, D_MODEL)
    tabs128 = _rope_tables(HEAD_DIM)
    tabs64 = _rope_tables(IDX_DIM)
    ovt = _overlap_t()
    x2 = x.reshape(b * s, d)

    def row(v):
        return v.reshape(1, -1)

    for l in range(DEPTH):
        z = _inproj(x2, row(norm_g[l]), _proj_weight(w_in[l]))
        ya, yd = _mix_ad(z, lru_conv_w[l], row(lru_conv_b[l]), lru_wa[l].astype(BF16), row(lru_ba[l]),
                         lru_wx[l].astype(BF16), row(lru_bx[l]), row(lru_lambda[l]), sc_conv_w[l])
        qn, qr, ks, vst, kw, vwt, qd, kd, vdt, iq, ik = _prep(
            z, tabs128, tabs64, row(nsa_q_gain[l]), row(nsa_k_gain[l]), row(dsa_q_gain[l]), row(dsa_k_gain[l]))
        cmp_, cmpt = _compress(z, jnp.stack([cmp_pe_k[l], cmp_pe_v[l]]),
                               jnp.stack([cmp_w1_k[l], cmp_w1_v[l]]).astype(BF16),
                               jnp.stack([cmp_w2_k[l], cmp_w2_v[l]]).astype(BF16), row(nsa_k_gain[l]))
        yb = _nsa(z, qn, qr, cmp_, cmpt, ks, vst, kw, vwt, ovt)
        yc = _dsa(z, qd, kd, vdt, iq, ik)
        x2 = _outproj((ya, yb, yc, yd), w_out[l].astype(BF16), x2)
    return x2.reshape(b, s, d)
```

```python
import functools

import jax
import jax.numpy as jnp
from jax import lax
from jax.experimental import pallas as pl
from jax.experimental.pallas import tpu as pltpu

F32 = jnp.float32
BF16 = jnp.bfloat16
I32 = jnp.int32

D_MODEL = 4096
BATCH = 2
SEQ = 4096
DEPTH = 2
GROUP_W = D_MODEL // 4
HEAD_DIM = 128
N_HEADS = GROUP_W // HEAD_DIM
N_KV = N_HEADS // 4
HPG = N_HEADS // N_KV
ROPE_THETA = 500000.0
NORM_EPS = 1e-6
NEG = -1e30
FORCE = 1e6
LRU_BLOCKS = N_HEADS
LRU_C = 8.0
CONV_A = 4
CONV_D = 3
CMP_LEN = 32
CMP_STRIDE = 16
SLC_BLK = 64
SLC_TOPN = 16
WINDOW = 512
IDX_HEADS = 8
IDX_DIM = 64
DSA_TOPK = 256
SCALE = HEAD_DIM ** -0.5
Q_PRESCALE = SCALE * 1.4426950408889634
IW_SCALE = IDX_HEADS ** -0.5 * IDX_DIM ** -0.5

SPLIT_SIZES = (
    GROUP_W, GROUP_W,
    N_HEADS * HEAD_DIM, GROUP_W, 6 * N_KV * HEAD_DIM, 3 * N_HEADS,
    N_HEADS * HEAD_DIM, GROUP_W, 2 * N_KV * HEAD_DIM,
    IDX_HEADS * IDX_DIM, IDX_DIM, IDX_HEADS,
    GROUP_W, GROUP_W, GROUP_W, GROUP_W,
)

LANES = 128
COL_A_X, COL_A_G, COL_B_Q, COL_B_G, COL_C_Q, COL_C_G = 0, 1024, 2048, 3072, 4096, 5120
COL_D_IN, COL_D_B, COL_D_C, COL_D_G = 6144, 7168, 8192, 9216
COL_B_KV, COL_C_KV, COL_C_IQ = 10240, 11776, 12288
COL_B_GATE, COL_C_IK, COL_C_IW = 12800, 12928, 13056
PROJ_W = 13312

N_CMP_PAD = SEQ // CMP_STRIDE
N_SLC = SEQ // SLC_BLK
KV_TILE = 256
M_FLOOR = -1e20

VMEM_LIMIT_BYTES = 56 * 1024 * 1024

TM_IN, TN_IN = 512, 1024
TM_OUT, TN_OUT = 1024, 512
T_TOK = 256
TQ = 256

NT_DIMS = (((1,), (1,)), ((), ()))


def _params(n_axes):
    return pltpu.CompilerParams(dimension_semantics=("arbitrary",) * n_axes,
                                vmem_limit_bytes=VMEM_LIMIT_BYTES)


def _silu(x):
    return x * jax.nn.sigmoid(x)


def _fold8(x):
    parts = [x[r:r + 8] for r in range(0, x.shape[0], 8)]
    while len(parts) > 1:
        nxt = [parts[a] + parts[a + 1] for a in range(0, len(parts) - 1, 2)]
        if len(parts) % 2:
            nxt.append(parts[-1])
        parts = nxt
    return parts[0]


def _inproj_body(x_ref, g_ref, w_ref, o_ref, h_ref):
    @pl.when(pl.program_id(1) == 0)
    def _():
        def rows(r, carry):
            r0 = pl.multiple_of(r * 64, 64)
            x = x_ref[pl.ds(r0, 64), :]
            ms = jnp.mean(x * x, axis=-1, keepdims=True)
            h_ref[pl.ds(r0, 64), :] = ((x * lax.rsqrt(ms + NORM_EPS)) * g_ref[...]).astype(BF16)
            return carry
        lax.fori_loop(0, TM_IN // 64, rows, 0)

    o_ref[...] = jnp.dot(h_ref[...], w_ref[...], preferred_element_type=F32)


def _inproj(x2, g, w):
    m = x2.shape[0]
    return pl.pallas_call(
        _inproj_body,
        grid=(m // TM_IN, PROJ_W // TN_IN),
        in_specs=[pl.BlockSpec((TM_IN, D_MODEL), lambda i, j: (i, 0)),
                  pl.BlockSpec((1, D_MODEL), lambda i, j: (0, 0)),
                  pl.BlockSpec((D_MODEL, TN_IN), lambda i, j: (0, j))],
        out_specs=pl.BlockSpec((TM_IN, TN_IN), lambda i, j: (i, j)),
        out_shape=jax.ShapeDtypeStruct((m, PROJ_W), F32),
        scratch_shapes=[pltpu.VMEM((TM_IN, D_MODEL), BF16)],
        compiler_params=_params(2),
        name="inproj",
    )(x2, g, w)


def _outproj_body(ya, yb, yc, yd, wa, wb, wc, wd, x_ref, o_ref):
    acc = jnp.dot(ya[...], wa[...], preferred_element_type=F32)
    acc = acc + jnp.dot(yb[...], wb[...], preferred_element_type=F32)
    acc = acc + jnp.dot(yc[...], wc[...], preferred_element_type=F32)
    acc = acc + jnp.dot(yd[...], wd[...], preferred_element_type=F32)
    o_ref[...] = x_ref[...] + acc


def _outproj(ys, w, x2):
    m = x2.shape[0]
    yspec = pl.BlockSpec((TM_OUT, GROUP_W), lambda i, j: (i, 0))
    wspecs = [pl.BlockSpec((GROUP_W, TN_OUT), functools.partial(lambda i, j, k: (k, j), k=k)) for k in range(4)]
    xspec = pl.BlockSpec((TM_OUT, TN_OUT), lambda i, j: (i, j))
    return pl.pallas_call(
        _outproj_body,
        grid=(m // TM_OUT, D_MODEL // TN_OUT),
        in_specs=[yspec] * 4 + wspecs + [xspec],
        out_specs=xspec,
        out_shape=jax.ShapeDtypeStruct((m, D_MODEL), F32),
        compiler_params=_params(2),
        name="outproj",
    )(*ys, w, w, w, w, x2)


def _mix_ad_body(ax, ag, din, db, dc, dg, cw, cb, wa, ba, wx, bx, lam, scw,
                 ya, yd, xbuf, vbuf, a_s, b_s, hst):
    t = T_TOK

    @pl.when(pl.program_id(1) == 0)
    def _():
        xbuf[0:8, :] = jnp.zeros((8, GROUP_W), F32)
        vbuf[0:8, :] = jnp.zeros((8, GROUP_W), F32)
        hst[...] = jnp.zeros((8, GROUP_W), F32)

    xbuf[8:8 + t, :] = ax[...]
    for blk in range(LRU_BLOCKS):
        c = slice(blk * LANES, (blk + 1) * LANES)
        u = (cw[0:1, c] * xbuf[5:5 + t, c] + cw[1:2, c] * xbuf[6:6 + t, c]
             + cw[2:3, c] * xbuf[7:7 + t, c] + cw[3:4, c] * xbuf[8:8 + t, c]) + cb[:, c]
        ub = u.astype(BF16)
        r = jax.nn.sigmoid(jnp.dot(ub, wa[blk], preferred_element_type=F32) + ba[:, c])
        ig = jax.nn.sigmoid(jnp.dot(ub, wx[blk], preferred_element_type=F32) + bx[:, c])
        nl = -lam[:, c]
        sp = jnp.maximum(nl, 0.0) + jnp.log1p(jnp.exp(-jnp.abs(nl)))
        log_a = (-LRU_C * r) * sp
        a_s[:, c] = jnp.exp(log_a)
        y2 = 2.0 * log_a
        em1 = jnp.tanh(0.5 * y2) * (jnp.exp(y2) + 1.0)
        b_s[:, c] = jnp.sqrt(-em1) * (ig * u)
    xbuf[0:8, :] = xbuf[t:t + 8, :]

    row = lax.broadcasted_iota(I32, (8, GROUP_W), 0)

    def group(gi, hprev):
        r0 = pl.multiple_of(gi * 8, 8)
        a = a_s[pl.ds(r0, 8), :]
        b = b_s[pl.ds(r0, 8), :]
        for d in (1, 2, 4):
            ok = row >= d
            b = jnp.where(ok, a * pltpu.roll(b, d, axis=0) + b, b)
            a = jnp.where(ok, a * pltpu.roll(a, d, axis=0), a)
        h = a * hprev + b
        b_s[pl.ds(r0, 8), :] = h
        return jnp.broadcast_to(h[7:8, :], (8, GROUP_W))

    hst[...] = lax.fori_loop(0, t // 8, group, hst[...])

    vbuf[8:8 + t, :] = dc[...] * din[...]
    for blk in range(LRU_BLOCKS):
        c = slice(blk * LANES, (blk + 1) * LANES)
        ya[:, c] = (b_s[:, c] * _silu(ag[:, c])).astype(BF16)
        conv = (scw[0:1, c] * vbuf[6:6 + t, c] + scw[1:2, c] * vbuf[7:7 + t, c]
                + scw[2:3, c] * vbuf[8:8 + t, c])
        yd[:, c] = ((db[:, c] * conv) * _silu(dg[:, c])).astype(BF16)
    vbuf[0:8, :] = vbuf[t:t + 8, :]


def _mix_ad(z, cw, cb, wa, ba, wx, bx, lam, scw):
    nt = SEQ // T_TOK

    def zspec(col):
        return pl.BlockSpec((T_TOK, GROUP_W), lambda b, i: (b * nt + i, col // GROUP_W))

    def full(a):
        return pl.BlockSpec(a.shape, lambda b, i: (0,) * a.ndim)

    small = (cw, cb, wa, ba, wx, bx, lam, scw)
    yspec = pl.BlockSpec((T_TOK, GROUP_W), lambda b, i: (b * nt + i, 0))
    return pl.pallas_call(
        _mix_ad_body,
        grid=(BATCH, nt),
        in_specs=[zspec(COL_A_X), zspec(COL_A_G), zspec(COL_D_IN), zspec(COL_D_B), zspec(COL_D_C),
                  zspec(COL_D_G)] + [full(a) for a in small],
        out_specs=[yspec, yspec],
        out_shape=[jax.ShapeDtypeStruct((BATCH * SEQ, GROUP_W), BF16)] * 2,
        scratch_shapes=[pltpu.VMEM((T_TOK + 8, GROUP_W), F32), pltpu.VMEM((T_TOK + 8, GROUP_W), F32),
                        pltpu.VMEM((T_TOK, GROUP_W), F32), pltpu.VMEM((T_TOK, GROUP_W), F32),
                        pltpu.VMEM((8, GROUP_W), F32)],
        compiler_params=_params(2),
        name="mix_ad",
    )(z, z, z, z, z, z, *small)


def _rms_head(x, gain):
    ms = jnp.mean(x * x, axis=-1, keepdims=True)
    return (x * lax.rsqrt(ms + NORM_EPS)) * gain


def _rope(y, cos, nsin, psin, half):
    return y * cos + pltpu.roll(y, LANES - half, axis=1) * nsin + pltpu.roll(y, half, axis=1) * psin


def _store_vt(dst, g, v):
    for cc in range(T_TOK // KV_TILE):
        dst[0, g, cc] = v[cc * KV_TILE:(cc + 1) * KV_TILE, :].T.astype(BF16)


def _prep_body(bq, bkv_s, bkv_w, cq, ckv, ciq, cik, c128, n128, p128, c64, n64, p64,
               qgn, kgn, qgd, kgd,
               qn_o, qr_o, ks_o, vst_o, kw_o, vwt_o, qd_o, kd_o, vdt_o, iq_o, ik_o):
    r128 = (c128[...], n128[...], p128[...], HEAD_DIM // 8)
    r64 = (c64[...], n64[...], p64[...], IDX_DIM // 8)
    for h in range(N_HEADS):
        c = slice(h * LANES, (h + 1) * LANES)
        y = _rms_head(bq[:, c], qgn[...])
        qn_o[:, c] = (y * Q_PRESCALE).astype(BF16)
        qr_o[:, c] = (_rope(y, *r128) * Q_PRESCALE).astype(BF16)
        qd_o[:, c] = (_rope(_rms_head(cq[:, c], qgd[...]), *r128) * Q_PRESCALE).astype(BF16)
    for g in range(N_KV):
        ck = slice(g * LANES, (g + 1) * LANES)
        cv = slice((N_KV + g) * LANES, (N_KV + g + 1) * LANES)
        ks_o[0, g] = _rope(_rms_head(bkv_s[:, ck], kgn[...]), *r128).astype(BF16)
        _store_vt(vst_o, g, bkv_s[:, cv])
        kw_o[0, g] = _rope(_rms_head(bkv_w[:, ck], kgn[...]), *r128).astype(BF16)
        _store_vt(vwt_o, g, bkv_w[:, cv])
        kd_o[0, g] = _rope(_rms_head(ckv[:, ck], kgd[...]), *r128).astype(BF16)
        _store_vt(vdt_o, g, ckv[:, cv])

    low = lax.broadcasted_iota(I32, (T_TOK, LANES), 1) < IDX_DIM
    ik = _rope(cik[...], *r64)
    hi = ik.astype(BF16).astype(F32)
    ik_o[0, :, 0:LANES] = (hi + pltpu.roll(ik - hi, IDX_DIM, axis=1)).astype(BF16)
    ik_o[0, :, LANES:2 * LANES] = hi.astype(BF16)
    for v in range(IDX_HEADS // 2):
        x = _rope(ciq[:, v * LANES:(v + 1) * LANES], *r64)
        for half in range(2):
            h = 2 * v + half
            xs = x if half == 0 else pltpu.roll(x, IDX_DIM, axis=1)
            xh = jnp.where(low, xs, 0.0)
            hi = xh.astype(BF16).astype(F32)
            iq_o[:, 2 * h * LANES:(2 * h + 1) * LANES] = (hi + pltpu.roll(hi, IDX_DIM, axis=1)).astype(BF16)
            iq_o[:, (2 * h + 1) * LANES:(2 * h + 2) * LANES] = (xh - hi).astype(BF16)


def _prep(z, tabs128, tabs64, qgn, kgn, qgd, kgd):
    nt = SEQ // T_TOK

    def zspec(col, width):
        return pl.BlockSpec((T_TOK, width), lambda b, i: (b * nt + i, col // width))

    tab = pl.BlockSpec((T_TOK, LANES), lambda b, i: (i, 0))
    gain = pl.BlockSpec((1, LANES), lambda b, i: (0, 0))
    qspec = pl.BlockSpec((T_TOK, GROUP_W), lambda b, i: (b * nt + i, 0))
    kspec = pl.BlockSpec((1, N_KV, T_TOK, LANES), lambda b, i: (b, 0, i, 0))
    vtspec = pl.BlockSpec((1, N_KV, T_TOK // KV_TILE, LANES, KV_TILE), lambda b, i: (b, 0, i, 0, 0))
    qshape = jax.ShapeDtypeStruct((BATCH * SEQ, GROUP_W), BF16)
    kshape = jax.ShapeDtypeStruct((BATCH, N_KV, SEQ, LANES), BF16)
    vtshape = jax.ShapeDtypeStruct((BATCH, N_KV, SEQ // KV_TILE, LANES, KV_TILE), BF16)
    return pl.pallas_call(
        _prep_body,
        grid=(BATCH, nt),
        in_specs=[zspec(COL_B_Q, GROUP_W), zspec(COL_B_KV + 512, 512), zspec(COL_B_KV + 1024, 512),
                  zspec(COL_C_Q, GROUP_W), zspec(COL_C_KV, 512), zspec(COL_C_IQ, 512), zspec(COL_C_IK, LANES)]
                 + [tab] * 6 + [gain] * 4,
        out_specs=[qspec, qspec, kspec, vtspec, kspec, vtspec, qspec, kspec, vtspec,
                   pl.BlockSpec((T_TOK, 2 * LANES * IDX_HEADS), lambda b, i: (b * nt + i, 0)),
                   pl.BlockSpec((1, T_TOK, 2 * LANES), lambda b, i: (b, i, 0))],
        out_shape=[qshape, qshape, kshape, vtshape, kshape, vtshape, qshape, kshape, vtshape,
                   jax.ShapeDtypeStruct((BATCH * SEQ, 2 * LANES * IDX_HEADS), BF16),
                   jax.ShapeDtypeStruct((BATCH, SEQ, 2 * LANES), BF16)],
        compiler_params=_params(2),
        name="attn_prep",
    )(z, z, z, z, z, z, z, *tabs128, *tabs64, qgn, kgn, qgd, kgd)


def _cmp_body(zc, pe, w1, w2, kg, o, ot, kbuf):
    kbuf[0:SEQ, :] = zc[...]
    kbuf[SEQ:SEQ + LANES, :] = jnp.zeros((LANES, LANES), F32)
    acc = jnp.zeros((N_CMP_PAD, LANES), F32)
    for l in range(CMP_LEN):
        rows = kbuf[pl.ds(l, N_CMP_PAD, stride=CMP_STRIDE), :] + pe[0, l:l + 1, :]
        acc = acc + jnp.dot(rows.astype(BF16), w1[0, l], preferred_element_type=F32)
    out = jnp.dot(_silu(acc).astype(BF16), w2[0], preferred_element_type=F32)
    res = jnp.where(pl.program_id(1) == 0, _rms_head(out, kg[...]), out)
    o[0, 0, 0] = res.astype(BF16)
    ot[0, 0, 0] = res.T.astype(BF16)


def _compress(z, pe, w1, w2, kg):
    base = COL_B_KV // LANES
    return pl.pallas_call(
        _cmp_body,
        grid=(BATCH, 2, N_KV),
        in_specs=[pl.BlockSpec((SEQ, LANES), lambda b, kv, g: (b, base + kv * N_KV + g)),
                  pl.BlockSpec((1, CMP_LEN, LANES), lambda b, kv, g: (kv, 0, 0)),
                  pl.BlockSpec((1, CMP_LEN, LANES, LANES), lambda b, kv, g: (kv, 0, 0, 0)),
                  pl.BlockSpec((1, LANES, LANES), lambda b, kv, g: (kv, 0, 0)),
                  pl.BlockSpec((1, LANES), lambda b, kv, g: (0, 0))],
        out_specs=[pl.BlockSpec((1, 1, 1, N_CMP_PAD, LANES), lambda b, kv, g: (b, kv, g, 0, 0)),
                   pl.BlockSpec((1, 1, 1, LANES, N_CMP_PAD), lambda b, kv, g: (b, kv, g, 0, 0))],
        out_shape=[jax.ShapeDtypeStruct((BATCH, 2, N_KV, N_CMP_PAD, LANES), BF16),
                   jax.ShapeDtypeStruct((BATCH, 2, N_KV, LANES, N_CMP_PAD), BF16)],
        scratch_shapes=[pltpu.VMEM((SEQ + LANES, LANES), F32)],
        compiler_params=_params(3),
        name="nsa_compress",
    )(z, pe, w1, w2, kg)


def _flash_heads(q_ref, slot0, groups, jlo, jhi, bias_fn, m_s, l_s, acc_s):
    n_heads = sum(len(cs) for _, _, cs in groups)
    lo, hi = slot0, slot0 + n_heads
    m_s[lo:hi, :] = jnp.full((n_heads, TQ), NEG, F32)
    l_s[lo:hi, :] = jnp.zeros((n_heads, TQ), F32)
    for slot in range(lo, hi):
        acc_s[slot] = jnp.zeros((HEAD_DIM, TQ), F32)

    def body(j, carry):
        r0 = pl.multiple_of(j * KV_TILE, KV_TILE)
        b = bias_fn(j)
        m_old = m_s[lo:hi, :]
        l_old = l_s[lo:hi, :]
        ss, vts = [], []
        for k_ref, vt_ref, cs in groups:
            k = k_ref[pl.ds(r0, KV_TILE), :]
            vt = vt_ref[j]
            for c in cs:
                ss.append(lax.dot_general(k, q_ref[:, c], NT_DIMS, preferred_element_type=F32) + b)
                vts.append(vt)
        m_new = jnp.maximum(m_old, jnp.concatenate([jnp.max(s, axis=0, keepdims=True) for s in ss], axis=0))
        m_safe = jnp.maximum(m_new, M_FLOOR)
        alpha = jnp.exp2(m_old - m_safe)
        ps = [jnp.exp2(s - m_safe[n:n + 1, :]) for n, s in enumerate(ss)]
        l_s[lo:hi, :] = alpha * l_old + jnp.concatenate([jnp.sum(p, axis=0, keepdims=True) for p in ps], axis=0)
        m_s[lo:hi, :] = m_new
        pv = [jnp.dot(vt, p.astype(BF16), preferred_element_type=F32) for vt, p in zip(vts, ps)]
        for n in range(n_heads):
            acc_s[lo + n] = alpha[n:n + 1, :] * acc_s[lo + n] + pv[n]
        return carry

    lax.fori_loop(jlo, jhi, body, 0)


def _flash_out(slot, l_s, acc_s):
    return acc_s[slot] * (1.0 / l_s[slot:slot + 1, :])


def _nsa_body(qn, qr, kc, vct, ks, vst, kw, vwt, gate, bg, ovt, yb, imp_s, bias_s, m_s, l_s, acc_s):
    i = pl.program_id(1)
    tq = i * TQ + lax.broadcasted_iota(I32, (1, TQ), 1)
    n_kv = i + 1
    sg = jax.nn.sigmoid(gate[...].T[0:3 * N_HEADS, :])
    cm = (lax.broadcasted_iota(I32, (N_CMP_PAD, TQ), 0) * CMP_STRIDE + (CMP_LEN - 1)) <= tq
    kk = lax.broadcasted_iota(I32, (N_SLC, TQ), 0)
    blk_t = lax.shift_right_logical(tq, 6)
    valid = kk <= blk_t
    force = (kk == 0) | (kk == blk_t) | (kk == blk_t - 1)
    rowkv = lax.broadcasted_iota(I32, (KV_TILE, TQ), 0)
    cmp_slot, slc_slot, win_slot = 0, HPG, 2 * HPG

    def win_bias(j):
        d = tq - (j * KV_TILE + rowkv)
        return jnp.where((d >= 0) & (d < WINDOW), 0.0, NEG)

    def slc_bias(j):
        return bias_s[pl.ds(pl.multiple_of(j * KV_TILE, KV_TILE), KV_TILE), :]

    for g in range(N_KV):
        kcg = kc[0, 0, g]
        vcg = vct[0, 0, g]
        psum = jnp.zeros((N_CMP_PAD, TQ), F32)
        for hh in range(HPG):
            h = HPG * g + hh
            s = lax.dot_general(kcg, qn[:, h * LANES:(h + 1) * LANES], NT_DIMS,
                                preferred_element_type=F32)
            s = jnp.where(cm, s, NEG)
            e = jnp.exp2(s - jnp.max(s, axis=0, keepdims=True))
            p = jnp.where(cm, e * (1.0 / jnp.sum(e, axis=0, keepdims=True)), 0.0)
            psum = psum + p
            acc_s[cmp_slot + hh] = jnp.dot(vcg, p.astype(BF16), preferred_element_type=F32)

        p_hi = psum.astype(BF16)
        p_lo = (psum - p_hi.astype(F32)).astype(BF16)
        imp = (jnp.dot(ovt[...], p_hi, preferred_element_type=F32)
               + jnp.dot(ovt[...], p_lo, preferred_element_type=F32))
        imp = jnp.where(force, FORCE, jnp.where(valid, imp, -FORCE))
        imp_s[...] = imp
        rank = jnp.zeros((N_SLC, TQ), F32)
        for j in range(N_SLC):
            vj = imp_s[j:j + 1, :]
            rank = rank + jnp.where((vj > imp) | ((vj == imp) & (kk > j)), 1.0, 0.0)
        imp_s[...] = jnp.where(rank < SLC_TOPN, 0.0, NEG)

        def fill(j, carry):
            r0 = pl.multiple_of(j * KV_TILE, KV_TILE)
            nb = KV_TILE // SLC_BLK
            tile = jnp.concatenate([jnp.broadcast_to(imp_s[pl.ds(nb * j + u, 1), :], (SLC_BLK, TQ))
                                    for u in range(nb)], axis=0)
            bias_s[pl.ds(r0, KV_TILE), :] = jnp.where(r0 + rowkv <= tq, tile, NEG)
            return carry
        lax.fori_loop(0, n_kv, fill, 0)

        cols = [slice((HPG * g + hh) * LANES, (HPG * g + hh + 1) * LANES) for hh in range(HPG)]
        _flash_heads(qr, slc_slot, [(ks.at[0, g], vst.at[0, g], cols)], 0, n_kv, slc_bias, m_s, l_s, acc_s)
        _flash_heads(qr, win_slot, [(kw.at[0, g], vwt.at[0, g], cols)],
                     jnp.maximum(n_kv - 1 - WINDOW // KV_TILE, 0), n_kv, win_bias, m_s, l_s, acc_s)
        for hh, c in enumerate(cols):
            h = HPG * g + hh
            tot = (sg[h:h + 1] * acc_s[cmp_slot + hh]
                   + sg[N_HEADS + h:N_HEADS + h + 1] * _flash_out(slc_slot + hh, l_s, acc_s)
                   + sg[2 * N_HEADS + h:2 * N_HEADS + h + 1] * _flash_out(win_slot + hh, l_s, acc_s))
            yb[:, c] = (tot.T * _silu(bg[:, c])).astype(BF16)


def _nsa(z, qn, qr, cmp_, cmpt, ks, vst, kw, vwt, ovt):
    nq = SEQ // TQ
    qspec = pl.BlockSpec((TQ, GROUP_W), lambda b, i: (b * nq + i, 0))
    kspec = pl.BlockSpec((1, N_KV, SEQ, LANES), lambda b, i: (b, 0, 0, 0))
    vtspec = pl.BlockSpec((1, N_KV, SEQ // KV_TILE, LANES, KV_TILE), lambda b, i: (b, 0, 0, 0, 0))
    return pl.pallas_call(
        _nsa_body,
        grid=(BATCH, nq),
        in_specs=[qspec, qspec,
                  pl.BlockSpec((1, 1, N_KV, N_CMP_PAD, LANES), lambda b, i: (b, 0, 0, 0, 0)),
                  pl.BlockSpec((1, 1, N_KV, LANES, N_CMP_PAD), lambda b, i: (b, 1, 0, 0, 0)),
                  kspec, vtspec, kspec, vtspec,
                  pl.BlockSpec((TQ, LANES), lambda b, i: (b * nq + i, COL_B_GATE // LANES)),
                  pl.BlockSpec((TQ, GROUP_W), lambda b, i: (b * nq + i, COL_B_G // GROUP_W)),
                  pl.BlockSpec((N_SLC, N_CMP_PAD), lambda b, i: (0, 0))],
        out_specs=qspec,
        out_shape=jax.ShapeDtypeStruct((BATCH * SEQ, GROUP_W), BF16),
        scratch_shapes=[pltpu.VMEM((N_SLC, TQ), F32), pltpu.VMEM((SEQ, TQ), F32),
                        pltpu.VMEM((3 * HPG, TQ), F32), pltpu.VMEM((3 * HPG, TQ), F32),
                        pltpu.VMEM((3 * HPG, HEAD_DIM, TQ), F32)],
        compiler_params=_params(2),
        name="nsa_attn",
    )(qn, qr, cmp_, cmpt, ks, vst, kw, vwt, z, z, ovt)


def _dsa_body(qd, kd, vdt, iq, ik, iw, cg, yc, sc_s, m_s, l_s, acc_s):
    i = pl.program_id(1)
    tq = i * TQ + lax.broadcasted_iota(I32, (1, TQ), 1)
    n_kv = i + 1
    iwt = iw[...].T[0:IDX_HEADS, :] * IW_SCALE
    rowkv = lax.broadcasted_iota(I32, (KV_TILE, TQ), 0)
    kf = float(DSA_TOPK)

    def fill(j, carry):
        mn, mx = carry
        r0 = pl.multiple_of(j * KV_TILE, KV_TILE)
        ikc = ik[0, pl.ds(r0, KV_TILE), :]
        sc = jnp.zeros((KV_TILE, TQ), F32)
        for h in range(IDX_HEADS):
            r = lax.dot_general(ikc, iq[:, 2 * h * LANES:(2 * h + 2) * LANES], NT_DIMS,
                                preferred_element_type=F32)
            sc = sc + jnp.maximum(r, 0.0) * iwt[h:h + 1, :]
        ok = r0 + rowkv <= tq
        sc_s[pl.ds(r0, KV_TILE), :] = jnp.where(ok, sc, -jnp.inf)
        mn = jnp.minimum(mn, jnp.min(jnp.where(ok, sc, jnp.inf), axis=0, keepdims=True))
        mx = jnp.maximum(mx, jnp.max(jnp.where(ok, sc, -jnp.inf), axis=0, keepdims=True))
        return mn, mx

    lo0, hi0 = lax.fori_loop(0, n_kv, fill,
                             (jnp.full((1, TQ), jnp.inf, F32), jnp.full((1, TQ), -jnp.inf, F32)))

    def sweep(fn, n_out):
        def body(c, accs):
            r0 = pl.multiple_of(c * KV_TILE, KV_TILE)
            inds = fn(sc_s[pl.ds(r0, KV_TILE), :], r0 + rowkv)
            return tuple(a + _fold8(jnp.where(ind, 1.0, 0.0)) for a, ind in zip(accs, inds))
        accs = lax.fori_loop(0, n_kv, body, tuple(jnp.zeros((8, TQ), F32) for _ in range(n_out)))
        return tuple(jnp.sum(a, axis=0, keepdims=True) for a in accs)

    nvalid = (tq + 1).astype(F32)
    act0 = jnp.where(nvalid > kf, 1.0, 0.0)

    def cond(c):
        return c[0] > 0.0

    def step(c):
        _, lo, hi, clo, act = c
        mid = lo + (hi - lo) * 0.5
        exh = (mid <= lo) | (mid >= hi)
        p = jnp.where(exh, hi, mid)
        (cnt,) = sweep(lambda x, srow: (x >= p,), 1)
        on = act > 0.0
        up = on & (cnt >= kf)
        dn = on & (cnt < kf)
        lo = jnp.where(up, p, lo)
        hi = jnp.where(dn, p, hi)
        clo = jnp.where(up, cnt, clo)
        act = jnp.where(on & (clo != kf) & jnp.logical_not(exh), 1.0, 0.0)
        return jnp.sum(act), lo, hi, clo, act

    _, thr, _, _, _ = lax.while_loop(cond, step, (jnp.sum(act0), lo0, hi0, nvalid, act0))

    cnt_g, cnt_e = sweep(lambda x, srow: (x > thr, x == thr), 2)
    need = kf - cnt_g
    excess = (cnt_g + cnt_e) > kf

    def tie_break():
        def it(_, c):
            jl, jh = c
            jm = lax.shift_right_arithmetic(jl + jh, 1)
            (cnt,) = sweep(lambda x, srow: ((x == thr) & (srow <= jm),), 1)
            ge = cnt >= need
            return jnp.where(ge, jl, jm), jnp.where(ge, jm, jh)
        _, jh = lax.fori_loop(0, 12, it, (jnp.full((1, TQ), -1, I32), jnp.full((1, TQ), SEQ - 1, I32)))
        return jnp.where(excess, jh, SEQ)

    last = lax.cond(jnp.sum(jnp.where(excess, 1.0, 0.0)) > 0.0, tie_break,
                    lambda: jnp.full((1, TQ), SEQ, I32))

    def to_bias(c, carry):
        r0 = pl.multiple_of(c * KV_TILE, KV_TILE)
        x = sc_s[pl.ds(r0, KV_TILE), :]
        sel = (x > thr) | ((x == thr) & (r0 + rowkv <= last))
        sc_s[pl.ds(r0, KV_TILE), :] = jnp.where(sel, 0.0, NEG)
        return carry
    lax.fori_loop(0, n_kv, to_bias, 0)

    def bias(j):
        return sc_s[pl.ds(pl.multiple_of(j * KV_TILE, KV_TILE), KV_TILE), :]

    groups = [(kd.at[0, g], vdt.at[0, g], [slice(h * LANES, (h + 1) * LANES)
                                           for h in range(HPG * g, HPG * (g + 1))]) for g in range(N_KV)]
    _flash_heads(qd, 0, groups, 0, n_kv, bias, m_s, l_s, acc_s)
    for h in range(N_HEADS):
        c = slice(h * LANES, (h + 1) * LANES)
        yc[:, c] = (_flash_out(h, l_s, acc_s).T * _silu(cg[:, c])).astype(BF16)


def _dsa(z, qd, kd, vdt, iq, ik):
    nq = SEQ // TQ
    qspec = pl.BlockSpec((TQ, GROUP_W), lambda b, i: (b * nq + i, 0))
    return pl.pallas_call(
        _dsa_body,
        grid=(BATCH, nq),
        in_specs=[qspec,
                  pl.BlockSpec((1, N_KV, SEQ, LANES), lambda b, i: (b, 0, 0, 0)),
                  pl.BlockSpec((1, N_KV, SEQ // KV_TILE, LANES, KV_TILE), lambda b, i: (b, 0, 0, 0, 0)),
                  pl.BlockSpec((TQ, 2 * LANES * IDX_HEADS), lambda b, i: (b * nq + i, 0)),
                  pl.BlockSpec((1, SEQ, 2 * LANES), lambda b, i: (b, 0, 0)),
                  pl.BlockSpec((TQ, LANES), lambda b, i: (b * nq + i, COL_C_IW // LANES)),
                  pl.BlockSpec((TQ, GROUP_W), lambda b, i: (b * nq + i, COL_C_G // GROUP_W))],
        out_specs=qspec,
        out_shape=jax.ShapeDtypeStruct((BATCH * SEQ, GROUP_W), BF16),
        scratch_shapes=[pltpu.VMEM((SEQ, TQ), F32), pltpu.VMEM((N_HEADS, TQ), F32),
                        pltpu.VMEM((N_HEADS, TQ), F32), pltpu.VMEM((N_HEADS, HEAD_DIM, TQ), F32)],
        compiler_params=_params(2),
        name="dsa_attn",
    )(qd, kd, vdt, iq, ik, z, z)


def _proj_weight(w):
    offs = []
    o = 0
    for s in SPLIT_SIZES:
        offs.append((o, s))
        o += s
    (a_x, a_g, b_q, b_g, b_kv, b_gate, c_q, c_g, c_kv, c_iq, c_ik, c_iw,
     d_in, d_b, d_c, d_g) = [w[:, o:o + s] for o, s in offs]

    def pad(s):
        return jnp.pad(s, ((0, 0), (0, LANES - s.shape[1])))

    cols = [a_x, a_g, b_q, b_g, c_q, c_g, d_in, d_b, d_c, d_g, b_kv, c_kv, c_iq,
            pad(b_gate), pad(c_ik), pad(c_iw), jnp.zeros((w.shape[0], PROJ_W - COL_C_IW - LANES), w.dtype)]
    return jnp.concatenate(cols, axis=1).astype(BF16)


def _rope_tables(head_dim):
    r = head_dim // 4
    half = r // 2
    pos = jnp.arange(SEQ, dtype=jnp.int32)
    inv = ROPE_THETA ** (-jnp.arange(half, dtype=F32) * 2.0 / r)
    ang = pos.astype(F32)[:, None] * inv[None, :]
    cos, sin = jnp.cos(ang), jnp.sin(ang)
    one = jnp.ones((SEQ, head_dim - r), F32)
    zero = jnp.zeros((SEQ, head_dim - r), F32)
    zh = jnp.zeros((SEQ, half), F32)
    reps = LANES // head_dim
    return tuple(jnp.tile(t, (1, reps)) for t in (
        jnp.concatenate([cos, cos, one], axis=1),
        jnp.concatenate([-sin, zh, zero], axis=1),
        jnp.concatenate([zh, sin, zero], axis=1)))


def _overlap_t():
    cs = jnp.arange(N_CMP_PAD) * CMP_STRIDE
    ss = jnp.arange(N_SLC) * SLC_BLK
    ov = jnp.clip(jnp.minimum(cs[None, :] + CMP_LEN, ss[:, None] + SLC_BLK)
                  - jnp.maximum(cs[None, :], ss[:, None]), 0, None).astype(F32) / CMP_LEN
    return jnp.where(jnp.arange(N_CMP_PAD)[None, :] < N_CMP_PAD - 1, ov, 0.0).astype(BF16)


def kernel(x, norm_g, w_in, w_out, lru_conv_w, lru_conv_b, lru_wa, lru_ba, lru_wx, lru_bx, lru_lambda,
           nsa_q_gain, nsa_k_gain, cmp_pe_k, cmp_w1_k, cmp_w2_k, cmp_pe_v, cmp_w1_v, cmp_w2_v,
           dsa_q_gain, dsa_k_gain, sc_conv_w):
    b, s, d = x.shape
    assert (b, s, d) == (BATCH, SEQ, D_MODEL)
    tabs128 = _rope_tables(HEAD_DIM)
    tabs64 = _rope_tables(IDX_DIM)
    ovt = _overlap_t()
    x2 = x.reshape(b * s, d)

    def row(v):
        return v.reshape(1, -1)

    for l in range(DEPTH):
        z = _inproj(x2, row(norm_g[l]), _proj_weight(w_in[l]))
        ya, yd = _mix_ad(z, lru_conv_w[l], row(lru_conv_b[l]), lru_wa[l].astype(BF16), row(lru_ba[l]),
                         lru_wx[l].astype(BF16), row(lru_bx[l]), row(lru_lambda[l]), sc_conv_w[l])
        qn, qr, ks, vst, kw, vwt, qd, kd, vdt, iq, ik = _prep(
            z, tabs128, tabs64, row(nsa_q_gain[l]), row(nsa_k_gain[l]), row(dsa_q_gain[l]), row(dsa_k_gain[l]))
        cmp_, cmpt = _compress(z, jnp.stack([cmp_pe_k[l], cmp_pe_v[l]]),
                               jnp.stack([cmp_w1_k[l], cmp_w1_v[l]]).astype(BF16),
                               jnp.stack([cmp_w2_k[l], cmp_w2_v[l]]).astype(BF16), row(nsa_k_gain[l]))
        yb = _nsa(z, qn, qr, cmp_, cmpt, ks, vst, kw, vwt, ovt)
        yc = _dsa(z, qd, kd, vdt, iq, ik)
        x2 = _outproj((ya, yb, yc, yd), w_out[l].astype(BF16), x2)
    return x2.reshape(b, s, d)
```

```python
import functools

import jax
import jax.numpy as jnp
from jax import lax
from jax.experimental import pallas as pl
from jax.experimental.pallas import tpu as pltpu

F32 = jnp.float32
BF16 = jnp.bfloat16
I32 = jnp.int32

D_MODEL = 4096
BATCH = 2
SEQ = 4096
DEPTH = 2
GROUP_W = D_MODEL // 4
HEAD_DIM = 128
N_HEADS = GROUP_W // HEAD_DIM
N_KV = N_HEADS // 4
HPG = N_HEADS // N_KV
ROPE_THETA = 500000.0
NORM_EPS = 1e-6
NEG = -1e30
FORCE = 1e6
LRU_BLOCKS = N_HEADS
LRU_C = 8.0
CONV_A = 4
CONV_D = 3
CMP_LEN = 32
CMP_STRIDE = 16
SLC_BLK = 64
SLC_TOPN = 16
WINDOW = 512
IDX_HEADS = 8
IDX_DIM = 64
DSA_TOPK = 256
SCALE = HEAD_DIM ** -0.5
Q_PRESCALE = SCALE * 1.4426950408889634
IW_SCALE = IDX_HEADS ** -0.5 * IDX_DIM ** -0.5

SPLIT_SIZES = (
    GROUP_W, GROUP_W,
    N_HEADS * HEAD_DIM, GROUP_W, 6 * N_KV * HEAD_DIM, 3 * N_HEADS,
    N_HEADS * HEAD_DIM, GROUP_W, 2 * N_KV * HEAD_DIM,
    IDX_HEADS * IDX_DIM, IDX_DIM, IDX_HEADS,
    GROUP_W, GROUP_W, GROUP_W, GROUP_W,
)

LANES = 128
COL_A_X, COL_A_G, COL_B_Q, COL_B_G, COL_C_Q, COL_C_G = 0, 1024, 2048, 3072, 4096, 5120
COL_D_IN, COL_D_B, COL_D_C, COL_D_G = 6144, 7168, 8192, 9216
COL_B_KV, COL_C_KV, COL_C_IQ = 10240, 11776, 12288
COL_B_GATE, COL_C_IK, COL_C_IW = 12800, 12928, 13056
PROJ_W = 13312

N_CMP_PAD = SEQ // CMP_STRIDE
N_SLC = SEQ // SLC_BLK
KV_TILE = 256
M_FLOOR = -1e20
SNAP_FIRST, SNAP_EVERY = 8, 4

VMEM_LIMIT_BYTES = 56 * 1024 * 1024

TM_IN, TN_IN = 512, 1024
TM_OUT, TN_OUT = 1024, 512
T_TOK = 256
TR_W = 128
TQ = 256

NT_DIMS = (((1,), (1,)), ((), ()))


def _params(n_axes):
    return pltpu.CompilerParams(dimension_semantics=("arbitrary",) * n_axes,
                                vmem_limit_bytes=VMEM_LIMIT_BYTES)


def _silu(x):
    return x * jax.nn.sigmoid(x)


def _fold8(x, op=jnp.add):
    parts = [x[r:r + 8] for r in range(0, x.shape[0], 8)]
    while len(parts) > 1:
        nxt = [op(parts[a], parts[a + 1]) for a in range(0, len(parts) - 1, 2)]
        if len(parts) % 2:
            nxt.append(parts[-1])
        parts = nxt
    return parts[0]


def _inproj_body(x_ref, g_ref, w_ref, o_ref, h_ref):
    @pl.when(pl.program_id(1) == 0)
    def _():
        def rows(r, carry):
            r0 = pl.multiple_of(r * 64, 64)
            x = x_ref[pl.ds(r0, 64), :]
            ms = jnp.mean(x * x, axis=-1, keepdims=True)
            h_ref[pl.ds(r0, 64), :] = ((x * lax.rsqrt(ms + NORM_EPS)) * g_ref[...]).astype(BF16)
            return carry
        lax.fori_loop(0, TM_IN // 64, rows, 0)

    o_ref[...] = lax.dot_general(h_ref[...], w_ref[...], NT_DIMS, preferred_element_type=F32)


def _inproj(x2, g, wt):
    m = x2.shape[0]
    return pl.pallas_call(
        _inproj_body,
        grid=(m // TM_IN, PROJ_W // TN_IN),
        in_specs=[pl.BlockSpec((TM_IN, D_MODEL), lambda i, j: (i, 0)),
                  pl.BlockSpec((1, D_MODEL), lambda i, j: (0, 0)),
                  pl.BlockSpec((TN_IN, D_MODEL), lambda i, j: (j, 0))],
        out_specs=pl.BlockSpec((TM_IN, TN_IN), lambda i, j: (i, j)),
        out_shape=jax.ShapeDtypeStruct((m, PROJ_W), F32),
        scratch_shapes=[pltpu.VMEM((TM_IN, D_MODEL), BF16)],
        compiler_params=_params(2),
        name="inproj",
    )(x2, g, wt)


def _outproj_body(ya, yb, yc, yd, wa, wb, wc, wd, x_ref, o_ref):
    acc = jnp.dot(ya[...], wa[...], preferred_element_type=F32)
    acc = acc + jnp.dot(yb[...], wb[...], preferred_element_type=F32)
    acc = acc + jnp.dot(yc[...], wc[...], preferred_element_type=F32)
    acc = acc + jnp.dot(yd[...], wd[...], preferred_element_type=F32)
    o_ref[...] = x_ref[...] + acc


def _outproj(ys, w, x2):
    m = x2.shape[0]
    yspec = pl.BlockSpec((TM_OUT, GROUP_W), lambda i, j: (i, 0))
    wspecs = [pl.BlockSpec((GROUP_W, TN_OUT), functools.partial(lambda i, j, k: (k, j), k=k)) for k in range(4)]
    xspec = pl.BlockSpec((TM_OUT, TN_OUT), lambda i, j: (i, j))
    return pl.pallas_call(
        _outproj_body,
        grid=(m // TM_OUT, D_MODEL // TN_OUT),
        in_specs=[yspec] * 4 + wspecs + [xspec],
        out_specs=xspec,
        out_shape=jax.ShapeDtypeStruct((m, D_MODEL), F32),
        compiler_params=_params(2),
        name="outproj",
    )(*ys, w, w, w, w, x2)


def _mix_ad_body(ax, ag, din, db, dc, dg, cw, cb, wa, ba, wx, bx, lam, scw,
                 ya, yd, xbuf, vbuf, a_s, b_s, hst):
    t = T_TOK

    @pl.when(pl.program_id(1) == 0)
    def _():
        xbuf[0:8, :] = jnp.zeros((8, GROUP_W), F32)
        vbuf[0:8, :] = jnp.zeros((8, GROUP_W), F32)
        hst[...] = jnp.zeros((8, GROUP_W), F32)

    xbuf[8:8 + t, :] = ax[...]
    for blk in range(LRU_BLOCKS):
        c = slice(blk * LANES, (blk + 1) * LANES)
        u = (cw[0:1, c] * xbuf[5:5 + t, c] + cw[1:2, c] * xbuf[6:6 + t, c]
             + cw[2:3, c] * xbuf[7:7 + t, c] + cw[3:4, c] * xbuf[8:8 + t, c]) + cb[:, c]
        ub = u.astype(BF16)
        r = jax.nn.sigmoid(jnp.dot(ub, wa[blk], preferred_element_type=F32) + ba[:, c])
        ig = jax.nn.sigmoid(jnp.dot(ub, wx[blk], preferred_element_type=F32) + bx[:, c])
        nl = -lam[:, c]
        sp = jnp.maximum(nl, 0.0) + jnp.log1p(jnp.exp(-jnp.abs(nl)))
        log_a = (-LRU_C * r) * sp
        a_s[:, c] = jnp.exp(log_a)
        y2 = 2.0 * log_a
        em1 = jnp.tanh(0.5 * y2) * (jnp.exp(y2) + 1.0)
        b_s[:, c] = jnp.sqrt(-em1) * (ig * u)
    xbuf[0:8, :] = xbuf[t:t + 8, :]

    row = lax.broadcasted_iota(I32, (8, GROUP_W), 0)

    def group(gi, hprev):
        r0 = pl.multiple_of(gi * 8, 8)
        a = a_s[pl.ds(r0, 8), :]
        b = b_s[pl.ds(r0, 8), :]
        for d in (1, 2, 4):
            ok = row >= d
            b = jnp.where(ok, a * pltpu.roll(b, d, axis=0) + b, b)
            a = jnp.where(ok, a * pltpu.roll(a, d, axis=0), a)
        h = a * hprev + b
        b_s[pl.ds(r0, 8), :] = h
        return jnp.broadcast_to(h[7:8, :], (8, GROUP_W))

    hst[...] = lax.fori_loop(0, t // 8, group, hst[...])

    vbuf[8:8 + t, :] = dc[...] * din[...]
    for blk in range(LRU_BLOCKS):
        c = slice(blk * LANES, (blk + 1) * LANES)
        ya[:, c] = (b_s[:, c] * _silu(ag[:, c])).astype(BF16)
        conv = (scw[0:1, c] * vbuf[6:6 + t, c] + scw[1:2, c] * vbuf[7:7 + t, c]
                + scw[2:3, c] * vbuf[8:8 + t, c])
        yd[:, c] = ((db[:, c] * conv) * _silu(dg[:, c])).astype(BF16)
    vbuf[0:8, :] = vbuf[t:t + 8, :]


def _mix_ad(z, cw, cb, wa, ba, wx, bx, lam, scw):
    nt = SEQ // T_TOK

    def zspec(col):
        return pl.BlockSpec((T_TOK, GROUP_W), lambda b, i: (b * nt + i, col // GROUP_W))

    def full(a):
        return pl.BlockSpec(a.shape, lambda b, i: (0,) * a.ndim)

    small = (cw, cb, wa, ba, wx, bx, lam, scw)
    yspec = pl.BlockSpec((T_TOK, GROUP_W), lambda b, i: (b * nt + i, 0))
    return pl.pallas_call(
        _mix_ad_body,
        grid=(BATCH, nt),
        in_specs=[zspec(COL_A_X), zspec(COL_A_G), zspec(COL_D_IN), zspec(COL_D_B), zspec(COL_D_C),
                  zspec(COL_D_G)] + [full(a) for a in small],
        out_specs=[yspec, yspec],
        out_shape=[jax.ShapeDtypeStruct((BATCH * SEQ, GROUP_W), BF16)] * 2,
        scratch_shapes=[pltpu.VMEM((T_TOK + 8, GROUP_W), F32), pltpu.VMEM((T_TOK + 8, GROUP_W), F32),
                        pltpu.VMEM((T_TOK, GROUP_W), F32), pltpu.VMEM((T_TOK, GROUP_W), F32),
                        pltpu.VMEM((8, GROUP_W), F32)],
        compiler_params=_params(2),
        name="mix_ad",
    )(z, z, z, z, z, z, *small)


def _rms_head(x, gain):
    ms = jnp.mean(x * x, axis=-1, keepdims=True)
    return (x * lax.rsqrt(ms + NORM_EPS)) * gain


def _rope(y, cos, nsin, psin, half):
    return y * cos + pltpu.roll(y, LANES - half, axis=1) * nsin + pltpu.roll(y, half, axis=1) * psin


def _store_vt(dst, g, v):
    for cc in range(T_TOK // KV_TILE):
        dst[0, g, cc] = v[cc * KV_TILE:(cc + 1) * KV_TILE, :].T.astype(BF16)


def _prep_body(bq, bkv_s, bkv_w, cq, ckv, ciq, cik, c128, n128, p128, c64, n64, p64,
               qgn, kgn, qgd, kgd,
               qn_o, qr_o, ks_o, vst_o, kw_o, vwt_o, qd_o, kd_o, vdt_o, iq_o, ik_o):
    r128 = (c128[...], n128[...], p128[...], HEAD_DIM // 8)
    r64 = (c64[...], n64[...], p64[...], IDX_DIM // 8)
    for h in range(N_HEADS):
        c = slice(h * LANES, (h + 1) * LANES)
        y = _rms_head(bq[:, c], qgn[...])
        qn_o[:, c] = (y * Q_PRESCALE).astype(BF16)
        qr_o[:, c] = (_rope(y, *r128) * Q_PRESCALE).astype(BF16)
        qd_o[:, c] = (_rope(_rms_head(cq[:, c], qgd[...]), *r128) * Q_PRESCALE).astype(BF16)
    for g in range(N_KV):
        ck = slice(g * LANES, (g + 1) * LANES)
        cv = slice((N_KV + g) * LANES, (N_KV + g + 1) * LANES)
        ks_o[0, g] = _rope(_rms_head(bkv_s[:, ck], kgn[...]), *r128).astype(BF16)
        _store_vt(vst_o, g, bkv_s[:, cv])
        kw_o[0, g] = _rope(_rms_head(bkv_w[:, ck], kgn[...]), *r128).astype(BF16)
        _store_vt(vwt_o, g, bkv_w[:, cv])
        kd_o[0, g] = _rope(_rms_head(ckv[:, ck], kgd[...]), *r128).astype(BF16)
        _store_vt(vdt_o, g, ckv[:, cv])

    low = lax.broadcasted_iota(I32, (T_TOK, LANES), 1) < IDX_DIM
    ik = _rope(cik[...], *r64)
    hi = ik.astype(BF16).astype(F32)
    ik_o[0, :, 0:LANES] = (hi + pltpu.roll(ik - hi, IDX_DIM, axis=1)).astype(BF16)
    ik_o[0, :, LANES:2 * LANES] = hi.astype(BF16)
    for v in range(IDX_HEADS // 2):
        x = _rope(ciq[:, v * LANES:(v + 1) * LANES], *r64)
        for half in range(2):
            h = 2 * v + half
            xs = x if half == 0 else pltpu.roll(x, IDX_DIM, axis=1)
            xh = jnp.where(low, xs, 0.0)
            hi = xh.astype(BF16).astype(F32)
            iq_o[:, 2 * h * LANES:(2 * h + 1) * LANES] = (hi + pltpu.roll(hi, IDX_DIM, axis=1)).astype(BF16)
            iq_o[:, (2 * h + 1) * LANES:(2 * h + 2) * LANES] = (xh - hi).astype(BF16)


def _prep(z, tabs128, tabs64, qgn, kgn, qgd, kgd):
    nt = SEQ // T_TOK

    def zspec(col, width):
        return pl.BlockSpec((T_TOK, width), lambda b, i: (b * nt + i, col // width))

    tab = pl.BlockSpec((T_TOK, LANES), lambda b, i: (i, 0))
    gain = pl.BlockSpec((1, LANES), lambda b, i: (0, 0))
    qspec = pl.BlockSpec((T_TOK, GROUP_W), lambda b, i: (b * nt + i, 0))
    kspec = pl.BlockSpec((1, N_KV, T_TOK, LANES), lambda b, i: (b, 0, i, 0))
    vtspec = pl.BlockSpec((1, N_KV, T_TOK // KV_TILE, LANES, KV_TILE), lambda b, i: (b, 0, i, 0, 0))
    qshape = jax.ShapeDtypeStruct((BATCH * SEQ, GROUP_W), BF16)
    kshape = jax.ShapeDtypeStruct((BATCH, N_KV, SEQ, LANES), BF16)
    vtshape = jax.ShapeDtypeStruct((BATCH, N_KV, SEQ // KV_TILE, LANES, KV_TILE), BF16)
    return pl.pallas_call(
        _prep_body,
        grid=(BATCH, nt),
        in_specs=[zspec(COL_B_Q, GROUP_W), zspec(COL_B_KV + 512, 512), zspec(COL_B_KV + 1024, 512),
                  zspec(COL_C_Q, GROUP_W), zspec(COL_C_KV, 512), zspec(COL_C_IQ, 512), zspec(COL_C_IK, LANES)]
                 + [tab] * 6 + [gain] * 4,
        out_specs=[qspec, qspec, kspec, vtspec, kspec, vtspec, qspec, kspec, vtspec,
                   pl.BlockSpec((T_TOK, 2 * LANES * IDX_HEADS), lambda b, i: (b * nt + i, 0)),
                   pl.BlockSpec((1, T_TOK, 2 * LANES), lambda b, i: (b, i, 0))],
        out_shape=[qshape, qshape, kshape, vtshape, kshape, vtshape, qshape, kshape, vtshape,
                   jax.ShapeDtypeStruct((BATCH * SEQ, 2 * LANES * IDX_HEADS), BF16),
                   jax.ShapeDtypeStruct((BATCH, SEQ, 2 * LANES), BF16)],
        compiler_params=_params(2),
        name="attn_prep",
    )(z, z, z, z, z, z, z, *tabs128, *tabs64, qgn, kgn, qgd, kgd)


def _cmp_body(zc, pe, w1, w2, kg, o, ot, kbuf):
    kbuf[0:SEQ, :] = zc[...]
    kbuf[SEQ:SEQ + LANES, :] = jnp.zeros((LANES, LANES), F32)
    acc = jnp.zeros((N_CMP_PAD, LANES), F32)
    for l in range(CMP_LEN):
        rows = kbuf[pl.ds(l, N_CMP_PAD, stride=CMP_STRIDE), :] + pe[0, l:l + 1, :]
        acc = acc + jnp.dot(rows.astype(BF16), w1[0, l], preferred_element_type=F32)
    out = jnp.dot(_silu(acc).astype(BF16), w2[0], preferred_element_type=F32)
    res = jnp.where(pl.program_id(1) == 0, _rms_head(out, kg[...]), out)
    o[0, 0, 0] = res.astype(BF16)
    ot[0, 0, 0] = res.T.astype(BF16)


def _compress(z, pe, w1, w2, kg):
    base = COL_B_KV // LANES
    return pl.pallas_call(
        _cmp_body,
        grid=(BATCH, 2, N_KV),
        in_specs=[pl.BlockSpec((SEQ, LANES), lambda b, kv, g: (b, base + kv * N_KV + g)),
                  pl.BlockSpec((1, CMP_LEN, LANES), lambda b, kv, g: (kv, 0, 0)),
                  pl.BlockSpec((1, CMP_LEN, LANES, LANES), lambda b, kv, g: (kv, 0, 0, 0)),
                  pl.BlockSpec((1, LANES, LANES), lambda b, kv, g: (kv, 0, 0)),
                  pl.BlockSpec((1, LANES), lambda b, kv, g: (0, 0))],
        out_specs=[pl.BlockSpec((1, 1, 1, N_CMP_PAD, LANES), lambda b, kv, g: (b, kv, g, 0, 0)),
                   pl.BlockSpec((1, 1, 1, LANES, N_CMP_PAD), lambda b, kv, g: (b, kv, g, 0, 0))],
        out_shape=[jax.ShapeDtypeStruct((BATCH, 2, N_KV, N_CMP_PAD, LANES), BF16),
                   jax.ShapeDtypeStruct((BATCH, 2, N_KV, LANES, N_CMP_PAD), BF16)],
        scratch_shapes=[pltpu.VMEM((SEQ + LANES, LANES), F32)],
        compiler_params=_params(3),
        name="nsa_compress",
    )(z, pe, w1, w2, kg)


def _flash_heads(q_ref, slot0, groups, jlo, jhi, bias_fn, m_s, l_s, acc_s):
    n_heads = sum(len(cs) for _, _, cs in groups)
    lo, hi = slot0, slot0 + n_heads
    m_s[lo:hi, :] = jnp.full((n_heads, TQ), NEG, F32)
    l_s[lo:hi, :] = jnp.zeros((n_heads, TQ), F32)
    for slot in range(lo, hi):
        acc_s[slot] = jnp.zeros((HEAD_DIM, TQ), F32)

    def body(j, carry):
        r0 = pl.multiple_of(j * KV_TILE, KV_TILE)
        b = bias_fn(j)
        m_old = m_s[lo:hi, :]
        l_old = l_s[lo:hi, :]
        ss, vts = [], []
        for k_ref, vt_ref, cs in groups:
            k = k_ref[pl.ds(r0, KV_TILE), :]
            vt = vt_ref[j]
            for c in cs:
                ss.append(lax.dot_general(k, q_ref[:, c], NT_DIMS, preferred_element_type=F32) + b)
                vts.append(vt)
        m_new = jnp.maximum(m_old, jnp.concatenate([jnp.max(s, axis=0, keepdims=True) for s in ss], axis=0))
        m_safe = jnp.maximum(m_new, M_FLOOR)
        alpha = jnp.exp2(m_old - m_safe)
        ps = [jnp.exp2(s - m_safe[n:n + 1, :]) for n, s in enumerate(ss)]
        l_s[lo:hi, :] = alpha * l_old + jnp.concatenate([jnp.sum(p, axis=0, keepdims=True) for p in ps], axis=0)
        m_s[lo:hi, :] = m_new
        pv = [jnp.dot(vt, p.astype(BF16), preferred_element_type=F32) for vt, p in zip(vts, ps)]
        for n in range(n_heads):
            acc_s[lo + n] = alpha[n:n + 1, :] * acc_s[lo + n] + pv[n]
        return carry

    lax.fori_loop(jlo, jhi, body, 0)


def _flash_out(slot, l_s, acc_s):
    return acc_s[slot] * (1.0 / l_s[slot:slot + 1, :])


def _nsa_body(qn, qr, kc, vct, ks, vst, kw, vwt, gate, bg, ovt, yb, imp_s, bias_s, m_s, l_s, acc_s):
    i = pl.program_id(1)
    tq = i * TQ + lax.broadcasted_iota(I32, (1, TQ), 1)
    n_kv = i + 1
    sg = jax.nn.sigmoid(gate[...].T[0:3 * N_HEADS, :])
    cm = (lax.broadcasted_iota(I32, (N_CMP_PAD, TQ), 0) * CMP_STRIDE + (CMP_LEN - 1)) <= tq
    kk = lax.broadcasted_iota(I32, (N_SLC, TQ), 0)
    sub8 = lax.broadcasted_iota(I32, (8, TQ), 0)
    blk_t = lax.shift_right_logical(tq, 6)
    valid = kk <= blk_t
    force = (kk == 0) | (kk == blk_t) | (kk == blk_t - 1)
    rowkv = lax.broadcasted_iota(I32, (KV_TILE, TQ), 0)
    cmp_slot, slc_slot, win_slot = 0, HPG, 2 * HPG

    def win_bias(j):
        d = tq - (j * KV_TILE + rowkv)
        return jnp.where((d >= 0) & (d < WINDOW), 0.0, NEG)

    def slc_bias(j):
        return bias_s[pl.ds(pl.multiple_of(j * KV_TILE, KV_TILE), KV_TILE), :]

    for g in range(N_KV):
        kcg = kc[0, 0, g]
        vcg = vct[0, 0, g]
        cols = [slice((HPG * g + hh) * LANES, (HPG * g + hh + 1) * LANES) for hh in range(HPG)]
        ss = [jnp.where(cm, lax.dot_general(kcg, qn[:, c], NT_DIMS, preferred_element_type=F32), NEG)
              for c in cols]
        es = [jnp.exp2(s - jnp.max(s, axis=0, keepdims=True)) for s in ss]
        ps = [jnp.where(cm, e * (1.0 / jnp.sum(e, axis=0, keepdims=True)), 0.0) for e in es]
        for hh, p in enumerate(ps):
            acc_s[cmp_slot + hh] = jnp.dot(vcg, p.astype(BF16), preferred_element_type=F32)
        psum = (ps[0] + ps[1]) + (ps[2] + ps[3])

        p_hi = psum.astype(BF16)
        p_lo = (psum - p_hi.astype(F32)).astype(BF16)
        imp = (jnp.dot(ovt[...], p_hi, preferred_element_type=F32)
               + jnp.dot(ovt[...], p_lo, preferred_element_type=F32))
        imp = jnp.where(force, FORCE, jnp.where(valid, imp, -FORCE))
        imp_s[...] = imp
        bands = [imp[8 * r:8 * r + 8] for r in range(N_SLC // 8)]
        ranks = [jnp.zeros((8, TQ), F32) for _ in bands]
        for j in range(N_SLC):
            vj = imp_s[j:j + 1, :]
            for r, x in enumerate(bands):
                if r < j // 8:
                    hit = vj > x
                elif r > j // 8:
                    hit = vj >= x
                else:
                    hit = (vj > x) | ((vj == x) & (sub8 > j % 8))
                ranks[r] = ranks[r] + jnp.where(hit, 1.0, 0.0)
        imp_s[...] = jnp.where(jnp.concatenate(ranks, axis=0) < SLC_TOPN, 0.0, NEG)

        def fill(j, carry):
            r0 = pl.multiple_of(j * KV_TILE, KV_TILE)
            nb = KV_TILE // SLC_BLK
            tile = jnp.concatenate([jnp.broadcast_to(imp_s[pl.ds(nb * j + u, 1), :], (SLC_BLK, TQ))
                                    for u in range(nb)], axis=0)
            bias_s[pl.ds(r0, KV_TILE), :] = jnp.where(r0 + rowkv <= tq, tile, NEG)
            return carry
        lax.fori_loop(0, n_kv, fill, 0)

        _flash_heads(qr, slc_slot, [(ks.at[0, g], vst.at[0, g], cols)], 0, n_kv, slc_bias, m_s, l_s, acc_s)
        _flash_heads(qr, win_slot, [(kw.at[0, g], vwt.at[0, g], cols)],
                     jnp.maximum(n_kv - 1 - WINDOW // KV_TILE, 0), n_kv, win_bias, m_s, l_s, acc_s)
        for hh, c in enumerate(cols):
            h = HPG * g + hh
            tot = (sg[h:h + 1] * acc_s[cmp_slot + hh]
                   + sg[N_HEADS + h:N_HEADS + h + 1] * _flash_out(slc_slot + hh, l_s, acc_s)
                   + sg[2 * N_HEADS + h:2 * N_HEADS + h + 1] * _flash_out(win_slot + hh, l_s, acc_s))
            yb[:, c] = (tot.T * _silu(bg[:, c])).astype(BF16)


def _nsa(z, qn, qr, cmp_, cmpt, ks, vst, kw, vwt, ovt):
    nq = SEQ // TQ
    qspec = pl.BlockSpec((TQ, GROUP_W), lambda b, i: (b * nq + i, 0))
    kspec = pl.BlockSpec((1, N_KV, SEQ, LANES), lambda b, i: (b, 0, 0, 0))
    vtspec = pl.BlockSpec((1, N_KV, SEQ // KV_TILE, LANES, KV_TILE), lambda b, i: (b, 0, 0, 0, 0))
    return pl.pallas_call(
        _nsa_body,
        grid=(BATCH, nq),
        in_specs=[qspec, qspec,
                  pl.BlockSpec((1, 1, N_KV, N_CMP_PAD, LANES), lambda b, i: (b, 0, 0, 0, 0)),
                  pl.BlockSpec((1, 1, N_KV, LANES, N_CMP_PAD), lambda b, i: (b, 1, 0, 0, 0)),
                  kspec, vtspec, kspec, vtspec,
                  pl.BlockSpec((TQ, LANES), lambda b, i: (b * nq + i, COL_B_GATE // LANES)),
                  pl.BlockSpec((TQ, GROUP_W), lambda b, i: (b * nq + i, COL_B_G // GROUP_W)),
                  pl.BlockSpec((N_SLC, N_CMP_PAD), lambda b, i: (0, 0))],
        out_specs=qspec,
        out_shape=jax.ShapeDtypeStruct((BATCH * SEQ, GROUP_W), BF16),
        scratch_shapes=[pltpu.VMEM((N_SLC, TQ), F32), pltpu.VMEM((SEQ, TQ), F32),
                        pltpu.VMEM((3 * HPG, TQ), F32), pltpu.VMEM((3 * HPG, TQ), F32),
                        pltpu.VMEM((3 * HPG, HEAD_DIM, TQ), F32)],
        compiler_params=_params(2),
        name="nsa_attn",
    )(qn, qr, cmp_, cmpt, ks, vst, kw, vwt, z, z, ovt)


def _dsa_body(qd, kd, vdt, iq, ik, iw, cg, yc, sc_s, m_s, l_s, acc_s):
    i = pl.program_id(1)
    tq = i * TQ + lax.broadcasted_iota(I32, (1, TQ), 1)
    n_kv = i + 1
    iwt = iw[...].T[0:IDX_HEADS, :] * IW_SCALE
    rowkv = lax.broadcasted_iota(I32, (KV_TILE, TQ), 0)
    kf = float(DSA_TOPK)

    def fill(j, carry):
        mn, mx = carry
        r0 = pl.multiple_of(j * KV_TILE, KV_TILE)
        ikc = ik[0, pl.ds(r0, KV_TILE), :]
        sc = jnp.zeros((KV_TILE, TQ), F32)
        for h in range(IDX_HEADS):
            r = lax.dot_general(ikc, iq[:, 2 * h * LANES:(2 * h + 2) * LANES], NT_DIMS,
                                preferred_element_type=F32)
            sc = sc + jnp.maximum(r, 0.0) * iwt[h:h + 1, :]
        ok = r0 + rowkv <= tq
        sc_s[pl.ds(r0, KV_TILE), :] = jnp.where(ok, sc, -jnp.inf)
        mn = jnp.minimum(mn, jnp.min(jnp.where(ok, sc, jnp.inf), axis=0, keepdims=True))
        mx = jnp.maximum(mx, jnp.max(jnp.where(ok, sc, -jnp.inf), axis=0, keepdims=True))
        return mn, mx

    lo0, hi0 = lax.fori_loop(0, n_kv, fill,
                             (jnp.full((1, TQ), jnp.inf, F32), jnp.full((1, TQ), -jnp.inf, F32)))

    def sweep(fn, n_out):
        def body(c, accs):
            r0 = pl.multiple_of(c * KV_TILE, KV_TILE)
            inds = fn(sc_s[pl.ds(r0, KV_TILE), :], r0 + rowkv)
            return tuple(a + _fold8(jnp.where(ind, 1.0, 0.0)) for a, ind in zip(accs, inds))
        accs = lax.fori_loop(0, n_kv, body, tuple(jnp.zeros((8, TQ), F32) for _ in range(n_out)))
        return tuple(jnp.sum(a, axis=0, keepdims=True) for a in accs)

    def data_span(lo, hi):
        def body(c, carry):
            mn, mx = carry
            x = sc_s[pl.ds(pl.multiple_of(c * KV_TILE, KV_TILE), KV_TILE), :]
            mn = jnp.minimum(mn, _fold8(jnp.where(x >= lo, x, jnp.inf), jnp.minimum))
            mx = jnp.maximum(mx, _fold8(jnp.where(x <= hi, x, -jnp.inf), jnp.maximum))
            return mn, mx
        mn, mx = lax.fori_loop(0, n_kv, body, (jnp.full((8, TQ), jnp.inf, F32), jnp.full((8, TQ), -jnp.inf, F32)))
        return jnp.min(mn, axis=0, keepdims=True), jnp.max(mx, axis=0, keepdims=True)

    nvalid = (tq + 1).astype(F32)
    act0 = jnp.where(nvalid > kf, 1.0, 0.0)

    def cond(c):
        return c[0] > 0.0

    def bisect(lo, hi, clo, on):
        mid = lo + (hi - lo) * 0.5
        exh = (mid <= lo) | (mid >= hi)
        p = jnp.where(exh, hi, mid)
        (cnt,) = sweep(lambda x, srow: (x >= p,), 1)
        up = on & (cnt >= kf)
        dn = on & (cnt < kf)
        clo = jnp.where(up, cnt, clo)
        act = jnp.where(on & (clo != kf) & jnp.logical_not(exh), 1.0, 0.0)
        return jnp.where(up, p, lo), jnp.where(dn, p, hi), clo, act

    def snap(lo, hi, clo, on):
        dmin, dmax = data_span(lo, hi)
        act = jnp.where(on & (dmin < dmax), 1.0, 0.0)
        return jnp.where(on, dmin, lo), jnp.where(on, dmax, hi), clo, act

    def step(c):
        _, it, lo, hi, clo, act = c
        is_snap = (it >= SNAP_FIRST) & (((it - SNAP_FIRST) & (SNAP_EVERY - 1)) == 0)
        lo, hi, clo, act = lax.cond(is_snap, snap, bisect, lo, hi, clo, act > 0.0)
        return jnp.sum(act), it + 1, lo, hi, clo, act

    _, _, thr, _, _, _ = lax.while_loop(cond, step, (jnp.sum(act0), jnp.int32(0), lo0, hi0, nvalid, act0))

    cnt_g, cnt_e = sweep(lambda x, srow: (x > thr, x == thr), 2)
    need = kf - cnt_g
    excess = (cnt_g + cnt_e) > kf

    def tie_break():
        def it(_, c):
            jl, jh = c
            jm = lax.shift_right_arithmetic(jl + jh, 1)
            (cnt,) = sweep(lambda x, srow: ((x == thr) & (srow <= jm),), 1)
            ge = cnt >= need
            return jnp.where(ge, jl, jm), jnp.where(ge, jm, jh)
        _, jh = lax.fori_loop(0, 12, it, (jnp.full((1, TQ), -1, I32), jnp.full((1, TQ), SEQ - 1, I32)))
        return jnp.where(excess, jh, SEQ)

    last = lax.cond(jnp.sum(jnp.where(excess, 1.0, 0.0)) > 0.0, tie_break,
                    lambda: jnp.full((1, TQ), SEQ, I32))

    def to_bias(c, carry):
        r0 = pl.multiple_of(c * KV_TILE, KV_TILE)
        x = sc_s[pl.ds(r0, KV_TILE), :]
        sel = (x > thr) | ((x == thr) & (r0 + rowkv <= last))
        sc_s[pl.ds(r0, KV_TILE), :] = jnp.where(sel, 0.0, NEG)
        return carry
    lax.fori_loop(0, n_kv, to_bias, 0)

    def bias(j):
        return sc_s[pl.ds(pl.multiple_of(j * KV_TILE, KV_TILE), KV_TILE), :]

    groups = [(kd.at[0, g], vdt.at[0, g], [slice(h * LANES, (h + 1) * LANES)
                                           for h in range(HPG * g, HPG * (g + 1))]) for g in range(N_KV)]
    _flash_heads(qd, 0, groups, 0, n_kv, bias, m_s, l_s, acc_s)
    for h in range(N_HEADS):
        c = slice(h * LANES, (h + 1) * LANES)
        yc[:, c] = (_flash_out(h, l_s, acc_s).T * _silu(cg[:, c])).astype(BF16)


def _dsa(z, qd, kd, vdt, iq, ik):
    nq = SEQ // TQ
    qspec = pl.BlockSpec((TQ, GROUP_W), lambda b, i: (b * nq + i, 0))
    return pl.pallas_call(
        _dsa_body,
        grid=(BATCH, nq),
        in_specs=[qspec,
                  pl.BlockSpec((1, N_KV, SEQ, LANES), lambda b, i: (b, 0, 0, 0)),
                  pl.BlockSpec((1, N_KV, SEQ // KV_TILE, LANES, KV_TILE), lambda b, i: (b, 0, 0, 0, 0)),
                  pl.BlockSpec((TQ, 2 * LANES * IDX_HEADS), lambda b, i: (b * nq + i, 0)),
                  pl.BlockSpec((1, SEQ, 2 * LANES), lambda b, i: (b, 0, 0)),
                  pl.BlockSpec((TQ, LANES), lambda b, i: (b * nq + i, COL_C_IW // LANES)),
                  pl.BlockSpec((TQ, GROUP_W), lambda b, i: (b * nq + i, COL_C_G // GROUP_W))],
        out_specs=qspec,
        out_shape=jax.ShapeDtypeStruct((BATCH * SEQ, GROUP_W), BF16),
        scratch_shapes=[pltpu.VMEM((SEQ, TQ), F32), pltpu.VMEM((N_HEADS, TQ), F32),
                        pltpu.VMEM((N_HEADS, TQ), F32), pltpu.VMEM((N_HEADS, HEAD_DIM, TQ), F32)],
        compiler_params=_params(2),
        name="dsa_attn",
    )(qd, kd, vdt, iq, ik, z, z)


def _segment_moves():
    src = {}
    o = 0
    names = ("a_x", "a_g", "b_q", "b_g", "b_kv", "b_gate", "c_q", "c_g", "c_kv", "c_iq", "c_ik", "c_iw",
             "d_in", "d_b", "d_c", "d_g")
    for n, s in zip(names, SPLIT_SIZES):
        src[n] = (o, s)
        o += s
    dst = dict(a_x=COL_A_X, a_g=COL_A_G, b_q=COL_B_Q, b_g=COL_B_G, c_q=COL_C_Q, c_g=COL_C_G,
               d_in=COL_D_IN, d_b=COL_D_B, d_c=COL_D_C, d_g=COL_D_G, b_kv=COL_B_KV, c_kv=COL_C_KV,
               c_iq=COL_C_IQ, b_gate=COL_B_GATE, c_ik=COL_C_IK, c_iw=COL_C_IW)
    return [(src[n][0], dst[n], src[n][1]) for n in names]


def _wprep_body(w_ref, o_ref):
    o_ref[COL_B_GATE:PROJ_W, :] = jnp.zeros((PROJ_W - COL_B_GATE, TR_W), BF16)
    for s, d, width in _segment_moves():
        for c in range(0, width, GROUP_W):
            n = min(GROUP_W, width - c)
            o_ref[d + c:d + c + n, :] = w_ref[s + c:s + c + n, :].astype(BF16)


def _proj_weight(wt, layer):
    _, n, k = wt.shape
    return pl.pallas_call(
        _wprep_body,
        grid=(k // TR_W,),
        in_specs=[pl.BlockSpec((None, n, TR_W), lambda i: (layer, 0, i))],
        out_specs=pl.BlockSpec((PROJ_W, TR_W), lambda i: (0, i)),
        out_shape=jax.ShapeDtypeStruct((PROJ_W, k), BF16),
        compiler_params=_params(1),
        name="w_in_layout",
    )(wt)


def _cast_body(w_ref, o_ref):
    o_ref[...] = w_ref[...].astype(BF16)


def _to_bf16(w, layer):
    _, k, n = w.shape
    return pl.pallas_call(
        _cast_body,
        grid=(k // 512,),
        in_specs=[pl.BlockSpec((None, 512, n), lambda i: (layer, i, 0))],
        out_specs=pl.BlockSpec((512, n), lambda i: (i, 0)),
        out_shape=jax.ShapeDtypeStruct((k, n), BF16),
        compiler_params=_params(1),
        name="w_cast",
    )(w)


def _rope_tables(head_dim):
    r = head_dim // 4
    half = r // 2
    pos = jnp.arange(SEQ, dtype=jnp.int32)
    inv = ROPE_THETA ** (-jnp.arange(half, dtype=F32) * 2.0 / r)
    ang = pos.astype(F32)[:, None] * inv[None, :]
    cos, sin = jnp.cos(ang), jnp.sin(ang)
    one = jnp.ones((SEQ, head_dim - r), F32)
    zero = jnp.zeros((SEQ, head_dim - r), F32)
    zh = jnp.zeros((SEQ, half), F32)
    reps = LANES // head_dim
    return tuple(jnp.tile(t, (1, reps)) for t in (
        jnp.concatenate([cos, cos, one], axis=1),
        jnp.concatenate([-sin, zh, zero], axis=1),
        jnp.concatenate([zh, sin, zero], axis=1)))


def _overlap_t():
    cs = jnp.arange(N_CMP_PAD) * CMP_STRIDE
    ss = jnp.arange(N_SLC) * SLC_BLK
    ov = jnp.clip(jnp.minimum(cs[None, :] + CMP_LEN, ss[:, None] + SLC_BLK)
                  - jnp.maximum(cs[None, :], ss[:, None]), 0, None).astype(F32) / CMP_LEN
    return jnp.where(jnp.arange(N_CMP_PAD)[None, :] < N_CMP_PAD - 1, ov, 0.0).astype(BF16)


def kernel(x, norm_g, w_in, w_out, lru_conv_w, lru_conv_b, lru_wa, lru_ba, lru_wx, lru_bx, lru_lambda,
           nsa_q_gain, nsa_k_gain, cmp_pe_k, cmp_w1_k, cmp_w2_k, cmp_pe_v, cmp_w1_v, cmp_w2_v,
           dsa_q_gain, dsa_k_gain, sc_conv_w):
    b, s, d = x.shape
    assert (b, s, d) == (BATCH, SEQ, D_MODEL)
    tabs128 = _rope_tables(HEAD_DIM)
    tabs64 = _rope_tables(IDX_DIM)
    ovt = _overlap_t()
    x2 = x.reshape(b * s, d)
    w_in_t = jnp.swapaxes(w_in, 1, 2)

    def row(v):
        return v.reshape(1, -1)

    for l in range(DEPTH):
        z = _inproj(x2, row(norm_g[l]), _proj_weight(w_in_t, l))
        ya, yd = _mix_ad(z, lru_conv_w[l], row(lru_conv_b[l]), lru_wa[l].astype(BF16), row(lru_ba[l]),
                         lru_wx[l].astype(BF16), row(lru_bx[l]), row(lru_lambda[l]), sc_conv_w[l])
        qn, qr, ks, vst, kw, vwt, qd, kd, vdt, iq, ik = _prep(
            z, tabs128, tabs64, row(nsa_q_gain[l]), row(nsa_k_gain[l]), row(dsa_q_gain[l]), row(dsa_k_gain[l]))
        cmp_, cmpt = _compress(z, jnp.stack([cmp_pe_k[l], cmp_pe_v[l]]),
                               jnp.stack([cmp_w1_k[l], cmp_w1_v[l]]).astype(BF16),
                               jnp.stack([cmp_w2_k[l], cmp_w2_v[l]]).astype(BF16), row(nsa_k_gain[l]))
        yb = _nsa(z, qn, qr, cmp_, cmpt, ks, vst, kw, vwt, ovt)
        yc = _dsa(z, qd, kd, vdt, iq, ik)
        x2 = _outproj((ya, yb, yc, yd), _to_bf16(w_out, l), x2)
    return x2.reshape(b, s, d)
```

```python
import functools

import jax
import jax.numpy as jnp
from jax import lax
from jax.experimental import pallas as pl
from jax.experimental.pallas import tpu as pltpu

F32 = jnp.float32
BF16 = jnp.bfloat16
I32 = jnp.int32

D_MODEL = 4096
BATCH = 2
SEQ = 4096
DEPTH = 2
GROUP_W = D_MODEL // 4
HEAD_DIM = 128
N_HEADS = GROUP_W // HEAD_DIM
N_KV = N_HEADS // 4
HPG = N_HEADS // N_KV
ROPE_THETA = 500000.0
NORM_EPS = 1e-6
NEG = -1e30
FORCE = 1e6
LRU_BLOCKS = N_HEADS
LRU_C = 8.0
CONV_A = 4
CONV_D = 3
CMP_LEN = 32
CMP_STRIDE = 16
SLC_BLK = 64
SLC_TOPN = 16
WINDOW = 512
IDX_HEADS = 8
IDX_DIM = 64
DSA_TOPK = 256
SCALE = HEAD_DIM ** -0.5
Q_PRESCALE = SCALE * 1.4426950408889634
IW_SCALE = IDX_HEADS ** -0.5 * IDX_DIM ** -0.5

SPLIT_SIZES = (
    GROUP_W, GROUP_W,
    N_HEADS * HEAD_DIM, GROUP_W, 6 * N_KV * HEAD_DIM, 3 * N_HEADS,
    N_HEADS * HEAD_DIM, GROUP_W, 2 * N_KV * HEAD_DIM,
    IDX_HEADS * IDX_DIM, IDX_DIM, IDX_HEADS,
    GROUP_W, GROUP_W, GROUP_W, GROUP_W,
)

LANES = 128
COL_A_X, COL_A_G, COL_B_Q, COL_B_G, COL_C_Q, COL_C_G = 0, 1024, 2048, 3072, 4096, 5120
COL_D_IN, COL_D_B, COL_D_C, COL_D_G = 6144, 7168, 8192, 9216
COL_B_KV, COL_C_KV, COL_C_IQ = 10240, 11776, 12288
COL_B_GATE, COL_C_IK, COL_C_IW = 12800, 12928, 13056
PROJ_W = 13312

N_CMP_PAD = SEQ // CMP_STRIDE
N_SLC = SEQ // SLC_BLK
KV_TILE = 256
M_FLOOR = -1e20
SNAP_FIRST, SNAP_EVERY = 8, 4

VMEM_LIMIT_BYTES = 56 * 1024 * 1024

TM_IN, TN_IN = 512, 1024
TM_OUT, TN_OUT = 1024, 512
T_TOK = 256
TR_W = 128
TQ = 256

NT_DIMS = (((1,), (1,)), ((), ()))


def _params(n_axes):
    return pltpu.CompilerParams(dimension_semantics=("arbitrary",) * n_axes,
                                vmem_limit_bytes=VMEM_LIMIT_BYTES)


def _silu(x):
    return x * jax.nn.sigmoid(x)


def _fold8(x, op=jnp.add):
    parts = [x[r:r + 8] for r in range(0, x.shape[0], 8)]
    while len(parts) > 1:
        nxt = [op(parts[a], parts[a + 1]) for a in range(0, len(parts) - 1, 2)]
        if len(parts) % 2:
            nxt.append(parts[-1])
        parts = nxt
    return parts[0]


def _inproj_body(x_ref, g_ref, w_ref, o_ref, h_ref):
    @pl.when(pl.program_id(1) == 0)
    def _():
        def rows(r, carry):
            r0 = pl.multiple_of(r * 64, 64)
            x = x_ref[pl.ds(r0, 64), :]
            ms = jnp.mean(x * x, axis=-1, keepdims=True)
            h_ref[pl.ds(r0, 64), :] = ((x * lax.rsqrt(ms + NORM_EPS)) * g_ref[...]).astype(BF16)
            return carry
        lax.fori_loop(0, TM_IN // 64, rows, 0)

    o_ref[...] = lax.dot_general(h_ref[...], w_ref[...], NT_DIMS, preferred_element_type=F32)


def _inproj(x2, g, wt):
    m = x2.shape[0]
    return pl.pallas_call(
        _inproj_body,
        grid=(m // TM_IN, PROJ_W // TN_IN),
        in_specs=[pl.BlockSpec((TM_IN, D_MODEL), lambda i, j: (i, 0)),
                  pl.BlockSpec((1, D_MODEL), lambda i, j: (0, 0)),
                  pl.BlockSpec((TN_IN, D_MODEL), lambda i, j: (j, 0))],
        out_specs=pl.BlockSpec((TM_IN, TN_IN), lambda i, j: (i, j)),
        out_shape=jax.ShapeDtypeStruct((m, PROJ_W), F32),
        scratch_shapes=[pltpu.VMEM((TM_IN, D_MODEL), BF16)],
        compiler_params=_params(2),
        name="inproj",
    )(x2, g, wt)


def _outproj_body(ya, yb, yc, yd, wa, wb, wc, wd, x_ref, o_ref):
    acc = jnp.dot(ya[...], wa[...], preferred_element_type=F32)
    acc = acc + jnp.dot(yb[...], wb[...], preferred_element_type=F32)
    acc = acc + jnp.dot(yc[...], wc[...], preferred_element_type=F32)
    acc = acc + jnp.dot(yd[...], wd[...], preferred_element_type=F32)
    o_ref[...] = x_ref[...] + acc


def _outproj(ys, w, x2):
    m = x2.shape[0]
    yspec = pl.BlockSpec((TM_OUT, GROUP_W), lambda i, j: (i, 0))
    wspecs = [pl.BlockSpec((GROUP_W, TN_OUT), functools.partial(lambda i, j, k: (k, j), k=k)) for k in range(4)]
    xspec = pl.BlockSpec((TM_OUT, TN_OUT), lambda i, j: (i, j))
    return pl.pallas_call(
        _outproj_body,
        grid=(m // TM_OUT, D_MODEL // TN_OUT),
        in_specs=[yspec] * 4 + wspecs + [xspec],
        out_specs=xspec,
        out_shape=jax.ShapeDtypeStruct((m, D_MODEL), F32),
        compiler_params=_params(2),
        name="outproj",
    )(*ys, w, w, w, w, x2)


def _mix_ad_body(ax, ag, din, db, dc, dg, cw, cb, wa, ba, wx, bx, lam, scw,
                 ya, yd, xbuf, vbuf, a_s, b_s, hst):
    t = T_TOK

    @pl.when(pl.program_id(1) == 0)
    def _():
        xbuf[0:8, :] = jnp.zeros((8, GROUP_W), F32)
        vbuf[0:8, :] = jnp.zeros((8, GROUP_W), F32)
        hst[...] = jnp.zeros((8, GROUP_W), F32)

    xbuf[8:8 + t, :] = ax[...]
    for blk in range(LRU_BLOCKS):
        c = slice(blk * LANES, (blk + 1) * LANES)
        u = (cw[0:1, c] * xbuf[5:5 + t, c] + cw[1:2, c] * xbuf[6:6 + t, c]
             + cw[2:3, c] * xbuf[7:7 + t, c] + cw[3:4, c] * xbuf[8:8 + t, c]) + cb[:, c]
        ub = u.astype(BF16)
        r = jax.nn.sigmoid(jnp.dot(ub, wa[blk], preferred_element_type=F32) + ba[:, c])
        ig = jax.nn.sigmoid(jnp.dot(ub, wx[blk], preferred_element_type=F32) + bx[:, c])
        nl = -lam[:, c]
        sp = jnp.maximum(nl, 0.0) + jnp.log1p(jnp.exp(-jnp.abs(nl)))
        log_a = (-LRU_C * r) * sp
        a = jnp.exp(log_a)
        a_s[:, c] = a
        em1 = jnp.tanh(log_a) * (a * a + 1.0)
        b_s[:, c] = jnp.sqrt(-em1) * (ig * u)
    xbuf[0:8, :] = xbuf[t:t + 8, :]

    row = lax.broadcasted_iota(I32, (8, GROUP_W), 0)

    def group(gi, hprev):
        r0 = pl.multiple_of(gi * 8, 8)
        a = a_s[pl.ds(r0, 8), :]
        b = b_s[pl.ds(r0, 8), :]
        for d in (1, 2, 4):
            ok = row >= d
            b = jnp.where(ok, a * pltpu.roll(b, d, axis=0) + b, b)
            a = jnp.where(ok, a * pltpu.roll(a, d, axis=0), a)
        h = a * hprev + b
        b_s[pl.ds(r0, 8), :] = h
        return jnp.broadcast_to(h[7:8, :], (8, GROUP_W))

    hst[...] = lax.fori_loop(0, t // 8, group, hst[...])

    vbuf[8:8 + t, :] = dc[...] * din[...]
    for blk in range(LRU_BLOCKS):
        c = slice(blk * LANES, (blk + 1) * LANES)
        ya[:, c] = (b_s[:, c] * _silu(ag[:, c])).astype(BF16)
        conv = (scw[0:1, c] * vbuf[6:6 + t, c] + scw[1:2, c] * vbuf[7:7 + t, c]
                + scw[2:3, c] * vbuf[8:8 + t, c])
        yd[:, c] = ((db[:, c] * conv) * _silu(dg[:, c])).astype(BF16)
    vbuf[0:8, :] = vbuf[t:t + 8, :]


def _mix_ad(z, cw, cb, wa, ba, wx, bx, lam, scw):
    nt = SEQ // T_TOK

    def zspec(col):
        return pl.BlockSpec((T_TOK, GROUP_W), lambda b, i: (b * nt + i, col // GROUP_W))

    def full(a):
        return pl.BlockSpec(a.shape, lambda b, i: (0,) * a.ndim)

    small = (cw, cb, wa, ba, wx, bx, lam, scw)
    yspec = pl.BlockSpec((T_TOK, GROUP_W), lambda b, i: (b * nt + i, 0))
    return pl.pallas_call(
        _mix_ad_body,
        grid=(BATCH, nt),
        in_specs=[zspec(COL_A_X), zspec(COL_A_G), zspec(COL_D_IN), zspec(COL_D_B), zspec(COL_D_C),
                  zspec(COL_D_G)] + [full(a) for a in small],
        out_specs=[yspec, yspec],
        out_shape=[jax.ShapeDtypeStruct((BATCH * SEQ, GROUP_W), BF16)] * 2,
        scratch_shapes=[pltpu.VMEM((T_TOK + 8, GROUP_W), F32), pltpu.VMEM((T_TOK + 8, GROUP_W), F32),
                        pltpu.VMEM((T_TOK, GROUP_W), F32), pltpu.VMEM((T_TOK, GROUP_W), F32),
                        pltpu.VMEM((8, GROUP_W), F32)],
        compiler_params=_params(2),
        name="mix_ad",
    )(z, z, z, z, z, z, *small)


def _rms_head(x, gain):
    ms = jnp.mean(x * x, axis=-1, keepdims=True)
    return (x * lax.rsqrt(ms + NORM_EPS)) * gain


def _rope(y, cos, nsin, psin, half):
    return y * cos + pltpu.roll(y, LANES - half, axis=1) * nsin + pltpu.roll(y, half, axis=1) * psin


def _store_vt(dst, g, v):
    for cc in range(T_TOK // KV_TILE):
        dst[0, g, cc] = v[cc * KV_TILE:(cc + 1) * KV_TILE, :].T.astype(BF16)


def _prep_body(bq, bkv_s, bkv_w, cq, ckv, ciq, cik, c128, n128, p128, c64, n64, p64,
               qgn, kgn, qgd, kgd,
               qn_o, qr_o, ks_o, vst_o, kw_o, vwt_o, qd_o, kd_o, vdt_o, iq_o, ik_o):
    r128 = (c128[...], n128[...], p128[...], HEAD_DIM // 8)
    r64 = (c64[...], n64[...], p64[...], IDX_DIM // 8)
    for h in range(N_HEADS):
        c = slice(h * LANES, (h + 1) * LANES)
        y = _rms_head(bq[:, c], qgn[...])
        qn_o[:, c] = (y * Q_PRESCALE).astype(BF16)
        qr_o[:, c] = (_rope(y, *r128) * Q_PRESCALE).astype(BF16)
        qd_o[:, c] = (_rope(_rms_head(cq[:, c], qgd[...]), *r128) * Q_PRESCALE).astype(BF16)
    for g in range(N_KV):
        ck = slice(g * LANES, (g + 1) * LANES)
        cv = slice((N_KV + g) * LANES, (N_KV + g + 1) * LANES)
        ks_o[0, g] = _rope(_rms_head(bkv_s[:, ck], kgn[...]), *r128).astype(BF16)
        _store_vt(vst_o, g, bkv_s[:, cv])
        kw_o[0, g] = _rope(_rms_head(bkv_w[:, ck], kgn[...]), *r128).astype(BF16)
        _store_vt(vwt_o, g, bkv_w[:, cv])
        kd_o[0, g] = _rope(_rms_head(ckv[:, ck], kgd[...]), *r128).astype(BF16)
        _store_vt(vdt_o, g, ckv[:, cv])

    low = lax.broadcasted_iota(I32, (T_TOK, LANES), 1) < IDX_DIM
    ik = _rope(cik[...], *r64)
    hi = ik.astype(BF16).astype(F32)
    ik_o[0, :, 0:LANES] = (hi + pltpu.roll(ik - hi, IDX_DIM, axis=1)).astype(BF16)
    ik_o[0, :, LANES:2 * LANES] = hi.astype(BF16)
    for v in range(IDX_HEADS // 2):
        x = _rope(ciq[:, v * LANES:(v + 1) * LANES], *r64)
        for half in range(2):
            h = 2 * v + half
            xs = x if half == 0 else pltpu.roll(x, IDX_DIM, axis=1)
            xh = jnp.where(low, xs, 0.0)
            hi = xh.astype(BF16).astype(F32)
            iq_o[:, 2 * h * LANES:(2 * h + 1) * LANES] = (hi + pltpu.roll(hi, IDX_DIM, axis=1)).astype(BF16)
            iq_o[:, (2 * h + 1) * LANES:(2 * h + 2) * LANES] = (xh - hi).astype(BF16)


def _prep(z, tabs128, tabs64, qgn, kgn, qgd, kgd):
    nt = SEQ // T_TOK

    def zspec(col, width):
        return pl.BlockSpec((T_TOK, width), lambda b, i: (b * nt + i, col // width))

    tab = pl.BlockSpec((T_TOK, LANES), lambda b, i: (i, 0))
    gain = pl.BlockSpec((1, LANES), lambda b, i: (0, 0))
    qspec = pl.BlockSpec((T_TOK, GROUP_W), lambda b, i: (b * nt + i, 0))
    kspec = pl.BlockSpec((1, N_KV, T_TOK, LANES), lambda b, i: (b, 0, i, 0))
    vtspec = pl.BlockSpec((1, N_KV, T_TOK // KV_TILE, LANES, KV_TILE), lambda b, i: (b, 0, i, 0, 0))
    qshape = jax.ShapeDtypeStruct((BATCH * SEQ, GROUP_W), BF16)
    kshape = jax.ShapeDtypeStruct((BATCH, N_KV, SEQ, LANES), BF16)
    vtshape = jax.ShapeDtypeStruct((BATCH, N_KV, SEQ // KV_TILE, LANES, KV_TILE), BF16)
    return pl.pallas_call(
        _prep_body,
        grid=(BATCH, nt),
        in_specs=[zspec(COL_B_Q, GROUP_W), zspec(COL_B_KV + 512, 512), zspec(COL_B_KV + 1024, 512),
                  zspec(COL_C_Q, GROUP_W), zspec(COL_C_KV, 512), zspec(COL_C_IQ, 512), zspec(COL_C_IK, LANES)]
                 + [tab] * 6 + [gain] * 4,
        out_specs=[qspec, qspec, kspec, vtspec, kspec, vtspec, qspec, kspec, vtspec,
                   pl.BlockSpec((T_TOK, 2 * LANES * IDX_HEADS), lambda b, i: (b * nt + i, 0)),
                   pl.BlockSpec((1, T_TOK, 2 * LANES), lambda b, i: (b, i, 0))],
        out_shape=[qshape, qshape, kshape, vtshape, kshape, vtshape, qshape, kshape, vtshape,
                   jax.ShapeDtypeStruct((BATCH * SEQ, 2 * LANES * IDX_HEADS), BF16),
                   jax.ShapeDtypeStruct((BATCH, SEQ, 2 * LANES), BF16)],
        compiler_params=_params(2),
        name="attn_prep",
    )(z, z, z, z, z, z, z, *tabs128, *tabs64, qgn, kgn, qgd, kgd)


def _cmp_body(zc, pe, w1, w2, kg, o, ot, kbuf):
    kbuf[0:SEQ, :] = zc[...]
    kbuf[SEQ:SEQ + LANES, :] = jnp.zeros((LANES, LANES), F32)
    acc = jnp.zeros((N_CMP_PAD, LANES), F32)
    for l in range(CMP_LEN):
        rows = kbuf[pl.ds(l, N_CMP_PAD, stride=CMP_STRIDE), :] + pe[0, l:l + 1, :]
        acc = acc + jnp.dot(rows.astype(BF16), w1[0, l], preferred_element_type=F32)
    out = jnp.dot(_silu(acc).astype(BF16), w2[0], preferred_element_type=F32)
    res = jnp.where(pl.program_id(1) == 0, _rms_head(out, kg[...]), out)
    o[0, 0, 0] = res.astype(BF16)
    ot[0, 0, 0] = res.T.astype(BF16)


def _compress(z, pe, w1, w2, kg):
    base = COL_B_KV // LANES
    return pl.pallas_call(
        _cmp_body,
        grid=(BATCH, 2, N_KV),
        in_specs=[pl.BlockSpec((SEQ, LANES), lambda b, kv, g: (b, base + kv * N_KV + g)),
                  pl.BlockSpec((1, CMP_LEN, LANES), lambda b, kv, g: (kv, 0, 0)),
                  pl.BlockSpec((1, CMP_LEN, LANES, LANES), lambda b, kv, g: (kv, 0, 0, 0)),
                  pl.BlockSpec((1, LANES, LANES), lambda b, kv, g: (kv, 0, 0)),
                  pl.BlockSpec((1, LANES), lambda b, kv, g: (0, 0))],
        out_specs=[pl.BlockSpec((1, 1, 1, N_CMP_PAD, LANES), lambda b, kv, g: (b, kv, g, 0, 0)),
                   pl.BlockSpec((1, 1, 1, LANES, N_CMP_PAD), lambda b, kv, g: (b, kv, g, 0, 0))],
        out_shape=[jax.ShapeDtypeStruct((BATCH, 2, N_KV, N_CMP_PAD, LANES), BF16),
                   jax.ShapeDtypeStruct((BATCH, 2, N_KV, LANES, N_CMP_PAD), BF16)],
        scratch_shapes=[pltpu.VMEM((SEQ + LANES, LANES), F32)],
        compiler_params=_params(3),
        name="nsa_compress",
    )(z, pe, w1, w2, kg)


def _flash_heads(q_ref, slot0, groups, jlo, jhi, m_s, l_s, acc_s):
    n_heads = sum(len(g[2]) for g in groups)
    lo, hi = slot0, slot0 + n_heads
    m_s[lo:hi, :] = jnp.full((n_heads, TQ), NEG, F32)
    l_s[lo:hi, :] = jnp.zeros((n_heads, TQ), F32)
    for slot in range(lo, hi):
        acc_s[slot] = jnp.zeros((HEAD_DIM, TQ), F32)

    def body(j, carry):
        r0 = pl.multiple_of(j * KV_TILE, KV_TILE)
        m_old = m_s[lo:hi, :]
        l_old = l_s[lo:hi, :]
        ss, vts, biases = [], [], {}
        for k_ref, vt_ref, cs, bias_fn in groups:
            k = k_ref[pl.ds(r0, KV_TILE), :]
            vt = vt_ref[j]
            if bias_fn not in biases:
                biases[bias_fn] = bias_fn(j)
            b = biases[bias_fn]
            for c in cs:
                ss.append(lax.dot_general(k, q_ref[:, c], NT_DIMS, preferred_element_type=F32) + b)
                vts.append(vt)
        m_new = jnp.maximum(m_old, jnp.concatenate([jnp.max(s, axis=0, keepdims=True) for s in ss], axis=0))
        m_safe = jnp.maximum(m_new, M_FLOOR)
        alpha = jnp.exp2(m_old - m_safe)
        ps = [jnp.exp2(s - m_safe[n:n + 1, :]) for n, s in enumerate(ss)]
        l_s[lo:hi, :] = alpha * l_old + jnp.concatenate([jnp.sum(p, axis=0, keepdims=True) for p in ps], axis=0)
        m_s[lo:hi, :] = m_new
        pv = [jnp.dot(vt, p.astype(BF16), preferred_element_type=F32) for vt, p in zip(vts, ps)]
        for n in range(n_heads):
            acc_s[lo + n] = alpha[n:n + 1, :] * acc_s[lo + n] + pv[n]
        return carry

    lax.fori_loop(jlo, jhi, body, 0)


def _flash_out(slot, l_s, acc_s):
    return acc_s[slot] * (1.0 / l_s[slot:slot + 1, :])


def _nsa_body(qn, qr, kc, vct, ks, vst, kw, vwt, gate, bg, ovt, yb, imp_s, bias_s, m_s, l_s, acc_s):
    i = pl.program_id(1)
    tq = i * TQ + lax.broadcasted_iota(I32, (1, TQ), 1)
    n_kv = i + 1
    sg = jax.nn.sigmoid(gate[...].T[0:3 * N_HEADS, :])
    cm = (lax.broadcasted_iota(I32, (N_CMP_PAD, TQ), 0) * CMP_STRIDE + (CMP_LEN - 1)) <= tq
    kk = lax.broadcasted_iota(I32, (N_SLC, TQ), 0)
    sub8 = lax.broadcasted_iota(I32, (8, TQ), 0)
    blk_t = lax.shift_right_logical(tq, 6)
    valid = kk <= blk_t
    force = (kk == 0) | (kk == blk_t) | (kk == blk_t - 1)
    rowkv = lax.broadcasted_iota(I32, (KV_TILE, TQ), 0)
    cmp_slot, slc_slot, win_slot = 0, N_HEADS, 2 * N_HEADS
    head_cols = [[slice(h * LANES, (h + 1) * LANES) for h in range(HPG * g, HPG * (g + 1))]
                 for g in range(N_KV)]

    def win_bias(j):
        d = tq - (j * KV_TILE + rowkv)
        return jnp.where((d >= 0) & (d < WINDOW), 0.0, NEG)

    def slc_bias(g):
        return lambda j: bias_s[g, pl.ds(pl.multiple_of(j * KV_TILE, KV_TILE), KV_TILE), :]

    for g in range(N_KV):
        kcg = kc[0, 0, g]
        vcg = vct[0, 0, g]
        cols = head_cols[g]
        ss = [jnp.where(cm, lax.dot_general(kcg, qn[:, c], NT_DIMS, preferred_element_type=F32), NEG)
              for c in cols]
        es = [jnp.exp2(s - jnp.max(s, axis=0, keepdims=True)) for s in ss]
        ps = [jnp.where(cm, e * (1.0 / jnp.sum(e, axis=0, keepdims=True)), 0.0) for e in es]
        for hh, p in enumerate(ps):
            acc_s[cmp_slot + HPG * g + hh] = jnp.dot(vcg, p.astype(BF16), preferred_element_type=F32)
        psum = (ps[0] + ps[1]) + (ps[2] + ps[3])

        p_hi = psum.astype(BF16)
        p_lo = (psum - p_hi.astype(F32)).astype(BF16)
        imp = (jnp.dot(ovt[...], p_hi, preferred_element_type=F32)
               + jnp.dot(ovt[...], p_lo, preferred_element_type=F32))
        imp = jnp.where(force, FORCE, jnp.where(valid, imp, -FORCE))
        imp_s[...] = imp
        bands = [imp[8 * r:8 * r + 8] for r in range(N_SLC // 8)]
        ranks = [jnp.zeros((8, TQ), F32) for _ in bands]
        for j in range(N_SLC):
            vj = imp_s[j:j + 1, :]
            for r, x in enumerate(bands):
                if r < j // 8:
                    hit = vj > x
                elif r > j // 8:
                    hit = vj >= x
                else:
                    hit = (vj > x) | ((vj == x) & (sub8 > j % 8))
                ranks[r] = ranks[r] + jnp.where(hit, 1.0, 0.0)
        imp_s[...] = jnp.where(jnp.concatenate(ranks, axis=0) < SLC_TOPN, 0.0, NEG)

        def fill(j, carry):
            r0 = pl.multiple_of(j * KV_TILE, KV_TILE)
            nb = KV_TILE // SLC_BLK
            tile = jnp.concatenate([jnp.broadcast_to(imp_s[pl.ds(nb * j + u, 1), :], (SLC_BLK, TQ))
                                    for u in range(nb)], axis=0)
            bias_s[g, pl.ds(r0, KV_TILE), :] = jnp.where(r0 + rowkv <= tq, tile, NEG)
            return carry
        lax.fori_loop(0, n_kv, fill, 0)

    _flash_heads(qr, slc_slot, [(ks.at[0, g], vst.at[0, g], head_cols[g], slc_bias(g)) for g in range(N_KV)],
                 0, n_kv, m_s, l_s, acc_s)
    _flash_heads(qr, win_slot, [(kw.at[0, g], vwt.at[0, g], head_cols[g], win_bias) for g in range(N_KV)],
                 jnp.maximum(n_kv - 1 - WINDOW // KV_TILE, 0), n_kv, m_s, l_s, acc_s)
    for h in range(N_HEADS):
        c = slice(h * LANES, (h + 1) * LANES)
        tot = (sg[h:h + 1] * acc_s[cmp_slot + h]
               + sg[N_HEADS + h:N_HEADS + h + 1] * _flash_out(slc_slot + h, l_s, acc_s)
               + sg[2 * N_HEADS + h:2 * N_HEADS + h + 1] * _flash_out(win_slot + h, l_s, acc_s))
        yb[:, c] = (tot.T * _silu(bg[:, c])).astype(BF16)


def _nsa(z, qn, qr, cmp_, cmpt, ks, vst, kw, vwt, ovt):
    nq = SEQ // TQ
    qspec = pl.BlockSpec((TQ, GROUP_W), lambda b, i: (b * nq + i, 0))
    kspec = pl.BlockSpec((1, N_KV, SEQ, LANES), lambda b, i: (b, 0, 0, 0))
    vtspec = pl.BlockSpec((1, N_KV, SEQ // KV_TILE, LANES, KV_TILE), lambda b, i: (b, 0, 0, 0, 0))
    return pl.pallas_call(
        _nsa_body,
        grid=(BATCH, nq),
        in_specs=[qspec, qspec,
                  pl.BlockSpec((1, 1, N_KV, N_CMP_PAD, LANES), lambda b, i: (b, 0, 0, 0, 0)),
                  pl.BlockSpec((1, 1, N_KV, LANES, N_CMP_PAD), lambda b, i: (b, 1, 0, 0, 0)),
                  kspec, vtspec, kspec, vtspec,
                  pl.BlockSpec((TQ, LANES), lambda b, i: (b * nq + i, COL_B_GATE // LANES)),
                  pl.BlockSpec((TQ, GROUP_W), lambda b, i: (b * nq + i, COL_B_G // GROUP_W)),
                  pl.BlockSpec((N_SLC, N_CMP_PAD), lambda b, i: (0, 0))],
        out_specs=qspec,
        out_shape=jax.ShapeDtypeStruct((BATCH * SEQ, GROUP_W), BF16),
        scratch_shapes=[pltpu.VMEM((N_SLC, TQ), F32), pltpu.VMEM((N_KV, SEQ, TQ), F32),
                        pltpu.VMEM((3 * N_HEADS, TQ), F32), pltpu.VMEM((3 * N_HEADS, TQ), F32),
                        pltpu.VMEM((3 * N_HEADS, HEAD_DIM, TQ), F32)],
        compiler_params=_params(2),
        name="nsa_attn",
    )(qn, qr, cmp_, cmpt, ks, vst, kw, vwt, z, z, ovt)


def _dsa_body(qd, kd, vdt, iq, ik, iw, cg, yc, sc_s, m_s, l_s, acc_s):
    i = pl.program_id(1)
    tq = i * TQ + lax.broadcasted_iota(I32, (1, TQ), 1)
    n_kv = i + 1
    iwt = iw[...].T[0:IDX_HEADS, :] * IW_SCALE
    rowkv = lax.broadcasted_iota(I32, (KV_TILE, TQ), 0)
    kf = float(DSA_TOPK)

    def fill(j, carry):
        mn, mx = carry
        r0 = pl.multiple_of(j * KV_TILE, KV_TILE)
        ikc = ik[0, pl.ds(r0, KV_TILE), :]
        sc = jnp.zeros((KV_TILE, TQ), F32)
        for h in range(IDX_HEADS):
            r = lax.dot_general(ikc, iq[:, 2 * h * LANES:(2 * h + 2) * LANES], NT_DIMS,
                                preferred_element_type=F32)
            sc = sc + jnp.maximum(r, 0.0) * iwt[h:h + 1, :]
        ok = r0 + rowkv <= tq
        sc_s[pl.ds(r0, KV_TILE), :] = jnp.where(ok, sc, -jnp.inf)
        mn = jnp.minimum(mn, jnp.min(jnp.where(ok, sc, jnp.inf), axis=0, keepdims=True))
        mx = jnp.maximum(mx, jnp.max(jnp.where(ok, sc, -jnp.inf), axis=0, keepdims=True))
        return mn, mx

    lo0, hi0 = lax.fori_loop(0, n_kv, fill,
                             (jnp.full((1, TQ), jnp.inf, F32), jnp.full((1, TQ), -jnp.inf, F32)))

    def sweep(fn, n_out):
        def body(c, accs):
            r0 = pl.multiple_of(c * KV_TILE, KV_TILE)
            inds = fn(sc_s[pl.ds(r0, KV_TILE), :], r0 + rowkv)
            return tuple(a + _fold8(jnp.where(ind, 1.0, 0.0)) for a, ind in zip(accs, inds))
        accs = lax.fori_loop(0, n_kv, body, tuple(jnp.zeros((8, TQ), F32) for _ in range(n_out)))
        return tuple(jnp.sum(a, axis=0, keepdims=True) for a in accs)

    def data_span(lo, hi):
        def body(c, carry):
            mn, mx = carry
            x = sc_s[pl.ds(pl.multiple_of(c * KV_TILE, KV_TILE), KV_TILE), :]
            mn = jnp.minimum(mn, _fold8(jnp.where(x >= lo, x, jnp.inf), jnp.minimum))
            mx = jnp.maximum(mx, _fold8(jnp.where(x <= hi, x, -jnp.inf), jnp.maximum))
            return mn, mx
        mn, mx = lax.fori_loop(0, n_kv, body, (jnp.full((8, TQ), jnp.inf, F32), jnp.full((8, TQ), -jnp.inf, F32)))
        return jnp.min(mn, axis=0, keepdims=True), jnp.max(mx, axis=0, keepdims=True)

    nvalid = (tq + 1).astype(F32)
    act0 = jnp.where(nvalid > kf, 1.0, 0.0)

    def bisect(_, c):
        lo, hi, clo, act = c
        on = act > 0.0
        mid = lo + (hi - lo) * 0.5
        exh = (mid <= lo) | (mid >= hi)
        p = jnp.where(exh, hi, mid)
        (cnt,) = sweep(lambda x, srow: (x >= p,), 1)
        up = on & (cnt >= kf)
        dn = on & (cnt < kf)
        clo = jnp.where(up, cnt, clo)
        act = jnp.where(on & (clo != kf) & jnp.logical_not(exh), 1.0, 0.0)
        return jnp.where(up, p, lo), jnp.where(dn, p, hi), clo, act

    def snap(c):
        lo, hi, clo, act = c
        on = act > 0.0
        dmin, dmax = data_span(lo, hi)
        act = jnp.where(on & (dmin < dmax), 1.0, 0.0)
        return jnp.where(on, dmin, lo), jnp.where(on, dmax, hi), clo, act

    def cond(c):
        return c[0] > 0.0

    def round_(c):
        state = lax.fori_loop(0, SNAP_EVERY - 1, bisect, snap(c[1:]))
        return (jnp.sum(state[3]),) + state

    state = lax.fori_loop(0, SNAP_FIRST, bisect, (lo0, hi0, nvalid, act0))
    _, thr, _, _, _ = lax.while_loop(cond, round_, (jnp.sum(state[3]),) + state)

    cnt_g, cnt_e = sweep(lambda x, srow: (x > thr, x == thr), 2)
    need = kf - cnt_g
    excess = (cnt_g + cnt_e) > kf

    def tie_break():
        def it(_, c):
            jl, jh = c
            jm = lax.shift_right_arithmetic(jl + jh, 1)
            (cnt,) = sweep(lambda x, srow: ((x == thr) & (srow <= jm),), 1)
            ge = cnt >= need
            return jnp.where(ge, jl, jm), jnp.where(ge, jm, jh)
        _, jh = lax.fori_loop(0, 12, it, (jnp.full((1, TQ), -1, I32), jnp.full((1, TQ), SEQ - 1, I32)))
        return jnp.where(excess, jh, SEQ)

    last = lax.cond(jnp.sum(jnp.where(excess, 1.0, 0.0)) > 0.0, tie_break,
                    lambda: jnp.full((1, TQ), SEQ, I32))

    def to_bias(c, carry):
        r0 = pl.multiple_of(c * KV_TILE, KV_TILE)
        x = sc_s[pl.ds(r0, KV_TILE), :]
        sel = (x > thr) | ((x == thr) & (r0 + rowkv <= last))
        sc_s[pl.ds(r0, KV_TILE), :] = jnp.where(sel, 0.0, NEG)
        return carry
    lax.fori_loop(0, n_kv, to_bias, 0)

    def bias(j):
        return sc_s[pl.ds(pl.multiple_of(j * KV_TILE, KV_TILE), KV_TILE), :]

    groups = [(kd.at[0, g], vdt.at[0, g], [slice(h * LANES, (h + 1) * LANES)
                                           for h in range(HPG * g, HPG * (g + 1))], bias) for g in range(N_KV)]
    _flash_heads(qd, 0, groups, 0, n_kv, m_s, l_s, acc_s)
    for h in range(N_HEADS):
        c = slice(h * LANES, (h + 1) * LANES)
        yc[:, c] = (_flash_out(h, l_s, acc_s).T * _silu(cg[:, c])).astype(BF16)


def _dsa(z, qd, kd, vdt, iq, ik):
    nq = SEQ // TQ
    qspec = pl.BlockSpec((TQ, GROUP_W), lambda b, i: (b * nq + i, 0))
    return pl.pallas_call(
        _dsa_body,
        grid=(BATCH, nq),
        in_specs=[qspec,
                  pl.BlockSpec((1, N_KV, SEQ, LANES), lambda b, i: (b, 0, 0, 0)),
                  pl.BlockSpec((1, N_KV, SEQ // KV_TILE, LANES, KV_TILE), lambda b, i: (b, 0, 0, 0, 0)),
                  pl.BlockSpec((TQ, 2 * LANES * IDX_HEADS), lambda b, i: (b * nq + i, 0)),
                  pl.BlockSpec((1, SEQ, 2 * LANES), lambda b, i: (b, 0, 0)),
                  pl.BlockSpec((TQ, LANES), lambda b, i: (b * nq + i, COL_C_IW // LANES)),
                  pl.BlockSpec((TQ, GROUP_W), lambda b, i: (b * nq + i, COL_C_G // GROUP_W))],
        out_specs=qspec,
        out_shape=jax.ShapeDtypeStruct((BATCH * SEQ, GROUP_W), BF16),
        scratch_shapes=[pltpu.VMEM((SEQ, TQ), F32), pltpu.VMEM((N_HEADS, TQ), F32),
                        pltpu.VMEM((N_HEADS, TQ), F32), pltpu.VMEM((N_HEADS, HEAD_DIM, TQ), F32)],
        compiler_params=_params(2),
        name="dsa_attn",
    )(qd, kd, vdt, iq, ik, z, z)


def _segment_moves():
    src = {}
    o = 0
    names = ("a_x", "a_g", "b_q", "b_g", "b_kv", "b_gate", "c_q", "c_g", "c_kv", "c_iq", "c_ik", "c_iw",
             "d_in", "d_b", "d_c", "d_g")
    for n, s in zip(names, SPLIT_SIZES):
        src[n] = (o, s)
        o += s
    dst = dict(a_x=COL_A_X, a_g=COL_A_G, b_q=COL_B_Q, b_g=COL_B_G, c_q=COL_C_Q, c_g=COL_C_G,
               d_in=COL_D_IN, d_b=COL_D_B, d_c=COL_D_C, d_g=COL_D_G, b_kv=COL_B_KV, c_kv=COL_C_KV,
               c_iq=COL_C_IQ, b_gate=COL_B_GATE, c_ik=COL_C_IK, c_iw=COL_C_IW)
    return [(src[n][0], dst[n], src[n][1]) for n in names]


def _wprep_body(w_ref, o_ref):
    o_ref[COL_B_GATE:PROJ_W, :] = jnp.zeros((PROJ_W - COL_B_GATE, TR_W), BF16)
    for s, d, width in _segment_moves():
        for c in range(0, width, GROUP_W):
            n = min(GROUP_W, width - c)
            o_ref[d + c:d + c + n, :] = w_ref[s + c:s + c + n, :].astype(BF16)


def _proj_weight(wt, layer):
    _, n, k = wt.shape
    return pl.pallas_call(
        _wprep_body,
        grid=(k // TR_W,),
        in_specs=[pl.BlockSpec((None, n, TR_W), lambda i: (layer, 0, i))],
        out_specs=pl.BlockSpec((PROJ_W, TR_W), lambda i: (0, i)),
        out_shape=jax.ShapeDtypeStruct((PROJ_W, k), BF16),
        compiler_params=_params(1),
        name="w_in_layout",
    )(wt)


def _cast_body(w_ref, o_ref):
    o_ref[...] = w_ref[...].astype(BF16)


def _to_bf16(w, layer):
    _, k, n = w.shape
    return pl.pallas_call(
        _cast_body,
        grid=(k // 512,),
        in_specs=[pl.BlockSpec((None, 512, n), lambda i: (layer, i, 0))],
        out_specs=pl.BlockSpec((512, n), lambda i: (i, 0)),
        out_shape=jax.ShapeDtypeStruct((k, n), BF16),
        compiler_params=_params(1),
        name="w_cast",
    )(w)


def _rope_tables(head_dim):
    r = head_dim // 4
    half = r // 2
    pos = jnp.arange(SEQ, dtype=jnp.int32)
    inv = ROPE_THETA ** (-jnp.arange(half, dtype=F32) * 2.0 / r)
    ang = pos.astype(F32)[:, None] * inv[None, :]
    cos, sin = jnp.cos(ang), jnp.sin(ang)
    one = jnp.ones((SEQ, head_dim - r), F32)
    zero = jnp.zeros((SEQ, head_dim - r), F32)
    zh = jnp.zeros((SEQ, half), F32)
    reps = LANES // head_dim
    return tuple(jnp.tile(t, (1, reps)) for t in (
        jnp.concatenate([cos, cos, one], axis=1),
        jnp.concatenate([-sin, zh, zero], axis=1),
        jnp.concatenate([zh, sin, zero], axis=1)))


def _overlap_t():
    cs = jnp.arange(N_CMP_PAD) * CMP_STRIDE
    ss = jnp.arange(N_SLC) * SLC_BLK
    ov = jnp.clip(jnp.minimum(cs[None, :] + CMP_LEN, ss[:, None] + SLC_BLK)
                  - jnp.maximum(cs[None, :], ss[:, None]), 0, None).astype(F32) / CMP_LEN
    return jnp.where(jnp.arange(N_CMP_PAD)[None, :] < N_CMP_PAD - 1, ov, 0.0).astype(BF16)


def kernel(x, norm_g, w_in, w_out, lru_conv_w, lru_conv_b, lru_wa, lru_ba, lru_wx, lru_bx, lru_lambda,
           nsa_q_gain, nsa_k_gain, cmp_pe_k, cmp_w1_k, cmp_w2_k, cmp_pe_v, cmp_w1_v, cmp_w2_v,
           dsa_q_gain, dsa_k_gain, sc_conv_w):
    b, s, d = x.shape
    assert (b, s, d) == (BATCH, SEQ, D_MODEL)
    tabs128 = _rope_tables(HEAD_DIM)
    tabs64 = _rope_tables(IDX_DIM)
    ovt = _overlap_t()
    x2 = x.reshape(b * s, d)
    w_in_t = jnp.swapaxes(w_in, 1, 2)

    def row(v):
        return v.reshape(1, -1)

    for l in range(DEPTH):
        z = _inproj(x2, row(norm_g[l]), _proj_weight(w_in_t, l))
        ya, yd = _mix_ad(z, lru_conv_w[l], row(lru_conv_b[l]), lru_wa[l].astype(BF16), row(lru_ba[l]),
                         lru_wx[l].astype(BF16), row(lru_bx[l]), row(lru_lambda[l]), sc_conv_w[l])
        qn, qr, ks, vst, kw, vwt, qd, kd, vdt, iq, ik = _prep(
            z, tabs128, tabs64, row(nsa_q_gain[l]), row(nsa_k_gain[l]), row(dsa_q_gain[l]), row(dsa_k_gain[l]))
        cmp_, cmpt = _compress(z, jnp.stack([cmp_pe_k[l], cmp_pe_v[l]]),
                               jnp.stack([cmp_w1_k[l], cmp_w1_v[l]]).astype(BF16),
                               jnp.stack([cmp_w2_k[l], cmp_w2_v[l]]).astype(BF16), row(nsa_k_gain[l]))
        yb = _nsa(z, qn, qr, cmp_, cmpt, ks, vst, kw, vwt, ovt)
        yc = _dsa(z, qd, kd, vdt, iq, ik)
        x2 = _outproj((ya, yb, yc, yd), _to_bf16(w_out, l), x2)
    return x2.reshape(b, s, d)
```

```python
import functools

import jax
import jax.numpy as jnp
from jax import lax
from jax.experimental import pallas as pl
from jax.experimental.pallas import tpu as pltpu

F32 = jnp.float32
BF16 = jnp.bfloat16
I32 = jnp.int32

D_MODEL = 4096
BATCH = 2
SEQ = 4096
DEPTH = 2
GROUP_W = D_MODEL // 4
HEAD_DIM = 128
N_HEADS = GROUP_W // HEAD_DIM
N_KV = N_HEADS // 4
HPG = N_HEADS // N_KV
ROPE_THETA = 500000.0
NORM_EPS = 1e-6
NEG = -1e30
FORCE = 1e6
LRU_BLOCKS = N_HEADS
LRU_C = 8.0
CONV_A = 4
CONV_D = 3
CMP_LEN = 32
CMP_STRIDE = 16
SLC_BLK = 64
SLC_TOPN = 16
WINDOW = 512
IDX_HEADS = 8
IDX_DIM = 64
DSA_TOPK = 256
SCALE = HEAD_DIM ** -0.5
Q_PRESCALE = SCALE * 1.4426950408889634
IW_SCALE = IDX_HEADS ** -0.5 * IDX_DIM ** -0.5

SPLIT_SIZES = (
    GROUP_W, GROUP_W,
    N_HEADS * HEAD_DIM, GROUP_W, 6 * N_KV * HEAD_DIM, 3 * N_HEADS,
    N_HEADS * HEAD_DIM, GROUP_W, 2 * N_KV * HEAD_DIM,
    IDX_HEADS * IDX_DIM, IDX_DIM, IDX_HEADS,
    GROUP_W, GROUP_W, GROUP_W, GROUP_W,
)

LANES = 128
COL_A_X, COL_A_G, COL_B_Q, COL_B_G, COL_C_Q, COL_C_G = 0, 1024, 2048, 3072, 4096, 5120
COL_D_IN, COL_D_B, COL_D_C, COL_D_G = 6144, 7168, 8192, 9216
COL_B_KV, COL_C_KV, COL_C_IQ = 10240, 11776, 12288
COL_B_GATE, COL_C_IK, COL_C_IW = 12800, 12928, 13056
PROJ_W = 13312

N_CMP_PAD = SEQ // CMP_STRIDE
N_SLC = SEQ // SLC_BLK
KV_TILE = 256
M_FLOOR = -1e20
BOUND_SLACK = 1.01
BOUND_LIMIT = 60.0
SNAP_FIRST, SNAP_EVERY = 8, 4

VMEM_LIMIT_BYTES = 56 * 1024 * 1024

TM_IN, TN_IN = 1024, 512
TM_OUT, TN_OUT = 1024, 512
T_TOK = 256
TR_W = 128
TQ = 256

NT_DIMS = (((1,), (1,)), ((), ()))


def _params(n_axes):
    return pltpu.CompilerParams(dimension_semantics=("arbitrary",) * n_axes,
                                vmem_limit_bytes=VMEM_LIMIT_BYTES)


def _silu(x):
    return x * jax.nn.sigmoid(x)


def _fold8(x, op=jnp.add):
    parts = [x[r:r + 8] for r in range(0, x.shape[0], 8)]
    while len(parts) > 1:
        nxt = [op(parts[a], parts[a + 1]) for a in range(0, len(parts) - 1, 2)]
        if len(parts) % 2:
            nxt.append(parts[-1])
        parts = nxt
    return parts[0]


def _inproj_body(x_ref, g_ref, w_ref, o_ref, h_ref):
    @pl.when(pl.program_id(1) == 0)
    def _():
        def rows(r, carry):
            r0 = pl.multiple_of(r * 64, 64)
            x = x_ref[pl.ds(r0, 64), :]
            ms = jnp.mean(x * x, axis=-1, keepdims=True)
            h_ref[pl.ds(r0, 64), :] = ((x * lax.rsqrt(ms + NORM_EPS)) * g_ref[...]).astype(BF16)
            return carry
        lax.fori_loop(0, TM_IN // 64, rows, 0)

    o_ref[...] = lax.dot_general(h_ref[...], w_ref[...], NT_DIMS, preferred_element_type=F32)


def _inproj(x2, g, wt):
    m = x2.shape[0]
    return pl.pallas_call(
        _inproj_body,
        grid=(m // TM_IN, PROJ_W // TN_IN),
        in_specs=[pl.BlockSpec((TM_IN, D_MODEL), lambda i, j: (i, 0)),
                  pl.BlockSpec((1, D_MODEL), lambda i, j: (0, 0)),
                  pl.BlockSpec((TN_IN, D_MODEL), lambda i, j: (j, 0))],
        out_specs=pl.BlockSpec((TM_IN, TN_IN), lambda i, j: (i, j)),
        out_shape=jax.ShapeDtypeStruct((m, PROJ_W), F32),
        scratch_shapes=[pltpu.VMEM((TM_IN, D_MODEL), BF16)],
        compiler_params=_params(2),
        name="inproj",
    )(x2, g, wt)


def _outproj_body(ya, yb, yc, yd, wa, wb, wc, wd, x_ref, o_ref):
    acc = jnp.dot(ya[...], wa[...], preferred_element_type=F32)
    acc = acc + jnp.dot(yb[...], wb[...], preferred_element_type=F32)
    acc = acc + jnp.dot(yc[...], wc[...], preferred_element_type=F32)
    acc = acc + jnp.dot(yd[...], wd[...], preferred_element_type=F32)
    o_ref[...] = x_ref[...] + acc


def _outproj(ys, w, x2):
    m = x2.shape[0]
    yspec = pl.BlockSpec((TM_OUT, GROUP_W), lambda i, j: (i, 0))
    wspecs = [pl.BlockSpec((GROUP_W, TN_OUT), functools.partial(lambda i, j, k: (k, j), k=k)) for k in range(4)]
    xspec = pl.BlockSpec((TM_OUT, TN_OUT), lambda i, j: (i, j))
    return pl.pallas_call(
        _outproj_body,
        grid=(m // TM_OUT, D_MODEL // TN_OUT),
        in_specs=[yspec] * 4 + wspecs + [xspec],
        out_specs=xspec,
        out_shape=jax.ShapeDtypeStruct((m, D_MODEL), F32),
        compiler_params=_params(2),
        name="outproj",
    )(*ys, w, w, w, w, x2)


def _mix_ad_body(ax, ag, din, db, dc, dg, cw, cb, wa, ba, wx, bx, lam, scw,
                 ya, yd, xbuf, vbuf, a_s, b_s, hst):
    t = T_TOK

    @pl.when(pl.program_id(1) == 0)
    def _():
        xbuf[0:8, :] = jnp.zeros((8, GROUP_W), F32)
        vbuf[0:8, :] = jnp.zeros((8, GROUP_W), F32)
        hst[...] = jnp.zeros((8, GROUP_W), F32)

    xbuf[8:8 + t, :] = ax[...]
    for blk in range(LRU_BLOCKS):
        c = slice(blk * LANES, (blk + 1) * LANES)
        u = (cw[0:1, c] * xbuf[5:5 + t, c] + cw[1:2, c] * xbuf[6:6 + t, c]
             + cw[2:3, c] * xbuf[7:7 + t, c] + cw[3:4, c] * xbuf[8:8 + t, c]) + cb[:, c]
        ub = u.astype(BF16)
        r = jax.nn.sigmoid(jnp.dot(ub, wa[blk], preferred_element_type=F32) + ba[:, c])
        ig = jax.nn.sigmoid(jnp.dot(ub, wx[blk], preferred_element_type=F32) + bx[:, c])
        nl = -lam[:, c]
        sp = jnp.maximum(nl, 0.0) + jnp.log1p(jnp.exp(-jnp.abs(nl)))
        log_a = (-LRU_C * r) * sp
        a = jnp.exp(log_a)
        a_s[:, c] = a
        em1 = jnp.tanh(log_a) * (a * a + 1.0)
        b_s[:, c] = jnp.sqrt(-em1) * (ig * u)
    xbuf[0:8, :] = xbuf[t:t + 8, :]

    row = lax.broadcasted_iota(I32, (8, GROUP_W), 0)

    def group(gi, hprev):
        r0 = pl.multiple_of(gi * 8, 8)
        a = a_s[pl.ds(r0, 8), :]
        b = b_s[pl.ds(r0, 8), :]
        for d in (1, 2, 4):
            ok = row >= d
            b = jnp.where(ok, a * pltpu.roll(b, d, axis=0) + b, b)
            a = jnp.where(ok, a * pltpu.roll(a, d, axis=0), a)
        h = a * hprev + b
        b_s[pl.ds(r0, 8), :] = h
        return jnp.broadcast_to(h[7:8, :], (8, GROUP_W))

    hst[...] = lax.fori_loop(0, t // 8, group, hst[...])

    vbuf[8:8 + t, :] = dc[...] * din[...]
    for blk in range(LRU_BLOCKS):
        c = slice(blk * LANES, (blk + 1) * LANES)
        ya[:, c] = (b_s[:, c] * _silu(ag[:, c])).astype(BF16)
        conv = (scw[0:1, c] * vbuf[6:6 + t, c] + scw[1:2, c] * vbuf[7:7 + t, c]
                + scw[2:3, c] * vbuf[8:8 + t, c])
        yd[:, c] = ((db[:, c] * conv) * _silu(dg[:, c])).astype(BF16)
    vbuf[0:8, :] = vbuf[t:t + 8, :]


def _mix_ad(z, cw, cb, wa, ba, wx, bx, lam, scw):
    nt = SEQ // T_TOK

    def zspec(col):
        return pl.BlockSpec((T_TOK, GROUP_W), lambda b, i: (b * nt + i, col // GROUP_W))

    def full(a):
        return pl.BlockSpec(a.shape, lambda b, i: (0,) * a.ndim)

    small = (cw, cb, wa, ba, wx, bx, lam, scw)
    yspec = pl.BlockSpec((T_TOK, GROUP_W), lambda b, i: (b * nt + i, 0))
    return pl.pallas_call(
        _mix_ad_body,
        grid=(BATCH, nt),
        in_specs=[zspec(COL_A_X), zspec(COL_A_G), zspec(COL_D_IN), zspec(COL_D_B), zspec(COL_D_C),
                  zspec(COL_D_G)] + [full(a) for a in small],
        out_specs=[yspec, yspec],
        out_shape=[jax.ShapeDtypeStruct((BATCH * SEQ, GROUP_W), BF16)] * 2,
        scratch_shapes=[pltpu.VMEM((T_TOK + 8, GROUP_W), F32), pltpu.VMEM((T_TOK + 8, GROUP_W), F32),
                        pltpu.VMEM((T_TOK, GROUP_W), F32), pltpu.VMEM((T_TOK, GROUP_W), F32),
                        pltpu.VMEM((8, GROUP_W), F32)],
        compiler_params=_params(2),
        name="mix_ad",
    )(z, z, z, z, z, z, *small)


def _rms_head(x, gain):
    ms = jnp.mean(x * x, axis=-1, keepdims=True)
    return (x * lax.rsqrt(ms + NORM_EPS)) * gain


def _rope(y, cos, nsin, psin, half):
    return y * cos + pltpu.roll(y, LANES - half, axis=1) * nsin + pltpu.roll(y, half, axis=1) * psin


def _store_vt(dst, g, v):
    for cc in range(T_TOK // KV_TILE):
        dst[0, g, cc] = v[cc * KV_TILE:(cc + 1) * KV_TILE, :].T.astype(BF16)


def _prep_body(bq, bkv_s, bkv_w, cq, ckv, ciq, cik, c128, n128, p128, c64, n64, p64,
               qgn, kgn, qgd, kgd,
               qn_o, qr_o, ks_o, vst_o, kw_o, vwt_o, qd_o, kd_o, vdt_o, iq_o, ik_o):
    r128 = (c128[...], n128[...], p128[...], HEAD_DIM // 8)
    r64 = (c64[...], n64[...], p64[...], IDX_DIM // 8)
    for h in range(N_HEADS):
        c = slice(h * LANES, (h + 1) * LANES)
        y = _rms_head(bq[:, c], qgn[...])
        qn_o[:, c] = (y * Q_PRESCALE).astype(BF16)
        qr_o[:, c] = (_rope(y, *r128) * Q_PRESCALE).astype(BF16)
        qd_o[:, c] = (_rope(_rms_head(cq[:, c], qgd[...]), *r128) * Q_PRESCALE).astype(BF16)
    for g in range(N_KV):
        ck = slice(g * LANES, (g + 1) * LANES)
        cv = slice((N_KV + g) * LANES, (N_KV + g + 1) * LANES)
        ks_o[0, g] = _rope(_rms_head(bkv_s[:, ck], kgn[...]), *r128).astype(BF16)
        _store_vt(vst_o, g, bkv_s[:, cv])
        kw_o[0, g] = _rope(_rms_head(bkv_w[:, ck], kgn[...]), *r128).astype(BF16)
        _store_vt(vwt_o, g, bkv_w[:, cv])
        kd_o[0, g] = _rope(_rms_head(ckv[:, ck], kgd[...]), *r128).astype(BF16)
        _store_vt(vdt_o, g, ckv[:, cv])

    low = lax.broadcasted_iota(I32, (T_TOK, LANES), 1) < IDX_DIM
    ik = _rope(cik[...], *r64)
    hi = ik.astype(BF16).astype(F32)
    ik_o[0, :, 0:LANES] = (hi + pltpu.roll(ik - hi, IDX_DIM, axis=1)).astype(BF16)
    ik_o[0, :, LANES:2 * LANES] = hi.astype(BF16)
    for v in range(IDX_HEADS // 2):
        x = _rope(ciq[:, v * LANES:(v + 1) * LANES], *r64)
        for half in range(2):
            h = 2 * v + half
            xs = x if half == 0 else pltpu.roll(x, IDX_DIM, axis=1)
            xh = jnp.where(low, xs, 0.0)
            hi = xh.astype(BF16).astype(F32)
            iq_o[:, 2 * h * LANES:(2 * h + 1) * LANES] = (hi + pltpu.roll(hi, IDX_DIM, axis=1)).astype(BF16)
            iq_o[:, (2 * h + 1) * LANES:(2 * h + 2) * LANES] = (xh - hi).astype(BF16)


def _prep(z, tabs128, tabs64, qgn, kgn, qgd, kgd):
    nt = SEQ // T_TOK

    def zspec(col, width):
        return pl.BlockSpec((T_TOK, width), lambda b, i: (b * nt + i, col // width))

    tab = pl.BlockSpec((T_TOK, LANES), lambda b, i: (i, 0))
    gain = pl.BlockSpec((1, LANES), lambda b, i: (0, 0))
    qspec = pl.BlockSpec((T_TOK, GROUP_W), lambda b, i: (b * nt + i, 0))
    kspec = pl.BlockSpec((1, N_KV, T_TOK, LANES), lambda b, i: (b, 0, i, 0))
    vtspec = pl.BlockSpec((1, N_KV, T_TOK // KV_TILE, LANES, KV_TILE), lambda b, i: (b, 0, i, 0, 0))
    qshape = jax.ShapeDtypeStruct((BATCH * SEQ, GROUP_W), BF16)
    kshape = jax.ShapeDtypeStruct((BATCH, N_KV, SEQ, LANES), BF16)
    vtshape = jax.ShapeDtypeStruct((BATCH, N_KV, SEQ // KV_TILE, LANES, KV_TILE), BF16)
    return pl.pallas_call(
        _prep_body,
        grid=(BATCH, nt),
        in_specs=[zspec(COL_B_Q, GROUP_W), zspec(COL_B_KV + 512, 512), zspec(COL_B_KV + 1024, 512),
                  zspec(COL_C_Q, GROUP_W), zspec(COL_C_KV, 512), zspec(COL_C_IQ, 512), zspec(COL_C_IK, LANES)]
                 + [tab] * 6 + [gain] * 4,
        out_specs=[qspec, qspec, kspec, vtspec, kspec, vtspec, qspec, kspec, vtspec,
                   pl.BlockSpec((T_TOK, 2 * LANES * IDX_HEADS), lambda b, i: (b * nt + i, 0)),
                   pl.BlockSpec((1, T_TOK, 2 * LANES), lambda b, i: (b, i, 0))],
        out_shape=[qshape, qshape, kshape, vtshape, kshape, vtshape, qshape, kshape, vtshape,
                   jax.ShapeDtypeStruct((BATCH * SEQ, 2 * LANES * IDX_HEADS), BF16),
                   jax.ShapeDtypeStruct((BATCH, SEQ, 2 * LANES), BF16)],
        compiler_params=_params(2),
        name="attn_prep",
    )(z, z, z, z, z, z, z, *tabs128, *tabs64, qgn, kgn, qgd, kgd)


def _cmp_body(zc, pe, w1, w2, kg, o, ot, kbuf):
    kbuf[0:SEQ, :] = zc[...]
    kbuf[SEQ:SEQ + LANES, :] = jnp.zeros((LANES, LANES), F32)
    acc = jnp.zeros((N_CMP_PAD, LANES), F32)
    for l in range(CMP_LEN):
        rows = kbuf[pl.ds(l, N_CMP_PAD, stride=CMP_STRIDE), :] + pe[0, l:l + 1, :]
        acc = acc + jnp.dot(rows.astype(BF16), w1[0, l], preferred_element_type=F32)
    out = jnp.dot(_silu(acc).astype(BF16), w2[0], preferred_element_type=F32)
    res = jnp.where(pl.program_id(1) == 0, _rms_head(out, kg[...]), out)
    o[0, 0, 0] = res.astype(BF16)
    ot[0, 0, 0] = res.T.astype(BF16)


def _compress(z, pe, w1, w2, kg):
    base = COL_B_KV // LANES
    return pl.pallas_call(
        _cmp_body,
        grid=(BATCH, 2, N_KV),
        in_specs=[pl.BlockSpec((SEQ, LANES), lambda b, kv, g: (b, base + kv * N_KV + g)),
                  pl.BlockSpec((1, CMP_LEN, LANES), lambda b, kv, g: (kv, 0, 0)),
                  pl.BlockSpec((1, CMP_LEN, LANES, LANES), lambda b, kv, g: (kv, 0, 0, 0)),
                  pl.BlockSpec((1, LANES, LANES), lambda b, kv, g: (kv, 0, 0)),
                  pl.BlockSpec((1, LANES), lambda b, kv, g: (0, 0))],
        out_specs=[pl.BlockSpec((1, 1, 1, N_CMP_PAD, LANES), lambda b, kv, g: (b, kv, g, 0, 0)),
                   pl.BlockSpec((1, 1, 1, LANES, N_CMP_PAD), lambda b, kv, g: (b, kv, g, 0, 0))],
        out_shape=[jax.ShapeDtypeStruct((BATCH, 2, N_KV, N_CMP_PAD, LANES), BF16),
                   jax.ShapeDtypeStruct((BATCH, 2, N_KV, LANES, N_CMP_PAD), BF16)],
        scratch_shapes=[pltpu.VMEM((SEQ + LANES, LANES), F32)],
        compiler_params=_params(3),
        name="nsa_compress",
    )(z, pe, w1, w2, kg)


def _key_norm_bound(k_ref):
    def body(c, mx):
        x = k_ref[pl.ds(pl.multiple_of(c * 512, 512), 512), :].astype(F32)
        return jnp.maximum(mx, jnp.max(jnp.sum(x * x, axis=1, keepdims=True), axis=0, keepdims=True))
    return jnp.broadcast_to(lax.fori_loop(0, SEQ // 512, body, jnp.zeros((1, 1), F32)), (1, TQ))


def _flash_heads(q_ref, slot0, groups, jlo, jhi, m_s, l_s, acc_s):
    n_heads = sum(len(g[2]) for g in groups)
    lo, hi = slot0, slot0 + n_heads
    l_s[lo:hi, :] = jnp.zeros((n_heads, TQ), F32)
    for slot in range(lo, hi):
        acc_s[slot] = jnp.zeros((HEAD_DIM, TQ), F32)

    ones = jnp.ones((8, HEAD_DIM), BF16)
    bounds = []
    for _, _, cs, _, kmax2 in groups:
        for c in cs:
            q = q_ref[:, c].astype(F32)
            qn2 = lax.dot_general(ones, (q * q).astype(BF16), NT_DIMS, preferred_element_type=F32)[0:1, :]
            bounds.append(jnp.sqrt(qn2 * kmax2) * BOUND_SLACK)
    m_bound = jnp.concatenate(bounds, axis=0)

    def scores(j):
        r0 = pl.multiple_of(j * KV_TILE, KV_TILE)
        ss, vts, biases = [], [], {}
        for k_ref, vt_ref, cs, bias_fn, _ in groups:
            k = k_ref[pl.ds(r0, KV_TILE), :]
            vt = vt_ref[j]
            if bias_fn not in biases:
                biases[bias_fn] = bias_fn(j)
            b = biases[bias_fn]
            for c in cs:
                ss.append(lax.dot_general(k, q_ref[:, c], NT_DIMS, preferred_element_type=F32) + b)
                vts.append(vt)
        return ss, vts

    def max_pass(j, m):
        ss, _ = scores(j)
        return jnp.maximum(m, jnp.concatenate([_fold8(s, jnp.maximum) for s in ss], axis=0))

    def exact_max():
        m8 = lax.fori_loop(jlo, jhi, max_pass, jnp.full((8 * n_heads, TQ), NEG, F32))
        m_fin = jnp.concatenate([jnp.max(m8[8 * n:8 * n + 8], axis=0, keepdims=True)
                                 for n in range(n_heads)], axis=0)
        return jnp.maximum(m_fin, M_FLOOR)

    m_s[lo:hi, :] = lax.cond(jnp.max(m_bound) > BOUND_LIMIT, exact_max, lambda: m_bound)

    def body(j, carry):
        ss, vts = scores(j)
        m_safe = m_s[lo:hi, :]
        ps = [jnp.exp2(s - m_safe[n:n + 1, :]) for n, s in enumerate(ss)]
        l_s[lo:hi, :] = l_s[lo:hi, :] + jnp.concatenate([jnp.sum(p, axis=0, keepdims=True) for p in ps], axis=0)
        pv = [jnp.dot(vt, p.astype(BF16), preferred_element_type=F32) for vt, p in zip(vts, ps)]
        for n in range(n_heads):
            acc_s[lo + n] = acc_s[lo + n] + pv[n]
        return carry

    lax.fori_loop(jlo, jhi, body, 0)


def _flash_out(slot, l_s, acc_s):
    return acc_s[slot] * (1.0 / l_s[slot:slot + 1, :])


def _nsa_body(qn, qr, kc, vct, ks, vst, kw, vwt, gate, bg, ovt, yb, imp_s, bias_s, m_s, l_s, acc_s, kn_s):
    i = pl.program_id(1)

    @pl.when(i == 0)
    def _():
        for g in range(N_KV):
            kn_s[g:g + 1, :] = _key_norm_bound(ks.at[0, g])
            kn_s[N_KV + g:N_KV + g + 1, :] = _key_norm_bound(kw.at[0, g])

    tq = i * TQ + lax.broadcasted_iota(I32, (1, TQ), 1)
    n_kv = i + 1
    sg = jax.nn.sigmoid(gate[...].T[0:3 * N_HEADS, :])
    cm = (lax.broadcasted_iota(I32, (N_CMP_PAD, TQ), 0) * CMP_STRIDE + (CMP_LEN - 1)) <= tq
    kk = lax.broadcasted_iota(I32, (N_SLC, TQ), 0)
    sub8 = lax.broadcasted_iota(I32, (8, TQ), 0)
    blk_t = lax.shift_right_logical(tq, 6)
    valid = kk <= blk_t
    force = (kk == 0) | (kk == blk_t) | (kk == blk_t - 1)
    rowkv = lax.broadcasted_iota(I32, (KV_TILE, TQ), 0)
    cmp_slot, slc_slot, win_slot = 0, N_HEADS, 2 * N_HEADS
    head_cols = [[slice(h * LANES, (h + 1) * LANES) for h in range(HPG * g, HPG * (g + 1))]
                 for g in range(N_KV)]

    def win_bias(j):
        d = tq - (j * KV_TILE + rowkv)
        return jnp.where((d >= 0) & (d < WINDOW), 0.0, NEG)

    def slc_bias(g):
        return lambda j: bias_s[g, pl.ds(pl.multiple_of(j * KV_TILE, KV_TILE), KV_TILE), :]

    for g in range(N_KV):
        kcg = kc[0, 0, g]
        vcg = vct[0, 0, g]
        cols = head_cols[g]
        ss = [jnp.where(cm, lax.dot_general(kcg, qn[:, c], NT_DIMS, preferred_element_type=F32), NEG)
              for c in cols]
        es = [jnp.exp2(s - jnp.max(s, axis=0, keepdims=True)) for s in ss]
        ps = [jnp.where(cm, e * (1.0 / jnp.sum(e, axis=0, keepdims=True)), 0.0) for e in es]
        for hh, p in enumerate(ps):
            acc_s[cmp_slot + HPG * g + hh] = jnp.dot(vcg, p.astype(BF16), preferred_element_type=F32)
        psum = (ps[0] + ps[1]) + (ps[2] + ps[3])

        p_hi = psum.astype(BF16)
        p_lo = (psum - p_hi.astype(F32)).astype(BF16)
        imp = (jnp.dot(ovt[...], p_hi, preferred_element_type=F32)
               + jnp.dot(ovt[...], p_lo, preferred_element_type=F32))
        imp = jnp.where(force, FORCE, jnp.where(valid, imp, -FORCE))
        imp_s[...] = imp
        bands = [imp[8 * r:8 * r + 8] for r in range(N_SLC // 8)]
        ranks = [jnp.zeros((8, TQ), F32) for _ in bands]
        for j in range(N_SLC):
            vj = imp_s[j:j + 1, :]
            for r, x in enumerate(bands):
                if r < j // 8:
                    hit = vj > x
                elif r > j // 8:
                    hit = vj >= x
                else:
                    hit = (vj > x) | ((vj == x) & (sub8 > j % 8))
                ranks[r] = ranks[r] + jnp.where(hit, 1.0, 0.0)
        imp_s[...] = jnp.where(jnp.concatenate(ranks, axis=0) < SLC_TOPN, 0.0, NEG)

        def fill(j, carry):
            r0 = pl.multiple_of(j * KV_TILE, KV_TILE)
            nb = KV_TILE // SLC_BLK
            tile = jnp.concatenate([jnp.broadcast_to(imp_s[pl.ds(nb * j + u, 1), :], (SLC_BLK, TQ))
                                    for u in range(nb)], axis=0)
            bias_s[g, pl.ds(r0, KV_TILE), :] = jnp.where(r0 + rowkv <= tq, tile, NEG)
            return carry
        lax.fori_loop(0, n_kv, fill, 0)

    _flash_heads(qr, slc_slot, [(ks.at[0, g], vst.at[0, g], head_cols[g], slc_bias(g), kn_s[g:g + 1, :])
                                for g in range(N_KV)], 0, n_kv, m_s, l_s, acc_s)
    _flash_heads(qr, win_slot, [(kw.at[0, g], vwt.at[0, g], head_cols[g], win_bias,
                                 kn_s[N_KV + g:N_KV + g + 1, :]) for g in range(N_KV)],
                 jnp.maximum(n_kv - 1 - WINDOW // KV_TILE, 0), n_kv, m_s, l_s, acc_s)
    for h in range(N_HEADS):
        c = slice(h * LANES, (h + 1) * LANES)
        tot = (sg[h:h + 1] * acc_s[cmp_slot + h]
               + sg[N_HEADS + h:N_HEADS + h + 1] * _flash_out(slc_slot + h, l_s, acc_s)
               + sg[2 * N_HEADS + h:2 * N_HEADS + h + 1] * _flash_out(win_slot + h, l_s, acc_s))
        yb[:, c] = (tot.T * _silu(bg[:, c])).astype(BF16)


def _nsa(z, qn, qr, cmp_, cmpt, ks, vst, kw, vwt, ovt):
    nq = SEQ // TQ
    qspec = pl.BlockSpec((TQ, GROUP_W), lambda b, i: (b * nq + i, 0))
    kspec = pl.BlockSpec((1, N_KV, SEQ, LANES), lambda b, i: (b, 0, 0, 0))
    vtspec = pl.BlockSpec((1, N_KV, SEQ // KV_TILE, LANES, KV_TILE), lambda b, i: (b, 0, 0, 0, 0))
    return pl.pallas_call(
        _nsa_body,
        grid=(BATCH, nq),
        in_specs=[qspec, qspec,
                  pl.BlockSpec((1, 1, N_KV, N_CMP_PAD, LANES), lambda b, i: (b, 0, 0, 0, 0)),
                  pl.BlockSpec((1, 1, N_KV, LANES, N_CMP_PAD), lambda b, i: (b, 1, 0, 0, 0)),
                  kspec, vtspec, kspec, vtspec,
                  pl.BlockSpec((TQ, LANES), lambda b, i: (b * nq + i, COL_B_GATE // LANES)),
                  pl.BlockSpec((TQ, GROUP_W), lambda b, i: (b * nq + i, COL_B_G // GROUP_W)),
                  pl.BlockSpec((N_SLC, N_CMP_PAD), lambda b, i: (0, 0))],
        out_specs=qspec,
        out_shape=jax.ShapeDtypeStruct((BATCH * SEQ, GROUP_W), BF16),
        scratch_shapes=[pltpu.VMEM((N_SLC, TQ), F32), pltpu.VMEM((N_KV, SEQ, TQ), F32),
                        pltpu.VMEM((3 * N_HEADS, TQ), F32), pltpu.VMEM((3 * N_HEADS, TQ), F32),
                        pltpu.VMEM((3 * N_HEADS, HEAD_DIM, TQ), F32), pltpu.VMEM((8, TQ), F32)],
        compiler_params=_params(2),
        name="nsa_attn",
    )(qn, qr, cmp_, cmpt, ks, vst, kw, vwt, z, z, ovt)


def _dsa_body(qd, kd, vdt, iq, ik, iw, cg, yc, sc_s, m_s, l_s, acc_s, kn_s):
    i = pl.program_id(1)

    @pl.when(i == 0)
    def _():
        for g in range(N_KV):
            kn_s[g:g + 1, :] = _key_norm_bound(kd.at[0, g])

    tq = i * TQ + lax.broadcasted_iota(I32, (1, TQ), 1)
    n_kv = i + 1
    iwt = iw[...].T[0:IDX_HEADS, :] * IW_SCALE
    rowkv = lax.broadcasted_iota(I32, (KV_TILE, TQ), 0)
    kf = float(DSA_TOPK)

    def score_tile(j):
        r0 = pl.multiple_of(j * KV_TILE, KV_TILE)
        ikc = ik[0, pl.ds(r0, KV_TILE), :]
        sc = jnp.zeros((KV_TILE, TQ), F32)
        for h in range(IDX_HEADS):
            r = lax.dot_general(ikc, iq[:, 2 * h * LANES:(2 * h + 2) * LANES], NT_DIMS,
                                preferred_element_type=F32)
            sc = sc + jnp.maximum(r, 0.0) * iwt[h:h + 1, :]
        return r0, sc

    def fill(j, carry):
        mn, mx = carry
        r0, sc = score_tile(j)
        sc_s[pl.ds(r0, KV_TILE), :] = sc
        return (jnp.minimum(mn, _fold8(sc, jnp.minimum)), jnp.maximum(mx, _fold8(sc, jnp.maximum)))

    def fill2(jj, carry):
        return fill(2 * jj + 1, fill(2 * jj, carry))

    half = lax.shift_right_logical(i, 1)
    mm = lax.fori_loop(0, half, fill2, (jnp.full((8, TQ), jnp.inf, F32), jnp.full((8, TQ), -jnp.inf, F32)))
    mn8, mx8 = lax.fori_loop(2 * half, i, fill, mm)
    r0, sc = score_tile(i)
    ok = r0 + rowkv <= tq
    sc_s[pl.ds(r0, KV_TILE), :] = jnp.where(ok, sc, -jnp.inf)
    lo0 = jnp.min(jnp.minimum(mn8, _fold8(jnp.where(ok, sc, jnp.inf), jnp.minimum)), axis=0, keepdims=True)
    hi0 = jnp.max(jnp.maximum(mx8, _fold8(jnp.where(ok, sc, -jnp.inf), jnp.maximum)), axis=0, keepdims=True)

    def sweep(fn, n_out):
        def body(c, accs):
            r0 = pl.multiple_of(c * KV_TILE, KV_TILE)
            inds = fn(sc_s[pl.ds(r0, KV_TILE), :], r0 + rowkv)
            return tuple(a + _fold8(jnp.where(ind, 1.0, 0.0)) for a, ind in zip(accs, inds))
        accs = lax.fori_loop(0, n_kv, body, tuple(jnp.zeros((8, TQ), F32) for _ in range(n_out)))
        return tuple(jnp.sum(a, axis=0, keepdims=True) for a in accs)

    def data_span(lo, hi):
        def body(c, carry):
            mn, mx = carry
            x = sc_s[pl.ds(pl.multiple_of(c * KV_TILE, KV_TILE), KV_TILE), :]
            mn = jnp.minimum(mn, _fold8(jnp.where(x >= lo, x, jnp.inf), jnp.minimum))
            mx = jnp.maximum(mx, _fold8(jnp.where(x <= hi, x, -jnp.inf), jnp.maximum))
            return mn, mx
        mn, mx = lax.fori_loop(0, n_kv, body, (jnp.full((8, TQ), jnp.inf, F32), jnp.full((8, TQ), -jnp.inf, F32)))
        return jnp.min(mn, axis=0, keepdims=True), jnp.max(mx, axis=0, keepdims=True)

    nvalid = (tq + 1).astype(F32)
    act0 = jnp.where(nvalid > kf, 1.0, 0.0)

    def bisect(_, c):
        lo, hi, clo, act = c
        on = act > 0.0
        mid = lo + (hi - lo) * 0.5
        exh = (mid <= lo) | (mid >= hi)
        p = jnp.where(exh, hi, mid)
        (cnt,) = sweep(lambda x, srow: (x >= p,), 1)
        up = on & (cnt >= kf)
        dn = on & (cnt < kf)
        clo = jnp.where(up, cnt, clo)
        act = jnp.where(on & (clo != kf) & jnp.logical_not(exh), 1.0, 0.0)
        return jnp.where(up, p, lo), jnp.where(dn, p, hi), clo, act

    def snap(c):
        lo, hi, clo, act = c
        on = act > 0.0
        dmin, dmax = data_span(lo, hi)
        act = jnp.where(on & (dmin < dmax), 1.0, 0.0)
        return jnp.where(on, dmin, lo), jnp.where(on, dmax, hi), clo, act

    def cond(c):
        return c[0] > 0.0

    def round_(c):
        state = lax.fori_loop(0, SNAP_EVERY - 1, bisect, snap(c[1:]))
        return (jnp.sum(state[3]),) + state

    state = lax.fori_loop(0, SNAP_FIRST, bisect, (lo0, hi0, nvalid, act0))
    _, thr, _, _, _ = lax.while_loop(cond, round_, (jnp.sum(state[3]),) + state)

    cnt_g, cnt_e = sweep(lambda x, srow: (x > thr, x == thr), 2)
    need = kf - cnt_g
    excess = (cnt_g + cnt_e) > kf

    def tie_break():
        def it(_, c):
            jl, jh = c
            jm = lax.shift_right_arithmetic(jl + jh, 1)
            (cnt,) = sweep(lambda x, srow: ((x == thr) & (srow <= jm),), 1)
            ge = cnt >= need
            return jnp.where(ge, jl, jm), jnp.where(ge, jm, jh)
        _, jh = lax.fori_loop(0, 12, it, (jnp.full((1, TQ), -1, I32), jnp.full((1, TQ), SEQ - 1, I32)))
        return jnp.where(excess, jh, SEQ)

    last = lax.cond(jnp.sum(jnp.where(excess, 1.0, 0.0)) > 0.0, tie_break,
                    lambda: jnp.full((1, TQ), SEQ, I32))

    def to_bias(c, carry):
        r0 = pl.multiple_of(c * KV_TILE, KV_TILE)
        x = sc_s[pl.ds(r0, KV_TILE), :]
        sel = (x > thr) | ((x == thr) & (r0 + rowkv <= last))
        sc_s[pl.ds(r0, KV_TILE), :] = jnp.where(sel, 0.0, NEG)
        return carry
    lax.fori_loop(0, n_kv, to_bias, 0)

    def bias(j):
        return sc_s[pl.ds(pl.multiple_of(j * KV_TILE, KV_TILE), KV_TILE), :]

    groups = [(kd.at[0, g], vdt.at[0, g], [slice(h * LANES, (h + 1) * LANES)
                                           for h in range(HPG * g, HPG * (g + 1))], bias, kn_s[g:g + 1, :])
              for g in range(N_KV)]
    _flash_heads(qd, 0, groups, 0, n_kv, m_s, l_s, acc_s)
    for h in range(N_HEADS):
        c = slice(h * LANES, (h + 1) * LANES)
        yc[:, c] = (_flash_out(h, l_s, acc_s).T * _silu(cg[:, c])).astype(BF16)


def _dsa(z, qd, kd, vdt, iq, ik):
    nq = SEQ // TQ
    qspec = pl.BlockSpec((TQ, GROUP_W), lambda b, i: (b * nq + i, 0))
    return pl.pallas_call(
        _dsa_body,
        grid=(BATCH, nq),
        in_specs=[qspec,
                  pl.BlockSpec((1, N_KV, SEQ, LANES), lambda b, i: (b, 0, 0, 0)),
                  pl.BlockSpec((1, N_KV, SEQ // KV_TILE, LANES, KV_TILE), lambda b, i: (b, 0, 0, 0, 0)),
                  pl.BlockSpec((TQ, 2 * LANES * IDX_HEADS), lambda b, i: (b * nq + i, 0)),
                  pl.BlockSpec((1, SEQ, 2 * LANES), lambda b, i: (b, 0, 0)),
                  pl.BlockSpec((TQ, LANES), lambda b, i: (b * nq + i, COL_C_IW // LANES)),
                  pl.BlockSpec((TQ, GROUP_W), lambda b, i: (b * nq + i, COL_C_G // GROUP_W))],
        out_specs=qspec,
        out_shape=jax.ShapeDtypeStruct((BATCH * SEQ, GROUP_W), BF16),
        scratch_shapes=[pltpu.VMEM((SEQ, TQ), F32), pltpu.VMEM((N_HEADS, TQ), F32),
                        pltpu.VMEM((N_HEADS, TQ), F32), pltpu.VMEM((N_HEADS, HEAD_DIM, TQ), F32),
                        pltpu.VMEM((8, TQ), F32)],
        compiler_params=_params(2),
        name="dsa_attn",
    )(qd, kd, vdt, iq, ik, z, z)


def _segment_moves():
    src = {}
    o = 0
    names = ("a_x", "a_g", "b_q", "b_g", "b_kv", "b_gate", "c_q", "c_g", "c_kv", "c_iq", "c_ik", "c_iw",
             "d_in", "d_b", "d_c", "d_g")
    for n, s in zip(names, SPLIT_SIZES):
        src[n] = (o, s)
        o += s
    dst = dict(a_x=COL_A_X, a_g=COL_A_G, b_q=COL_B_Q, b_g=COL_B_G, c_q=COL_C_Q, c_g=COL_C_G,
               d_in=COL_D_IN, d_b=COL_D_B, d_c=COL_D_C, d_g=COL_D_G, b_kv=COL_B_KV, c_kv=COL_C_KV,
               c_iq=COL_C_IQ, b_gate=COL_B_GATE, c_ik=COL_C_IK, c_iw=COL_C_IW)
    return [(src[n][0], dst[n], src[n][1]) for n in names]


def _wprep_body(w_ref, o_ref):
    o_ref[COL_B_GATE:PROJ_W, :] = jnp.zeros((PROJ_W - COL_B_GATE, TR_W), BF16)
    for s, d, width in _segment_moves():
        for c in range(0, width, GROUP_W):
            n = min(GROUP_W, width - c)
            o_ref[d + c:d + c + n, :] = w_ref[s + c:s + c + n, :].astype(BF16)


def _proj_weight(wt, layer):
    _, n, k = wt.shape
    return pl.pallas_call(
        _wprep_body,
        grid=(k // TR_W,),
        in_specs=[pl.BlockSpec((None, n, TR_W), lambda i: (layer, 0, i))],
        out_specs=pl.BlockSpec((PROJ_W, TR_W), lambda i: (0, i)),
        out_shape=jax.ShapeDtypeStruct((PROJ_W, k), BF16),
        compiler_params=_params(1),
        name="w_in_layout",
    )(wt)


def _cast_body(w_ref, o_ref):
    o_ref[...] = w_ref[...].astype(BF16)


def _to_bf16(w, layer):
    _, k, n = w.shape
    return pl.pallas_call(
        _cast_body,
        grid=(k // 512,),
        in_specs=[pl.BlockSpec((None, 512, n), lambda i: (layer, i, 0))],
        out_specs=pl.BlockSpec((512, n), lambda i: (i, 0)),
        out_shape=jax.ShapeDtypeStruct((k, n), BF16),
        compiler_params=_params(1),
        name="w_cast",
    )(w)


def _rope_tables(head_dim):
    r = head_dim // 4
    half = r // 2
    pos = jnp.arange(SEQ, dtype=jnp.int32)
    inv = ROPE_THETA ** (-jnp.arange(half, dtype=F32) * 2.0 / r)
    ang = pos.astype(F32)[:, None] * inv[None, :]
    cos, sin = jnp.cos(ang), jnp.sin(ang)
    one = jnp.ones((SEQ, head_dim - r), F32)
    zero = jnp.zeros((SEQ, head_dim - r), F32)
    zh = jnp.zeros((SEQ, half), F32)
    reps = LANES // head_dim
    return tuple(jnp.tile(t, (1, reps)) for t in (
        jnp.concatenate([cos, cos, one], axis=1),
        jnp.concatenate([-sin, zh, zero], axis=1),
        jnp.concatenate([zh, sin, zero], axis=1)))


def _overlap_t():
    cs = jnp.arange(N_CMP_PAD) * CMP_STRIDE
    ss = jnp.arange(N_SLC) * SLC_BLK
    ov = jnp.clip(jnp.minimum(cs[None, :] + CMP_LEN, ss[:, None] + SLC_BLK)
                  - jnp.maximum(cs[None, :], ss[:, None]), 0, None).astype(F32) / CMP_LEN
    return jnp.where(jnp.arange(N_CMP_PAD)[None, :] < N_CMP_PAD - 1, ov, 0.0).astype(BF16)


def kernel(x, norm_g, w_in, w_out, lru_conv_w, lru_conv_b, lru_wa, lru_ba, lru_wx, lru_bx, lru_lambda,
           nsa_q_gain, nsa_k_gain, cmp_pe_k, cmp_w1_k, cmp_w2_k, cmp_pe_v, cmp_w1_v, cmp_w2_v,
           dsa_q_gain, dsa_k_gain, sc_conv_w):
    b, s, d = x.shape
    assert (b, s, d) == (BATCH, SEQ, D_MODEL)
    tabs128 = _rope_tables(HEAD_DIM)
    tabs64 = _rope_tables(IDX_DIM)
    ovt = _overlap_t()
    x2 = x.reshape(b * s, d)
    w_in_t = jnp.swapaxes(w_in, 1, 2)

    def row(v):
        return v.reshape(1, -1)

    for l in range(DEPTH):
        z = _inproj(x2, row(norm_g[l]), _proj_weight(w_in_t, l))
        ya, yd = _mix_ad(z, lru_conv_w[l], row(lru_conv_b[l]), lru_wa[l].astype(BF16), row(lru_ba[l]),
                         lru_wx[l].astype(BF16), row(lru_bx[l]), row(lru_lambda[l]), sc_conv_w[l])
        qn, qr, ks, vst, kw, vwt, qd, kd, vdt, iq, ik = _prep(
            z, tabs128, tabs64, row(nsa_q_gain[l]), row(nsa_k_gain[l]), row(dsa_q_gain[l]), row(dsa_k_gain[l]))
        cmp_, cmpt = _compress(z, jnp.stack([cmp_pe_k[l], cmp_pe_v[l]]),
                               jnp.stack([cmp_w1_k[l], cmp_w1_v[l]]).astype(BF16),
                               jnp.stack([cmp_w2_k[l], cmp_w2_v[l]]).astype(BF16), row(nsa_k_gain[l]))
        yb = _nsa(z, qn, qr, cmp_, cmpt, ks, vst, kw, vwt, ovt)
        yc = _dsa(z, qd, kd, vdt, iq, ik)
        x2 = _outproj((ya, yb, yc, yd), _to_bf16(w_out, l), x2)
    return x2.reshape(b, s, d)
```

```python
import functools

import jax
import jax.numpy as jnp
from jax import lax
from jax.experimental import pallas as pl
from jax.experimental.pallas import tpu as pltpu

F32 = jnp.float32
BF16 = jnp.bfloat16
I32 = jnp.int32

D_MODEL = 4096
BATCH = 2
SEQ = 4096
DEPTH = 2
GROUP_W = D_MODEL // 4
HEAD_DIM = 128
N_HEADS = GROUP_W // HEAD_DIM
N_KV = N_HEADS // 4
HPG = N_HEADS // N_KV
ROPE_THETA = 500000.0
NORM_EPS = 1e-6
NEG = -1e30
FORCE = 1e6
LRU_BLOCKS = N_HEADS
LRU_C = 8.0
CONV_A = 4
CONV_D = 3
CMP_LEN = 32
CMP_STRIDE = 16
SLC_BLK = 64
SLC_TOPN = 16
WINDOW = 512
IDX_HEADS = 8
IDX_DIM = 64
DSA_TOPK = 256
SCALE = HEAD_DIM ** -0.5
Q_PRESCALE = SCALE * 1.4426950408889634
IW_SCALE = IDX_HEADS ** -0.5 * IDX_DIM ** -0.5

SPLIT_SIZES = (
    GROUP_W, GROUP_W,
    N_HEADS * HEAD_DIM, GROUP_W, 6 * N_KV * HEAD_DIM, 3 * N_HEADS,
    N_HEADS * HEAD_DIM, GROUP_W, 2 * N_KV * HEAD_DIM,
    IDX_HEADS * IDX_DIM, IDX_DIM, IDX_HEADS,
    GROUP_W, GROUP_W, GROUP_W, GROUP_W,
)

LANES = 128
COL_A_X, COL_A_G, COL_B_Q, COL_B_G, COL_C_Q, COL_C_G = 0, 1024, 2048, 3072, 4096, 5120
COL_D_IN, COL_D_B, COL_D_C, COL_D_G = 6144, 7168, 8192, 9216
COL_B_KV, COL_C_KV, COL_C_IQ = 10240, 11776, 12288
COL_B_GATE, COL_C_IK, COL_C_IW = 12800, 12928, 13056
PROJ_W = 13312

N_CMP_PAD = SEQ // CMP_STRIDE
N_SLC = SEQ // SLC_BLK
KV_TILE = 256
M_FLOOR = -1e20
BOUND_SLACK = 1.01
BOUND_LIMIT = 60.0
SNAP_FIRST, SNAP_EVERY = 8, 4

VMEM_LIMIT_BYTES = 56 * 1024 * 1024

TM_IN, TN_IN = 1024, 512
TM_OUT, TN_OUT = 1024, 512
T_TOK = 256
TR_W = 128
TQ = 256

NT_DIMS = (((1,), (1,)), ((), ()))


def _params(n_axes):
    return pltpu.CompilerParams(dimension_semantics=("arbitrary",) * n_axes,
                                vmem_limit_bytes=VMEM_LIMIT_BYTES)


def _silu(x):
    return x * jax.nn.sigmoid(x)


def _fold8(x, op=jnp.add):
    parts = [x[r:r + 8] for r in range(0, x.shape[0], 8)]
    while len(parts) > 1:
        nxt = [op(parts[a], parts[a + 1]) for a in range(0, len(parts) - 1, 2)]
        if len(parts) % 2:
            nxt.append(parts[-1])
        parts = nxt
    return parts[0]


def _inproj_body(x_ref, g_ref, w_ref, o_ref, h_ref):
    @pl.when(pl.program_id(1) == 0)
    def _():
        def rows(r, carry):
            r0 = pl.multiple_of(r * 64, 64)
            x = x_ref[pl.ds(r0, 64), :]
            ms = jnp.mean(x * x, axis=-1, keepdims=True)
            h_ref[pl.ds(r0, 64), :] = ((x * lax.rsqrt(ms + NORM_EPS)) * g_ref[...]).astype(BF16)
            return carry
        lax.fori_loop(0, TM_IN // 64, rows, 0)

    o_ref[...] = lax.dot_general(h_ref[...], w_ref[...], NT_DIMS, preferred_element_type=F32)


def _inproj(x2, g, wt):
    m = x2.shape[0]
    return pl.pallas_call(
        _inproj_body,
        grid=(m // TM_IN, PROJ_W // TN_IN),
        in_specs=[pl.BlockSpec((TM_IN, D_MODEL), lambda i, j: (i, 0)),
                  pl.BlockSpec((1, D_MODEL), lambda i, j: (0, 0)),
                  pl.BlockSpec((TN_IN, D_MODEL), lambda i, j: (j, 0))],
        out_specs=pl.BlockSpec((TM_IN, TN_IN), lambda i, j: (i, j)),
        out_shape=jax.ShapeDtypeStruct((m, PROJ_W), F32),
        scratch_shapes=[pltpu.VMEM((TM_IN, D_MODEL), BF16)],
        compiler_params=_params(2),
        name="inproj",
    )(x2, g, wt)


def _outproj_body(ya, yb, yc, yd, wa, wb, wc, wd, x_ref, o_ref):
    acc = jnp.dot(ya[...], wa[...], preferred_element_type=F32)
    acc = acc + jnp.dot(yb[...], wb[...], preferred_element_type=F32)
    acc = acc + jnp.dot(yc[...], wc[...], preferred_element_type=F32)
    acc = acc + jnp.dot(yd[...], wd[...], preferred_element_type=F32)
    o_ref[...] = x_ref[...] + acc


def _outproj(ys, w, x2):
    m = x2.shape[0]
    yspec = pl.BlockSpec((TM_OUT, GROUP_W), lambda i, j: (i, 0))
    wspecs = [pl.BlockSpec((GROUP_W, TN_OUT), functools.partial(lambda i, j, k: (k, j), k=k)) for k in range(4)]
    xspec = pl.BlockSpec((TM_OUT, TN_OUT), lambda i, j: (i, j))
    return pl.pallas_call(
        _outproj_body,
        grid=(m // TM_OUT, D_MODEL // TN_OUT),
        in_specs=[yspec] * 4 + wspecs + [xspec],
        out_specs=xspec,
        out_shape=jax.ShapeDtypeStruct((m, D_MODEL), F32),
        compiler_params=_params(2),
        name="outproj",
    )(*ys, w, w, w, w, x2)


def _mix_ad_body(ax, ag, din, db, dc, dg, cw, cb, wa, ba, wx, bx, lam, scw,
                 ya, yd, xbuf, vbuf, a_s, b_s, hst):
    t = T_TOK

    @pl.when(pl.program_id(1) == 0)
    def _():
        xbuf[0:8, :] = jnp.zeros((8, GROUP_W), F32)
        vbuf[0:8, :] = jnp.zeros((8, GROUP_W), F32)
        hst[...] = jnp.zeros((8, GROUP_W), F32)

    xbuf[8:8 + t, :] = ax[...]
    for blk in range(LRU_BLOCKS):
        c = slice(blk * LANES, (blk + 1) * LANES)
        u = (cw[0:1, c] * xbuf[5:5 + t, c] + cw[1:2, c] * xbuf[6:6 + t, c]
             + cw[2:3, c] * xbuf[7:7 + t, c] + cw[3:4, c] * xbuf[8:8 + t, c]) + cb[:, c]
        ub = u.astype(BF16)
        r = jax.nn.sigmoid(jnp.dot(ub, wa[blk], preferred_element_type=F32) + ba[:, c])
        ig = jax.nn.sigmoid(jnp.dot(ub, wx[blk], preferred_element_type=F32) + bx[:, c])
        nl = -lam[:, c]
        sp = jnp.maximum(nl, 0.0) + jnp.log1p(jnp.exp(-jnp.abs(nl)))
        log_a = (-LRU_C * r) * sp
        a = jnp.exp(log_a)
        a_s[:, c] = a
        em1 = jnp.tanh(log_a) * (a * a + 1.0)
        b_s[:, c] = jnp.sqrt(-em1) * (ig * u)
    xbuf[0:8, :] = xbuf[t:t + 8, :]

    row = lax.broadcasted_iota(I32, (8, GROUP_W), 0)

    def group(gi, hprev):
        r0 = pl.multiple_of(gi * 8, 8)
        a = a_s[pl.ds(r0, 8), :]
        b = b_s[pl.ds(r0, 8), :]
        for d in (1, 2, 4):
            ok = row >= d
            b = jnp.where(ok, a * pltpu.roll(b, d, axis=0) + b, b)
            a = jnp.where(ok, a * pltpu.roll(a, d, axis=0), a)
        h = a * hprev + b
        b_s[pl.ds(r0, 8), :] = h
        return jnp.broadcast_to(h[7:8, :], (8, GROUP_W))

    hst[...] = lax.fori_loop(0, t // 8, group, hst[...])

    vbuf[8:8 + t, :] = dc[...] * din[...]
    for blk in range(LRU_BLOCKS):
        c = slice(blk * LANES, (blk + 1) * LANES)
        ya[:, c] = (b_s[:, c] * _silu(ag[:, c])).astype(BF16)
        conv = (scw[0:1, c] * vbuf[6:6 + t, c] + scw[1:2, c] * vbuf[7:7 + t, c]
                + scw[2:3, c] * vbuf[8:8 + t, c])
        yd[:, c] = ((db[:, c] * conv) * _silu(dg[:, c])).astype(BF16)
    vbuf[0:8, :] = vbuf[t:t + 8, :]


def _mix_ad(z, cw, cb, wa, ba, wx, bx, lam, scw):
    nt = SEQ // T_TOK

    def zspec(col):
        return pl.BlockSpec((T_TOK, GROUP_W), lambda b, i: (b * nt + i, col // GROUP_W))

    def full(a):
        return pl.BlockSpec(a.shape, lambda b, i: (0,) * a.ndim)

    small = (cw, cb, wa, ba, wx, bx, lam, scw)
    yspec = pl.BlockSpec((T_TOK, GROUP_W), lambda b, i: (b * nt + i, 0))
    return pl.pallas_call(
        _mix_ad_body,
        grid=(BATCH, nt),
        in_specs=[zspec(COL_A_X), zspec(COL_A_G), zspec(COL_D_IN), zspec(COL_D_B), zspec(COL_D_C),
                  zspec(COL_D_G)] + [full(a) for a in small],
        out_specs=[yspec, yspec],
        out_shape=[jax.ShapeDtypeStruct((BATCH * SEQ, GROUP_W), BF16)] * 2,
        scratch_shapes=[pltpu.VMEM((T_TOK + 8, GROUP_W), F32), pltpu.VMEM((T_TOK + 8, GROUP_W), F32),
                        pltpu.VMEM((T_TOK, GROUP_W), F32), pltpu.VMEM((T_TOK, GROUP_W), F32),
                        pltpu.VMEM((8, GROUP_W), F32)],
        compiler_params=_params(2),
        name="mix_ad",
    )(z, z, z, z, z, z, *small)


def _rms_head(x, gain):
    ms = jnp.mean(x * x, axis=-1, keepdims=True)
    return (x * lax.rsqrt(ms + NORM_EPS)) * gain


def _rope(y, cos, nsin, psin, half):
    return y * cos + pltpu.roll(y, LANES - half, axis=1) * nsin + pltpu.roll(y, half, axis=1) * psin


def _store_vt(dst, g, v):
    for cc in range(T_TOK // KV_TILE):
        dst[0, g, cc] = v[cc * KV_TILE:(cc + 1) * KV_TILE, :].T.astype(BF16)


def _prep_body(bq, bkv_s, bkv_w, cq, ckv, ciq, cik, c128, n128, p128, c64, n64, p64,
               qgn, kgn, qgd, kgd,
               qn_o, qr_o, ks_o, vst_o, kw_o, vwt_o, qd_o, kd_o, vdt_o, iq_o, ik_o):
    r128 = (c128[...], n128[...], p128[...], HEAD_DIM // 8)
    r64 = (c64[...], n64[...], p64[...], IDX_DIM // 8)
    for h in range(N_HEADS):
        c = slice(h * LANES, (h + 1) * LANES)
        y = _rms_head(bq[:, c], qgn[...])
        qn_o[:, c] = (y * Q_PRESCALE).astype(BF16)
        qr_o[:, c] = (_rope(y, *r128) * Q_PRESCALE).astype(BF16)
        qd_o[:, c] = (_rope(_rms_head(cq[:, c], qgd[...]), *r128) * Q_PRESCALE).astype(BF16)
    for g in range(N_KV):
        ck = slice(g * LANES, (g + 1) * LANES)
        cv = slice((N_KV + g) * LANES, (N_KV + g + 1) * LANES)
        ks_o[0, g] = _rope(_rms_head(bkv_s[:, ck], kgn[...]), *r128).astype(BF16)
        _store_vt(vst_o, g, bkv_s[:, cv])
        kw_o[0, g] = _rope(_rms_head(bkv_w[:, ck], kgn[...]), *r128).astype(BF16)
        _store_vt(vwt_o, g, bkv_w[:, cv])
        kd_o[0, g] = _rope(_rms_head(ckv[:, ck], kgd[...]), *r128).astype(BF16)
        _store_vt(vdt_o, g, ckv[:, cv])

    low = lax.broadcasted_iota(I32, (T_TOK, LANES), 1) < IDX_DIM
    ik = _rope(cik[...], *r64)
    hi = ik.astype(BF16).astype(F32)
    ik_o[0, :, 0:LANES] = (hi + pltpu.roll(ik - hi, IDX_DIM, axis=1)).astype(BF16)
    ik_o[0, :, LANES:2 * LANES] = hi.astype(BF16)
    for v in range(IDX_HEADS // 2):
        x = _rope(ciq[:, v * LANES:(v + 1) * LANES], *r64)
        for half in range(2):
            h = 2 * v + half
            xs = x if half == 0 else pltpu.roll(x, IDX_DIM, axis=1)
            xh = jnp.where(low, xs, 0.0)
            hi = xh.astype(BF16).astype(F32)
            iq_o[:, 2 * h * LANES:(2 * h + 1) * LANES] = (hi + pltpu.roll(hi, IDX_DIM, axis=1)).astype(BF16)
            iq_o[:, (2 * h + 1) * LANES:(2 * h + 2) * LANES] = (xh - hi).astype(BF16)


def _prep(z, tabs128, tabs64, qgn, kgn, qgd, kgd):
    nt = SEQ // T_TOK

    def zspec(col, width):
        return pl.BlockSpec((T_TOK, width), lambda b, i: (b * nt + i, col // width))

    tab = pl.BlockSpec((T_TOK, LANES), lambda b, i: (i, 0))
    gain = pl.BlockSpec((1, LANES), lambda b, i: (0, 0))
    qspec = pl.BlockSpec((T_TOK, GROUP_W), lambda b, i: (b * nt + i, 0))
    kspec = pl.BlockSpec((1, N_KV, T_TOK, LANES), lambda b, i: (b, 0, i, 0))
    vtspec = pl.BlockSpec((1, N_KV, T_TOK // KV_TILE, LANES, KV_TILE), lambda b, i: (b, 0, i, 0, 0))
    qshape = jax.ShapeDtypeStruct((BATCH * SEQ, GROUP_W), BF16)
    kshape = jax.ShapeDtypeStruct((BATCH, N_KV, SEQ, LANES), BF16)
    vtshape = jax.ShapeDtypeStruct((BATCH, N_KV, SEQ // KV_TILE, LANES, KV_TILE), BF16)
    return pl.pallas_call(
        _prep_body,
        grid=(BATCH, nt),
        in_specs=[zspec(COL_B_Q, GROUP_W), zspec(COL_B_KV + 512, 512), zspec(COL_B_KV + 1024, 512),
                  zspec(COL_C_Q, GROUP_W), zspec(COL_C_KV, 512), zspec(COL_C_IQ, 512), zspec(COL_C_IK, LANES)]
                 + [tab] * 6 + [gain] * 4,
        out_specs=[qspec, qspec, kspec, vtspec, kspec, vtspec, qspec, kspec, vtspec,
                   pl.BlockSpec((T_TOK, 2 * LANES * IDX_HEADS), lambda b, i: (b * nt + i, 0)),
                   pl.BlockSpec((1, T_TOK, 2 * LANES), lambda b, i: (b, i, 0))],
        out_shape=[qshape, qshape, kshape, vtshape, kshape, vtshape, qshape, kshape, vtshape,
                   jax.ShapeDtypeStruct((BATCH * SEQ, 2 * LANES * IDX_HEADS), BF16),
                   jax.ShapeDtypeStruct((BATCH, SEQ, 2 * LANES), BF16)],
        compiler_params=_params(2),
        name="attn_prep",
    )(z, z, z, z, z, z, z, *tabs128, *tabs64, qgn, kgn, qgd, kgd)


def _cmp_body(zc, pe, w1, w2, kg, o, ot, kbuf):
    kbuf[0:SEQ, :] = zc[...]
    kbuf[SEQ:SEQ + LANES, :] = jnp.zeros((LANES, LANES), F32)
    acc = jnp.zeros((N_CMP_PAD, LANES), F32)
    for l in range(CMP_LEN):
        rows = kbuf[pl.ds(l, N_CMP_PAD, stride=CMP_STRIDE), :] + pe[0, l:l + 1, :]
        acc = acc + jnp.dot(rows.astype(BF16), w1[0, l], preferred_element_type=F32)
    out = jnp.dot(_silu(acc).astype(BF16), w2[0], preferred_element_type=F32)
    res = jnp.where(pl.program_id(1) == 0, _rms_head(out, kg[...]), out)
    o[0, 0, 0] = res.astype(BF16)
    ot[0, 0, 0] = res.T.astype(BF16)


def _compress(z, pe, w1, w2, kg):
    base = COL_B_KV // LANES
    return pl.pallas_call(
        _cmp_body,
        grid=(BATCH, 2, N_KV),
        in_specs=[pl.BlockSpec((SEQ, LANES), lambda b, kv, g: (b, base + kv * N_KV + g)),
                  pl.BlockSpec((1, CMP_LEN, LANES), lambda b, kv, g: (kv, 0, 0)),
                  pl.BlockSpec((1, CMP_LEN, LANES, LANES), lambda b, kv, g: (kv, 0, 0, 0)),
                  pl.BlockSpec((1, LANES, LANES), lambda b, kv, g: (kv, 0, 0)),
                  pl.BlockSpec((1, LANES), lambda b, kv, g: (0, 0))],
        out_specs=[pl.BlockSpec((1, 1, 1, N_CMP_PAD, LANES), lambda b, kv, g: (b, kv, g, 0, 0)),
                   pl.BlockSpec((1, 1, 1, LANES, N_CMP_PAD), lambda b, kv, g: (b, kv, g, 0, 0))],
        out_shape=[jax.ShapeDtypeStruct((BATCH, 2, N_KV, N_CMP_PAD, LANES), BF16),
                   jax.ShapeDtypeStruct((BATCH, 2, N_KV, LANES, N_CMP_PAD), BF16)],
        scratch_shapes=[pltpu.VMEM((SEQ + LANES, LANES), F32)],
        compiler_params=_params(3),
        name="nsa_compress",
    )(z, pe, w1, w2, kg)


def _key_norm_bound(k_ref):
    def body(c, mx):
        x = k_ref[pl.ds(pl.multiple_of(c * 512, 512), 512), :].astype(F32)
        return jnp.maximum(mx, jnp.max(jnp.sum(x * x, axis=1, keepdims=True), axis=0, keepdims=True))
    return jnp.broadcast_to(lax.fori_loop(0, SEQ // 512, body, jnp.zeros((1, 1), F32)), (1, TQ))


def _query_norms(q_ref):
    head_of_lane = lax.shift_right_logical(lax.broadcasted_iota(I32, (N_HEADS, GROUP_W), 1), 7)
    pick = jnp.where(head_of_lane == lax.broadcasted_iota(I32, (N_HEADS, GROUP_W), 0), 1.0, 0.0).astype(BF16)
    q = q_ref[...].astype(F32)
    return lax.dot_general(pick, (q * q).astype(BF16), NT_DIMS, preferred_element_type=F32)


def _flash_heads(q_ref, qn2, slot0, groups, jlo, jhi, m_s, l_s, acc_s):
    n_heads = sum(len(g[2]) for g in groups)
    lo, hi = slot0, slot0 + n_heads
    l_s[lo:hi, :] = jnp.zeros((n_heads, TQ), F32)
    for slot in range(lo, hi):
        acc_s[slot] = jnp.zeros((HEAD_DIM, TQ), F32)

    bounds = []
    for _, _, cs, _, kmax2 in groups:
        for c in cs:
            h = c.start // LANES
            bounds.append(jnp.sqrt(qn2[h:h + 1, :] * kmax2) * BOUND_SLACK)
    m_bound = jnp.concatenate(bounds, axis=0)

    def scores(j):
        r0 = pl.multiple_of(j * KV_TILE, KV_TILE)
        ss, vts, biases = [], [], {}
        for k_ref, vt_ref, cs, bias_fn, _ in groups:
            k = k_ref[pl.ds(r0, KV_TILE), :]
            vt = vt_ref[j]
            if bias_fn not in biases:
                biases[bias_fn] = bias_fn(j)
            b = biases[bias_fn]
            for c in cs:
                ss.append(lax.dot_general(k, q_ref[:, c], NT_DIMS, preferred_element_type=F32) + b)
                vts.append(vt)
        return ss, vts

    def max_pass(j, m):
        ss, _ = scores(j)
        return jnp.maximum(m, jnp.concatenate([_fold8(s, jnp.maximum) for s in ss], axis=0))

    def exact_max():
        m8 = lax.fori_loop(jlo, jhi, max_pass, jnp.full((8 * n_heads, TQ), NEG, F32))
        m_fin = jnp.concatenate([jnp.max(m8[8 * n:8 * n + 8], axis=0, keepdims=True)
                                 for n in range(n_heads)], axis=0)
        return jnp.maximum(m_fin, M_FLOOR)

    m_s[lo:hi, :] = lax.cond(jnp.max(m_bound) > BOUND_LIMIT, exact_max, lambda: m_bound)

    def tile(j):
        ss, vts = scores(j)
        m_ref = m_s[lo:hi, :]
        ps = [jnp.exp2(s - m_ref[n:n + 1, :]) for n, s in enumerate(ss)]
        l_add = jnp.concatenate([jnp.sum(p, axis=0, keepdims=True) for p in ps], axis=0)
        return l_add, [jnp.dot(vt, p.astype(BF16), preferred_element_type=F32) for vt, p in zip(vts, ps)]

    def one(j, carry):
        l_add, pv = tile(j)
        l_s[lo:hi, :] = l_s[lo:hi, :] + l_add
        for n in range(n_heads):
            acc_s[lo + n] = acc_s[lo + n] + pv[n]
        return carry

    def two(jj, carry):
        la, pa = tile(jlo + 2 * jj)
        lb, pb = tile(jlo + 2 * jj + 1)
        l_s[lo:hi, :] = l_s[lo:hi, :] + (la + lb)
        for n in range(n_heads):
            acc_s[lo + n] = acc_s[lo + n] + (pa[n] + pb[n])
        return carry

    pairs = lax.shift_right_logical(jhi - jlo, 1)
    lax.fori_loop(0, pairs, two, 0)
    lax.fori_loop(jlo + 2 * pairs, jhi, one, 0)


def _flash_out(slot, l_s, acc_s):
    return acc_s[slot] * (1.0 / l_s[slot:slot + 1, :])


def _nsa_body(qn, qr, kc, vct, ks, vst, kw, vwt, gate, bg, ovt, yb, imp_s, bias_s, m_s, l_s, acc_s, kn_s):
    i = pl.program_id(1)

    @pl.when(i == 0)
    def _():
        for g in range(N_KV):
            kn_s[g:g + 1, :] = _key_norm_bound(ks.at[0, g])
            kn_s[N_KV + g:N_KV + g + 1, :] = _key_norm_bound(kw.at[0, g])

    tq = i * TQ + lax.broadcasted_iota(I32, (1, TQ), 1)
    n_kv = i + 1
    sg = jax.nn.sigmoid(gate[...].T[0:3 * N_HEADS, :])
    cm = (lax.broadcasted_iota(I32, (N_CMP_PAD, TQ), 0) * CMP_STRIDE + (CMP_LEN - 1)) <= tq
    kk = lax.broadcasted_iota(I32, (N_SLC, TQ), 0)
    sub8 = lax.broadcasted_iota(I32, (8, TQ), 0)
    blk_t = lax.shift_right_logical(tq, 6)
    valid = kk <= blk_t
    force = (kk == 0) | (kk == blk_t) | (kk == blk_t - 1)
    rowkv = lax.broadcasted_iota(I32, (KV_TILE, TQ), 0)
    cmp_slot, slc_slot, win_slot = 0, N_HEADS, 2 * N_HEADS
    head_cols = [[slice(h * LANES, (h + 1) * LANES) for h in range(HPG * g, HPG * (g + 1))]
                 for g in range(N_KV)]

    def win_bias(j):
        d = tq - (j * KV_TILE + rowkv)
        return jnp.where((d >= 0) & (d < WINDOW), 0.0, NEG)

    def slc_bias(g):
        return lambda j: bias_s[g, pl.ds(pl.multiple_of(j * KV_TILE, KV_TILE), KV_TILE), :]

    for g in range(N_KV):
        kcg = kc[0, 0, g]
        vcg = vct[0, 0, g]
        cols = head_cols[g]
        ss = [jnp.where(cm, lax.dot_general(kcg, qn[:, c], NT_DIMS, preferred_element_type=F32), NEG)
              for c in cols]
        es = [jnp.exp2(s - jnp.max(s, axis=0, keepdims=True)) for s in ss]
        ps = [jnp.where(cm, e * (1.0 / jnp.sum(e, axis=0, keepdims=True)), 0.0) for e in es]
        for hh, p in enumerate(ps):
            acc_s[cmp_slot + HPG * g + hh] = jnp.dot(vcg, p.astype(BF16), preferred_element_type=F32)
        psum = (ps[0] + ps[1]) + (ps[2] + ps[3])

        p_hi = psum.astype(BF16)
        p_lo = (psum - p_hi.astype(F32)).astype(BF16)
        imp = (jnp.dot(ovt[...], p_hi, preferred_element_type=F32)
               + jnp.dot(ovt[...], p_lo, preferred_element_type=F32))
        imp = jnp.where(force, FORCE, jnp.where(valid, imp, -FORCE))
        imp_s[...] = imp
        bands = [imp[8 * r:8 * r + 8] for r in range(N_SLC // 8)]
        ranks = [jnp.zeros((8, TQ), F32) for _ in bands]
        for j in range(N_SLC):
            vj = imp_s[j:j + 1, :]
            for r, x in enumerate(bands):
                if r < j // 8:
                    hit = vj > x
                elif r > j // 8:
                    hit = vj >= x
                else:
                    hit = (vj > x) | ((vj == x) & (sub8 > j % 8))
                ranks[r] = ranks[r] + jnp.where(hit, 1.0, 0.0)
        imp_s[...] = jnp.where(jnp.concatenate(ranks, axis=0) < SLC_TOPN, 0.0, NEG)

        def fill(j, carry):
            r0 = pl.multiple_of(j * KV_TILE, KV_TILE)
            nb = KV_TILE // SLC_BLK
            tile = jnp.concatenate([jnp.broadcast_to(imp_s[pl.ds(nb * j + u, 1), :], (SLC_BLK, TQ))
                                    for u in range(nb)], axis=0)
            bias_s[g, pl.ds(r0, KV_TILE), :] = jnp.where(r0 + rowkv <= tq, tile, NEG)
            return carry
        lax.fori_loop(0, n_kv, fill, 0)

    qn2 = _query_norms(qr)
    _flash_heads(qr, qn2, slc_slot, [(ks.at[0, g], vst.at[0, g], head_cols[g], slc_bias(g), kn_s[g:g + 1, :])
                                for g in range(N_KV)], 0, n_kv, m_s, l_s, acc_s)
    _flash_heads(qr, qn2, win_slot, [(kw.at[0, g], vwt.at[0, g], head_cols[g], win_bias,
                                 kn_s[N_KV + g:N_KV + g + 1, :]) for g in range(N_KV)],
                 jnp.maximum(n_kv - 1 - WINDOW // KV_TILE, 0), n_kv, m_s, l_s, acc_s)
    for h in range(N_HEADS):
        c = slice(h * LANES, (h + 1) * LANES)
        tot = (sg[h:h + 1] * acc_s[cmp_slot + h]
               + sg[N_HEADS + h:N_HEADS + h + 1] * _flash_out(slc_slot + h, l_s, acc_s)
               + sg[2 * N_HEADS + h:2 * N_HEADS + h + 1] * _flash_out(win_slot + h, l_s, acc_s))
        yb[:, c] = (tot.T * _silu(bg[:, c])).astype(BF16)


def _nsa(z, qn, qr, cmp_, cmpt, ks, vst, kw, vwt, ovt):
    nq = SEQ // TQ
    qspec = pl.BlockSpec((TQ, GROUP_W), lambda b, i: (b * nq + i, 0))
    kspec = pl.BlockSpec((1, N_KV, SEQ, LANES), lambda b, i: (b, 0, 0, 0))
    vtspec = pl.BlockSpec((1, N_KV, SEQ // KV_TILE, LANES, KV_TILE), lambda b, i: (b, 0, 0, 0, 0))
    return pl.pallas_call(
        _nsa_body,
        grid=(BATCH, nq),
        in_specs=[qspec, qspec,
                  pl.BlockSpec((1, 1, N_KV, N_CMP_PAD, LANES), lambda b, i: (b, 0, 0, 0, 0)),
                  pl.BlockSpec((1, 1, N_KV, LANES, N_CMP_PAD), lambda b, i: (b, 1, 0, 0, 0)),
                  kspec, vtspec, kspec, vtspec,
                  pl.BlockSpec((TQ, LANES), lambda b, i: (b * nq + i, COL_B_GATE // LANES)),
                  pl.BlockSpec((TQ, GROUP_W), lambda b, i: (b * nq + i, COL_B_G // GROUP_W)),
                  pl.BlockSpec((N_SLC, N_CMP_PAD), lambda b, i: (0, 0))],
        out_specs=qspec,
        out_shape=jax.ShapeDtypeStruct((BATCH * SEQ, GROUP_W), BF16),
        scratch_shapes=[pltpu.VMEM((N_SLC, TQ), F32), pltpu.VMEM((N_KV, SEQ, TQ), F32),
                        pltpu.VMEM((3 * N_HEADS, TQ), F32), pltpu.VMEM((3 * N_HEADS, TQ), F32),
                        pltpu.VMEM((3 * N_HEADS, HEAD_DIM, TQ), F32), pltpu.VMEM((8, TQ), F32)],
        compiler_params=_params(2),
        name="nsa_attn",
    )(qn, qr, cmp_, cmpt, ks, vst, kw, vwt, z, z, ovt)


def _dsa_body(qd, kd, vdt, iq, ik, iw, cg, yc, sc_s, m_s, l_s, acc_s, kn_s):
    i = pl.program_id(1)

    @pl.when(i == 0)
    def _():
        for g in range(N_KV):
            kn_s[g:g + 1, :] = _key_norm_bound(kd.at[0, g])

    tq = i * TQ + lax.broadcasted_iota(I32, (1, TQ), 1)
    n_kv = i + 1
    iwt = iw[...].T[0:IDX_HEADS, :] * IW_SCALE
    rowkv = lax.broadcasted_iota(I32, (KV_TILE, TQ), 0)
    kf = float(DSA_TOPK)

    def score_tile(j):
        r0 = pl.multiple_of(j * KV_TILE, KV_TILE)
        ikc = ik[0, pl.ds(r0, KV_TILE), :]
        sc = jnp.zeros((KV_TILE, TQ), F32)
        for h in range(IDX_HEADS):
            r = lax.dot_general(ikc, iq[:, 2 * h * LANES:(2 * h + 2) * LANES], NT_DIMS,
                                preferred_element_type=F32)
            sc = sc + jnp.maximum(r, 0.0) * iwt[h:h + 1, :]
        return r0, sc

    def fill(j, carry):
        mn, mx = carry
        r0, sc = score_tile(j)
        sc_s[pl.ds(r0, KV_TILE), :] = sc
        return (jnp.minimum(mn, _fold8(sc, jnp.minimum)), jnp.maximum(mx, _fold8(sc, jnp.maximum)))

    def fill2(jj, carry):
        return fill(2 * jj + 1, fill(2 * jj, carry))

    half = lax.shift_right_logical(i, 1)
    mm = lax.fori_loop(0, half, fill2, (jnp.full((8, TQ), jnp.inf, F32), jnp.full((8, TQ), -jnp.inf, F32)))
    mn8, mx8 = lax.fori_loop(2 * half, i, fill, mm)
    r0, sc = score_tile(i)
    ok = r0 + rowkv <= tq
    sc_s[pl.ds(r0, KV_TILE), :] = jnp.where(ok, sc, -jnp.inf)
    lo0 = jnp.min(jnp.minimum(mn8, _fold8(jnp.where(ok, sc, jnp.inf), jnp.minimum)), axis=0, keepdims=True)
    hi0 = jnp.max(jnp.maximum(mx8, _fold8(jnp.where(ok, sc, -jnp.inf), jnp.maximum)), axis=0, keepdims=True)

    def sweep(fn, n_out):
        def body(c, accs):
            r0 = pl.multiple_of(c * KV_TILE, KV_TILE)
            inds = fn(sc_s[pl.ds(r0, KV_TILE), :], r0 + rowkv)
            return tuple(a + _fold8(jnp.where(ind, 1.0, 0.0)) for a, ind in zip(accs, inds))
        accs = lax.fori_loop(0, n_kv, body, tuple(jnp.zeros((8, TQ), F32) for _ in range(n_out)))
        return tuple(jnp.sum(a, axis=0, keepdims=True) for a in accs)

    def data_span(lo, hi):
        def body(c, carry):
            mn, mx = carry
            x = sc_s[pl.ds(pl.multiple_of(c * KV_TILE, KV_TILE), KV_TILE), :]
            mn = jnp.minimum(mn, _fold8(jnp.where(x >= lo, x, jnp.inf), jnp.minimum))
            mx = jnp.maximum(mx, _fold8(jnp.where(x <= hi, x, -jnp.inf), jnp.maximum))
            return mn, mx
        mn, mx = lax.fori_loop(0, n_kv, body, (jnp.full((8, TQ), jnp.inf, F32), jnp.full((8, TQ), -jnp.inf, F32)))
        return jnp.min(mn, axis=0, keepdims=True), jnp.max(mx, axis=0, keepdims=True)

    nvalid = (tq + 1).astype(F32)
    act0 = jnp.where(nvalid > kf, 1.0, 0.0)

    def bisect(_, c):
        lo, hi, clo, act = c
        on = act > 0.0
        mid = lo + (hi - lo) * 0.5
        exh = (mid <= lo) | (mid >= hi)
        p = jnp.where(exh, hi, mid)
        (cnt,) = sweep(lambda x, srow: (x >= p,), 1)
        up = on & (cnt >= kf)
        dn = on & (cnt < kf)
        clo = jnp.where(up, cnt, clo)
        act = jnp.where(on & (clo != kf) & jnp.logical_not(exh), 1.0, 0.0)
        return jnp.where(up, p, lo), jnp.where(dn, p, hi), clo, act

    def snap(c):
        lo, hi, clo, act = c
        on = act > 0.0
        dmin, dmax = data_span(lo, hi)
        act = jnp.where(on & (dmin < dmax), 1.0, 0.0)
        return jnp.where(on, dmin, lo), jnp.where(on, dmax, hi), clo, act

    def cond(c):
        return c[0] > 0.0

    def round_(c):
        state = lax.fori_loop(0, SNAP_EVERY - 1, bisect, snap(c[1:]))
        return (jnp.sum(state[3]),) + state

    state = lax.fori_loop(0, SNAP_FIRST, bisect, (lo0, hi0, nvalid, act0))
    _, thr, _, _, _ = lax.while_loop(cond, round_, (jnp.sum(state[3]),) + state)

    cnt_g, cnt_e = sweep(lambda x, srow: (x > thr, x == thr), 2)
    need = kf - cnt_g
    excess = (cnt_g + cnt_e) > kf

    def tie_break():
        def it(_, c):
            jl, jh = c
            jm = lax.shift_right_arithmetic(jl + jh, 1)
            (cnt,) = sweep(lambda x, srow: ((x == thr) & (srow <= jm),), 1)
            ge = cnt >= need
            return jnp.where(ge, jl, jm), jnp.where(ge, jm, jh)
        _, jh = lax.fori_loop(0, 12, it, (jnp.full((1, TQ), -1, I32), jnp.full((1, TQ), SEQ - 1, I32)))
        return jnp.where(excess, jh, SEQ)

    last = lax.cond(jnp.sum(jnp.where(excess, 1.0, 0.0)) > 0.0, tie_break,
                    lambda: jnp.full((1, TQ), SEQ, I32))

    def to_bias(c, carry):
        r0 = pl.multiple_of(c * KV_TILE, KV_TILE)
        x = sc_s[pl.ds(r0, KV_TILE), :]
        sel = (x > thr) | ((x == thr) & (r0 + rowkv <= last))
        sc_s[pl.ds(r0, KV_TILE), :] = jnp.where(sel, 0.0, NEG)
        return carry
    lax.fori_loop(0, n_kv, to_bias, 0)

    def bias(j):
        return sc_s[pl.ds(pl.multiple_of(j * KV_TILE, KV_TILE), KV_TILE), :]

    groups = [(kd.at[0, g], vdt.at[0, g], [slice(h * LANES, (h + 1) * LANES)
                                           for h in range(HPG * g, HPG * (g + 1))], bias, kn_s[g:g + 1, :])
              for g in range(N_KV)]
    _flash_heads(qd, _query_norms(qd), 0, groups, 0, n_kv, m_s, l_s, acc_s)
    for h in range(N_HEADS):
        c = slice(h * LANES, (h + 1) * LANES)
        yc[:, c] = (_flash_out(h, l_s, acc_s).T * _silu(cg[:, c])).astype(BF16)


def _dsa(z, qd, kd, vdt, iq, ik):
    nq = SEQ // TQ
    qspec = pl.BlockSpec((TQ, GROUP_W), lambda b, i: (b * nq + i, 0))
    return pl.pallas_call(
        _dsa_body,
        grid=(BATCH, nq),
        in_specs=[qspec,
                  pl.BlockSpec((1, N_KV, SEQ, LANES), lambda b, i: (b, 0, 0, 0)),
                  pl.BlockSpec((1, N_KV, SEQ // KV_TILE, LANES, KV_TILE), lambda b, i: (b, 0, 0, 0, 0)),
                  pl.BlockSpec((TQ, 2 * LANES * IDX_HEADS), lambda b, i: (b * nq + i, 0)),
                  pl.BlockSpec((1, SEQ, 2 * LANES), lambda b, i: (b, 0, 0)),
                  pl.BlockSpec((TQ, LANES), lambda b, i: (b * nq + i, COL_C_IW // LANES)),
                  pl.BlockSpec((TQ, GROUP_W), lambda b, i: (b * nq + i, COL_C_G // GROUP_W))],
        out_specs=qspec,
        out_shape=jax.ShapeDtypeStruct((BATCH * SEQ, GROUP_W), BF16),
        scratch_shapes=[pltpu.VMEM((SEQ, TQ), F32), pltpu.VMEM((N_HEADS, TQ), F32),
                        pltpu.VMEM((N_HEADS, TQ), F32), pltpu.VMEM((N_HEADS, HEAD_DIM, TQ), F32),
                        pltpu.VMEM((8, TQ), F32)],
        compiler_params=_params(2),
        name="dsa_attn",
    )(qd, kd, vdt, iq, ik, z, z)


def _segment_moves():
    src = {}
    o = 0
    names = ("a_x", "a_g", "b_q", "b_g", "b_kv", "b_gate", "c_q", "c_g", "c_kv", "c_iq", "c_ik", "c_iw",
             "d_in", "d_b", "d_c", "d_g")
    for n, s in zip(names, SPLIT_SIZES):
        src[n] = (o, s)
        o += s
    dst = dict(a_x=COL_A_X, a_g=COL_A_G, b_q=COL_B_Q, b_g=COL_B_G, c_q=COL_C_Q, c_g=COL_C_G,
               d_in=COL_D_IN, d_b=COL_D_B, d_c=COL_D_C, d_g=COL_D_G, b_kv=COL_B_KV, c_kv=COL_C_KV,
               c_iq=COL_C_IQ, b_gate=COL_B_GATE, c_ik=COL_C_IK, c_iw=COL_C_IW)
    return [(src[n][0], dst[n], src[n][1]) for n in names]


def _wprep_body(w_ref, o_ref):
    o_ref[COL_B_GATE:PROJ_W, :] = jnp.zeros((PROJ_W - COL_B_GATE, TR_W), BF16)
    for s, d, width in _segment_moves():
        for c in range(0, width, GROUP_W):
            n = min(GROUP_W, width - c)
            o_ref[d + c:d + c + n, :] = w_ref[s + c:s + c + n, :].astype(BF16)


def _proj_weight(wt, layer):
    _, n, k = wt.shape
    return pl.pallas_call(
        _wprep_body,
        grid=(k // TR_W,),
        in_specs=[pl.BlockSpec((None, n, TR_W), lambda i: (layer, 0, i))],
        out_specs=pl.BlockSpec((PROJ_W, TR_W), lambda i: (0, i)),
        out_shape=jax.ShapeDtypeStruct((PROJ_W, k), BF16),
        compiler_params=_params(1),
        name="w_in_layout",
    )(wt)


def _cast_body(w_ref, o_ref):
    o_ref[...] = w_ref[...].astype(BF16)


def _to_bf16(w, layer):
    _, k, n = w.shape
    return pl.pallas_call(
        _cast_body,
        grid=(k // 512,),
        in_specs=[pl.BlockSpec((None, 512, n), lambda i: (layer, i, 0))],
        out_specs=pl.BlockSpec((512, n), lambda i: (i, 0)),
        out_shape=jax.ShapeDtypeStruct((k, n), BF16),
        compiler_params=_params(1),
        name="w_cast",
    )(w)


def _rope_tables(head_dim):
    r = head_dim // 4
    half = r // 2
    pos = jnp.arange(SEQ, dtype=jnp.int32)
    inv = ROPE_THETA ** (-jnp.arange(half, dtype=F32) * 2.0 / r)
    ang = pos.astype(F32)[:, None] * inv[None, :]
    cos, sin = jnp.cos(ang), jnp.sin(ang)
    one = jnp.ones((SEQ, head_dim - r), F32)
    zero = jnp.zeros((SEQ, head_dim - r), F32)
    zh = jnp.zeros((SEQ, half), F32)
    reps = LANES // head_dim
    return tuple(jnp.tile(t, (1, reps)) for t in (
        jnp.concatenate([cos, cos, one], axis=1),
        jnp.concatenate([-sin, zh, zero], axis=1),
        jnp.concatenate([zh, sin, zero], axis=1)))


def _overlap_t():
    cs = jnp.arange(N_CMP_PAD) * CMP_STRIDE
    ss = jnp.arange(N_SLC) * SLC_BLK
    ov = jnp.clip(jnp.minimum(cs[None, :] + CMP_LEN, ss[:, None] + SLC_BLK)
                  - jnp.maximum(cs[None, :], ss[:, None]), 0, None).astype(F32) / CMP_LEN
    return jnp.where(jnp.arange(N_CMP_PAD)[None, :] < N_CMP_PAD - 1, ov, 0.0).astype(BF16)


def kernel(x, norm_g, w_in, w_out, lru_conv_w, lru_conv_b, lru_wa, lru_ba, lru_wx, lru_bx, lru_lambda,
           nsa_q_gain, nsa_k_gain, cmp_pe_k, cmp_w1_k, cmp_w2_k, cmp_pe_v, cmp_w1_v, cmp_w2_v,
           dsa_q_gain, dsa_k_gain, sc_conv_w):
    b, s, d = x.shape
    assert (b, s, d) == (BATCH, SEQ, D_MODEL)
    tabs128 = _rope_tables(HEAD_DIM)
    tabs64 = _rope_tables(IDX_DIM)
    ovt = _overlap_t()
    x2 = x.reshape(b * s, d)
    w_in_t = jnp.swapaxes(w_in, 1, 2)

    def row(v):
        return v.reshape(1, -1)

    for l in range(DEPTH):
        z = _inproj(x2, row(norm_g[l]), _proj_weight(w_in_t, l))
        ya, yd = _mix_ad(z, lru_conv_w[l], row(lru_conv_b[l]), lru_wa[l].astype(BF16), row(lru_ba[l]),
                         lru_wx[l].astype(BF16), row(lru_bx[l]), row(lru_lambda[l]), sc_conv_w[l])
        qn, qr, ks, vst, kw, vwt, qd, kd, vdt, iq, ik = _prep(
            z, tabs128, tabs64, row(nsa_q_gain[l]), row(nsa_k_gain[l]), row(dsa_q_gain[l]), row(dsa_k_gain[l]))
        cmp_, cmpt = _compress(z, jnp.stack([cmp_pe_k[l], cmp_pe_v[l]]),
                               jnp.stack([cmp_w1_k[l], cmp_w1_v[l]]).astype(BF16),
                               jnp.stack([cmp_w2_k[l], cmp_w2_v[l]]).astype(BF16), row(nsa_k_gain[l]))
        yb = _nsa(z, qn, qr, cmp_, cmpt, ks, vst, kw, vwt, ovt)
        yc = _dsa(z, qd, kd, vdt, iq, ik)
        x2 = _outproj((ya, yb, yc, yd), _to_bf16(w_out, l), x2)
    return x2.reshape(b, s, d)
```

```python
import functools

import jax
import jax.numpy as jnp
from jax import lax
from jax.experimental import pallas as pl
from jax.experimental.pallas import tpu as pltpu

F32 = jnp.float32
BF16 = jnp.bfloat16
I32 = jnp.int32

D_MODEL = 4096
BATCH = 2
SEQ = 4096
DEPTH = 2
GROUP_W = D_MODEL // 4
HEAD_DIM = 128
N_HEADS = GROUP_W // HEAD_DIM
N_KV = N_HEADS // 4
HPG = N_HEADS // N_KV
ROPE_THETA = 500000.0
NORM_EPS = 1e-6
NEG = -1e30
FORCE = 1e6
LRU_BLOCKS = N_HEADS
LRU_C = 8.0
CONV_A = 4
CONV_D = 3
CMP_LEN = 32
CMP_STRIDE = 16
SLC_BLK = 64
SLC_TOPN = 16
WINDOW = 512
IDX_HEADS = 8
IDX_DIM = 64
DSA_TOPK = 256
SCALE = HEAD_DIM ** -0.5
Q_PRESCALE = SCALE * 1.4426950408889634
IW_SCALE = IDX_HEADS ** -0.5 * IDX_DIM ** -0.5

SPLIT_SIZES = (
    GROUP_W, GROUP_W,
    N_HEADS * HEAD_DIM, GROUP_W, 6 * N_KV * HEAD_DIM, 3 * N_HEADS,
    N_HEADS * HEAD_DIM, GROUP_W, 2 * N_KV * HEAD_DIM,
    IDX_HEADS * IDX_DIM, IDX_DIM, IDX_HEADS,
    GROUP_W, GROUP_W, GROUP_W, GROUP_W,
)

LANES = 128
COL_A_X, COL_A_G, COL_B_Q, COL_B_G, COL_C_Q, COL_C_G = 0, 1024, 2048, 3072, 4096, 5120
COL_D_IN, COL_D_B, COL_D_C, COL_D_G = 6144, 7168, 8192, 9216
COL_B_KV, COL_C_KV, COL_C_IQ = 10240, 11776, 12288
COL_B_GATE, COL_C_IK, COL_C_IW = 12800, 12928, 13056
PROJ_W = 13312

N_CMP_PAD = SEQ // CMP_STRIDE
N_SLC = SEQ // SLC_BLK
KV_TILE = 256
M_FLOOR = -1e20
BOUND_SLACK = 1.01
BOUND_LIMIT = 60.0
SNAP_FIRST, SNAP_EVERY = 8, 4

VMEM_LIMIT_BYTES = 56 * 1024 * 1024

TM_IN, TN_IN = 1024, 512
TM_OUT, TN_OUT = 1024, 512
T_TOK = 256
TR_W = 128
TQ = 256

NT_DIMS = (((1,), (1,)), ((), ()))


def _params(n_axes):
    return pltpu.CompilerParams(dimension_semantics=("arbitrary",) * n_axes,
                                vmem_limit_bytes=VMEM_LIMIT_BYTES)


def _sigmoid(x):
    return 0.5 * jnp.tanh(0.5 * x) + 0.5


def _silu(x):
    return x * _sigmoid(x)


def _fold8(x, op=jnp.add):
    parts = [x[r:r + 8] for r in range(0, x.shape[0], 8)]
    while len(parts) > 1:
        nxt = [op(parts[a], parts[a + 1]) for a in range(0, len(parts) - 1, 2)]
        if len(parts) % 2:
            nxt.append(parts[-1])
        parts = nxt
    return parts[0]


def _inproj_body(x_ref, g_ref, w_ref, o_ref, h_ref):
    @pl.when(pl.program_id(1) == 0)
    def _():
        def rows(r, carry):
            r0 = pl.multiple_of(r * 64, 64)
            x = x_ref[pl.ds(r0, 64), :]
            ms = jnp.mean(x * x, axis=-1, keepdims=True)
            h_ref[pl.ds(r0, 64), :] = ((x * lax.rsqrt(ms + NORM_EPS)) * g_ref[...]).astype(BF16)
            return carry
        lax.fori_loop(0, TM_IN // 64, rows, 0)

    o_ref[...] = lax.dot_general(h_ref[...], w_ref[...], NT_DIMS, preferred_element_type=F32)


def _inproj(x2, g, wt):
    m = x2.shape[0]
    return pl.pallas_call(
        _inproj_body,
        grid=(m // TM_IN, PROJ_W // TN_IN),
        in_specs=[pl.BlockSpec((TM_IN, D_MODEL), lambda i, j: (i, 0)),
                  pl.BlockSpec((1, D_MODEL), lambda i, j: (0, 0)),
                  pl.BlockSpec((TN_IN, D_MODEL), lambda i, j: (j, 0))],
        out_specs=pl.BlockSpec((TM_IN, TN_IN), lambda i, j: (i, j)),
        out_shape=jax.ShapeDtypeStruct((m, PROJ_W), F32),
        scratch_shapes=[pltpu.VMEM((TM_IN, D_MODEL), BF16)],
        compiler_params=_params(2),
        name="inproj",
    )(x2, g, wt)


def _outproj_body(ya, yb, yc, yd, wa, wb, wc, wd, x_ref, o_ref):
    acc = jnp.dot(ya[...], wa[...].astype(BF16), preferred_element_type=F32)
    acc = acc + jnp.dot(yb[...], wb[...].astype(BF16), preferred_element_type=F32)
    acc = acc + jnp.dot(yc[...], wc[...].astype(BF16), preferred_element_type=F32)
    acc = acc + jnp.dot(yd[...], wd[...].astype(BF16), preferred_element_type=F32)
    o_ref[...] = x_ref[...] + acc


def _outproj(ys, w, layer, x2):
    m = x2.shape[0]
    yspec = pl.BlockSpec((TM_OUT, GROUP_W), lambda i, j: (i, 0))
    wspecs = [pl.BlockSpec((None, GROUP_W, TN_OUT), functools.partial(lambda i, j, k: (layer, k, j), k=k))
              for k in range(4)]
    xspec = pl.BlockSpec((TM_OUT, TN_OUT), lambda i, j: (i, j))
    return pl.pallas_call(
        _outproj_body,
        grid=(m // TM_OUT, D_MODEL // TN_OUT),
        in_specs=[yspec] * 4 + wspecs + [xspec],
        out_specs=xspec,
        out_shape=jax.ShapeDtypeStruct((m, D_MODEL), F32),
        compiler_params=_params(2),
        name="outproj",
    )(*ys, w, w, w, w, x2)


def _mix_ad_body(ax, ag, din, db, dc, dg, cw, cb, wa, ba, wx, bx, lam, scw,
                 ya, yd, xbuf, vbuf, a_s, b_s, hst):
    t = T_TOK

    @pl.when(pl.program_id(1) == 0)
    def _():
        xbuf[0:8, :] = jnp.zeros((8, GROUP_W), F32)
        vbuf[0:8, :] = jnp.zeros((8, GROUP_W), F32)
        hst[...] = jnp.zeros((8, GROUP_W), F32)

    xbuf[8:8 + t, :] = ax[...]
    for blk in range(LRU_BLOCKS):
        c = slice(blk * LANES, (blk + 1) * LANES)
        u = (cw[0:1, c] * xbuf[5:5 + t, c] + cw[1:2, c] * xbuf[6:6 + t, c]
             + cw[2:3, c] * xbuf[7:7 + t, c] + cw[3:4, c] * xbuf[8:8 + t, c]) + cb[:, c]
        ub = u.astype(BF16)
        r = _sigmoid(jnp.dot(ub, wa[blk], preferred_element_type=F32) + ba[:, c])
        ig = _sigmoid(jnp.dot(ub, wx[blk], preferred_element_type=F32) + bx[:, c])
        nl = -lam[:, c]
        sp = jnp.maximum(nl, 0.0) + jnp.log1p(jnp.exp(-jnp.abs(nl)))
        log_a = (-LRU_C * r) * sp
        a = jnp.exp(log_a)
        a_s[:, c] = a
        em1 = jnp.tanh(log_a) * (a * a + 1.0)
        b_s[:, c] = jnp.sqrt(-em1) * (ig * u)
    xbuf[0:8, :] = xbuf[t:t + 8, :]

    row = lax.broadcasted_iota(I32, (8, GROUP_W), 0)

    def group(gi, hprev):
        r0 = pl.multiple_of(gi * 8, 8)
        a = a_s[pl.ds(r0, 8), :]
        b = b_s[pl.ds(r0, 8), :]
        for d in (1, 2, 4):
            ok = row >= d
            b = jnp.where(ok, a * pltpu.roll(b, d, axis=0) + b, b)
            a = jnp.where(ok, a * pltpu.roll(a, d, axis=0), a)
        h = a * hprev + b
        b_s[pl.ds(r0, 8), :] = h
        return jnp.broadcast_to(h[7:8, :], (8, GROUP_W))

    hst[...] = lax.fori_loop(0, t // 8, group, hst[...])

    vbuf[8:8 + t, :] = dc[...] * din[...]
    for blk in range(LRU_BLOCKS):
        c = slice(blk * LANES, (blk + 1) * LANES)
        ya[:, c] = (b_s[:, c] * _silu(ag[:, c])).astype(BF16)
        conv = (scw[0:1, c] * vbuf[6:6 + t, c] + scw[1:2, c] * vbuf[7:7 + t, c]
                + scw[2:3, c] * vbuf[8:8 + t, c])
        yd[:, c] = ((db[:, c] * conv) * _silu(dg[:, c])).astype(BF16)
    vbuf[0:8, :] = vbuf[t:t + 8, :]


def _mix_ad(z, cw, cb, wa, ba, wx, bx, lam, scw):
    nt = SEQ // T_TOK

    def zspec(col):
        return pl.BlockSpec((T_TOK, GROUP_W), lambda b, i: (b * nt + i, col // GROUP_W))

    def full(a):
        return pl.BlockSpec(a.shape, lambda b, i: (0,) * a.ndim)

    small = (cw, cb, wa, ba, wx, bx, lam, scw)
    yspec = pl.BlockSpec((T_TOK, GROUP_W), lambda b, i: (b * nt + i, 0))
    return pl.pallas_call(
        _mix_ad_body,
        grid=(BATCH, nt),
        in_specs=[zspec(COL_A_X), zspec(COL_A_G), zspec(COL_D_IN), zspec(COL_D_B), zspec(COL_D_C),
                  zspec(COL_D_G)] + [full(a) for a in small],
        out_specs=[yspec, yspec],
        out_shape=[jax.ShapeDtypeStruct((BATCH * SEQ, GROUP_W), BF16)] * 2,
        scratch_shapes=[pltpu.VMEM((T_TOK + 8, GROUP_W), F32), pltpu.VMEM((T_TOK + 8, GROUP_W), F32),
                        pltpu.VMEM((T_TOK, GROUP_W), F32), pltpu.VMEM((T_TOK, GROUP_W), F32),
                        pltpu.VMEM((8, GROUP_W), F32)],
        compiler_params=_params(2),
        name="mix_ad",
    )(z, z, z, z, z, z, *small)


def _rms_head(x, gain):
    ms = jnp.mean(x * x, axis=-1, keepdims=True)
    return (x * lax.rsqrt(ms + NORM_EPS)) * gain


def _rope(y, cos, nsin, psin, half):
    return y * cos + pltpu.roll(y, LANES - half, axis=1) * nsin + pltpu.roll(y, half, axis=1) * psin


def _store_vt(dst, g, v):
    for cc in range(T_TOK // KV_TILE):
        dst[0, g, cc] = v[cc * KV_TILE:(cc + 1) * KV_TILE, :].T.astype(BF16)


def _rms_head_mxu(x, gain):
    ssq = jnp.dot((x * x).astype(BF16), jnp.ones((LANES, LANES), BF16), preferred_element_type=F32)
    return (x * lax.rsqrt(ssq * (1.0 / LANES) + NORM_EPS)) * gain


def _prep_body(bq, bkv_s, bkv_w, cq, ckv, ciq, cik, c128, n128, p128, c64, n64, p64,
               qgn, kgn, qgd, kgd,
               qn_o, qr_o, ks_o, vst_o, kw_o, vwt_o, qd_o, kd_o, vdt_o, iq_o, ik_o):
    r128 = (c128[...], n128[...], p128[...], HEAD_DIM // 8)
    r64 = (c64[...], n64[...], p64[...], IDX_DIM // 8)
    for h in range(N_HEADS):
        c = slice(h * LANES, (h + 1) * LANES)
        y = _rms_head_mxu(bq[:, c], qgn[...])
        qn_o[:, c] = (y * Q_PRESCALE).astype(BF16)
        qr_o[:, c] = (_rope(y, *r128) * Q_PRESCALE).astype(BF16)
        qd_o[:, c] = (_rope(_rms_head_mxu(cq[:, c], qgd[...]), *r128) * Q_PRESCALE).astype(BF16)
    for g in range(N_KV):
        ck = slice(g * LANES, (g + 1) * LANES)
        cv = slice((N_KV + g) * LANES, (N_KV + g + 1) * LANES)
        ks_o[0, g] = _rope(_rms_head_mxu(bkv_s[:, ck], kgn[...]), *r128).astype(BF16)
        _store_vt(vst_o, g, bkv_s[:, cv])
        kw_o[0, g] = _rope(_rms_head_mxu(bkv_w[:, ck], kgn[...]), *r128).astype(BF16)
        _store_vt(vwt_o, g, bkv_w[:, cv])
        kd_o[0, g] = _rope(_rms_head_mxu(ckv[:, ck], kgd[...]), *r128).astype(BF16)
        _store_vt(vdt_o, g, ckv[:, cv])

    low = lax.broadcasted_iota(I32, (T_TOK, LANES), 1) < IDX_DIM
    ik = _rope(cik[...], *r64)
    hi = ik.astype(BF16).astype(F32)
    ik_o[0, :, 0:LANES] = (hi + pltpu.roll(ik - hi, IDX_DIM, axis=1)).astype(BF16)
    ik_o[0, :, LANES:2 * LANES] = hi.astype(BF16)
    for v in range(IDX_HEADS // 2):
        x = _rope(ciq[:, v * LANES:(v + 1) * LANES], *r64)
        for half in range(2):
            h = 2 * v + half
            xs = x if half == 0 else pltpu.roll(x, IDX_DIM, axis=1)
            xh = jnp.where(low, xs, 0.0)
            hi = xh.astype(BF16).astype(F32)
            iq_o[:, 2 * h * LANES:(2 * h + 1) * LANES] = (hi + pltpu.roll(hi, IDX_DIM, axis=1)).astype(BF16)
            iq_o[:, (2 * h + 1) * LANES:(2 * h + 2) * LANES] = (xh - hi).astype(BF16)


def _prep(z, tabs128, tabs64, qgn, kgn, qgd, kgd):
    nt = SEQ // T_TOK

    def zspec(col, width):
        return pl.BlockSpec((T_TOK, width), lambda b, i: (b * nt + i, col // width))

    tab = pl.BlockSpec((T_TOK, LANES), lambda b, i: (i, 0))
    gain = pl.BlockSpec((1, LANES), lambda b, i: (0, 0))
    qspec = pl.BlockSpec((T_TOK, GROUP_W), lambda b, i: (b * nt + i, 0))
    kspec = pl.BlockSpec((1, N_KV, T_TOK, LANES), lambda b, i: (b, 0, i, 0))
    vtspec = pl.BlockSpec((1, N_KV, T_TOK // KV_TILE, LANES, KV_TILE), lambda b, i: (b, 0, i, 0, 0))
    qshape = jax.ShapeDtypeStruct((BATCH * SEQ, GROUP_W), BF16)
    kshape = jax.ShapeDtypeStruct((BATCH, N_KV, SEQ, LANES), BF16)
    vtshape = jax.ShapeDtypeStruct((BATCH, N_KV, SEQ // KV_TILE, LANES, KV_TILE), BF16)
    return pl.pallas_call(
        _prep_body,
        grid=(BATCH, nt),
        in_specs=[zspec(COL_B_Q, GROUP_W), zspec(COL_B_KV + 512, 512), zspec(COL_B_KV + 1024, 512),
                  zspec(COL_C_Q, GROUP_W), zspec(COL_C_KV, 512), zspec(COL_C_IQ, 512), zspec(COL_C_IK, LANES)]
                 + [tab] * 6 + [gain] * 4,
        out_specs=[qspec, qspec, kspec, vtspec, kspec, vtspec, qspec, kspec, vtspec,
                   pl.BlockSpec((T_TOK, 2 * LANES * IDX_HEADS), lambda b, i: (b * nt + i, 0)),
                   pl.BlockSpec((1, T_TOK, 2 * LANES), lambda b, i: (b, i, 0))],
        out_shape=[qshape, qshape, kshape, vtshape, kshape, vtshape, qshape, kshape, vtshape,
                   jax.ShapeDtypeStruct((BATCH * SEQ, 2 * LANES * IDX_HEADS), BF16),
                   jax.ShapeDtypeStruct((BATCH, SEQ, 2 * LANES), BF16)],
        compiler_params=_params(2),
        name="attn_prep",
    )(z, z, z, z, z, z, z, *tabs128, *tabs64, qgn, kgn, qgd, kgd)


def _cmp_body(zc, pe, w1, w2, kg, o, ot, kbuf):
    kbuf[0:SEQ, :] = zc[...]
    kbuf[SEQ:SEQ + LANES, :] = jnp.zeros((LANES, LANES), F32)
    acc = jnp.zeros((N_CMP_PAD, LANES), F32)
    for l in range(CMP_LEN):
        rows = kbuf[pl.ds(l, N_CMP_PAD, stride=CMP_STRIDE), :] + pe[0, l:l + 1, :]
        acc = acc + jnp.dot(rows.astype(BF16), w1[0, l], preferred_element_type=F32)
    out = jnp.dot(_silu(acc).astype(BF16), w2[0], preferred_element_type=F32)
    res = jnp.where(pl.program_id(1) == 0, _rms_head(out, kg[...]), out)
    o[0, 0, 0] = res.astype(BF16)
    ot[0, 0, 0] = res.T.astype(BF16)


def _compress(z, pe, w1, w2, kg):
    base = COL_B_KV // LANES
    return pl.pallas_call(
        _cmp_body,
        grid=(BATCH, 2, N_KV),
        in_specs=[pl.BlockSpec((SEQ, LANES), lambda b, kv, g: (b, base + kv * N_KV + g)),
                  pl.BlockSpec((1, CMP_LEN, LANES), lambda b, kv, g: (kv, 0, 0)),
                  pl.BlockSpec((1, CMP_LEN, LANES, LANES), lambda b, kv, g: (kv, 0, 0, 0)),
                  pl.BlockSpec((1, LANES, LANES), lambda b, kv, g: (kv, 0, 0)),
                  pl.BlockSpec((1, LANES), lambda b, kv, g: (0, 0))],
        out_specs=[pl.BlockSpec((1, 1, 1, N_CMP_PAD, LANES), lambda b, kv, g: (b, kv, g, 0, 0)),
                   pl.BlockSpec((1, 1, 1, LANES, N_CMP_PAD), lambda b, kv, g: (b, kv, g, 0, 0))],
        out_shape=[jax.ShapeDtypeStruct((BATCH, 2, N_KV, N_CMP_PAD, LANES), BF16),
                   jax.ShapeDtypeStruct((BATCH, 2, N_KV, LANES, N_CMP_PAD), BF16)],
        scratch_shapes=[pltpu.VMEM((SEQ + LANES, LANES), F32)],
        compiler_params=_params(3),
        name="nsa_compress",
    )(z, pe, w1, w2, kg)


def _key_norm_bound(k_ref):
    def body(c, mx):
        x = k_ref[pl.ds(pl.multiple_of(c * 512, 512), 512), :].astype(F32)
        return jnp.maximum(mx, jnp.max(jnp.sum(x * x, axis=1, keepdims=True), axis=0, keepdims=True))
    return jnp.broadcast_to(lax.fori_loop(0, SEQ // 512, body, jnp.zeros((1, 1), F32)), (1, TQ))


def _query_norms(q_ref):
    head_of_lane = lax.shift_right_logical(lax.broadcasted_iota(I32, (N_HEADS, GROUP_W), 1), 7)
    pick = jnp.where(head_of_lane == lax.broadcasted_iota(I32, (N_HEADS, GROUP_W), 0), 1.0, 0.0).astype(BF16)
    q = q_ref[...].astype(F32)
    return lax.dot_general(pick, (q * q).astype(BF16), NT_DIMS, preferred_element_type=F32)


def _flash_heads(q_ref, qn2, slot0, groups, jlo, jhi, m_s, l_s, acc_s):
    n_heads = sum(len(g[2]) for g in groups)
    lo, hi = slot0, slot0 + n_heads
    l_s[lo:hi, :] = jnp.zeros((n_heads, TQ), F32)
    for slot in range(lo, hi):
        acc_s[slot] = jnp.zeros((HEAD_DIM, TQ), F32)

    bounds = []
    for _, _, cs, _, kmax2 in groups:
        for c in cs:
            h = c.start // LANES
            bounds.append(jnp.sqrt(qn2[h:h + 1, :] * kmax2) * BOUND_SLACK)
    m_bound = jnp.concatenate(bounds, axis=0)

    def scores(j):
        r0 = pl.multiple_of(j * KV_TILE, KV_TILE)
        ss, vts, biases = [], [], {}
        for k_ref, vt_ref, cs, bias_fn, _ in groups:
            k = k_ref[pl.ds(r0, KV_TILE), :]
            vt = vt_ref[j]
            if bias_fn not in biases:
                biases[bias_fn] = bias_fn(j)
            b = biases[bias_fn]
            for c in cs:
                ss.append(lax.dot_general(k, q_ref[:, c], NT_DIMS, preferred_element_type=F32) + b)
                vts.append(vt)
        return ss, vts

    def max_pass(j, m):
        ss, _ = scores(j)
        return jnp.maximum(m, jnp.concatenate([_fold8(s, jnp.maximum) for s in ss], axis=0))

    def exact_max():
        m8 = lax.fori_loop(jlo, jhi, max_pass, jnp.full((8 * n_heads, TQ), NEG, F32))
        m_fin = jnp.concatenate([jnp.max(m8[8 * n:8 * n + 8], axis=0, keepdims=True)
                                 for n in range(n_heads)], axis=0)
        return jnp.maximum(m_fin, M_FLOOR)

    m_s[lo:hi, :] = lax.cond(jnp.max(m_bound) > BOUND_LIMIT, exact_max, lambda: m_bound)

    def tile(j):
        ss, vts = scores(j)
        m_ref = m_s[lo:hi, :]
        ps = [jnp.exp2(s - m_ref[n:n + 1, :]) for n, s in enumerate(ss)]
        l_add = jnp.concatenate([jnp.sum(p, axis=0, keepdims=True) for p in ps], axis=0)
        return l_add, [jnp.dot(vt, p.astype(BF16), preferred_element_type=F32) for vt, p in zip(vts, ps)]

    def one(j, carry):
        l_add, pv = tile(j)
        l_s[lo:hi, :] = l_s[lo:hi, :] + l_add
        for n in range(n_heads):
            acc_s[lo + n] = acc_s[lo + n] + pv[n]
        return carry

    def two(jj, carry):
        la, pa = tile(jlo + 2 * jj)
        lb, pb = tile(jlo + 2 * jj + 1)
        l_s[lo:hi, :] = l_s[lo:hi, :] + (la + lb)
        for n in range(n_heads):
            acc_s[lo + n] = acc_s[lo + n] + (pa[n] + pb[n])
        return carry

    pairs = lax.shift_right_logical(jhi - jlo, 1)
    lax.fori_loop(0, pairs, two, 0)
    lax.fori_loop(jlo + 2 * pairs, jhi, one, 0)


def _flash_out(slot, l_s, acc_s):
    return acc_s[slot] * (1.0 / l_s[slot:slot + 1, :])


def _nsa_body(qn, qr, kc, vct, ks, vst, kw, vwt, gate, bg, ovt, yb, imp_s, bias_s, m_s, l_s, acc_s, kn_s):
    i = pl.program_id(1)

    @pl.when(i == 0)
    def _():
        for g in range(N_KV):
            kn_s[g:g + 1, :] = _key_norm_bound(ks.at[0, g])
            kn_s[N_KV + g:N_KV + g + 1, :] = _key_norm_bound(kw.at[0, g])

    tq = i * TQ + lax.broadcasted_iota(I32, (1, TQ), 1)
    n_kv = i + 1
    sg = _sigmoid(gate[...].T[0:3 * N_HEADS, :])
    cm = (lax.broadcasted_iota(I32, (N_CMP_PAD, TQ), 0) * CMP_STRIDE + (CMP_LEN - 1)) <= tq
    kk = lax.broadcasted_iota(I32, (N_SLC, TQ), 0)
    sub8 = lax.broadcasted_iota(I32, (8, TQ), 0)
    blk_t = lax.shift_right_logical(tq, 6)
    valid = kk <= blk_t
    force = (kk == 0) | (kk == blk_t) | (kk == blk_t - 1)
    rowkv = lax.broadcasted_iota(I32, (KV_TILE, TQ), 0)
    cmp_slot, slc_slot, win_slot = 0, N_HEADS, 2 * N_HEADS
    head_cols = [[slice(h * LANES, (h + 1) * LANES) for h in range(HPG * g, HPG * (g + 1))]
                 for g in range(N_KV)]

    def win_bias(j):
        d = tq - (j * KV_TILE + rowkv)
        return jnp.where((d >= 0) & (d < WINDOW), 0.0, NEG)

    def slc_bias(g):
        return lambda j: bias_s[g, pl.ds(pl.multiple_of(j * KV_TILE, KV_TILE), KV_TILE), :]

    for g in range(N_KV):
        kcg = kc[0, 0, g]
        vcg = vct[0, 0, g]
        cols = head_cols[g]
        ss = [jnp.where(cm, lax.dot_general(kcg, qn[:, c], NT_DIMS, preferred_element_type=F32), NEG)
              for c in cols]
        es = [jnp.exp2(s - jnp.max(s, axis=0, keepdims=True)) for s in ss]
        ps = [jnp.where(cm, e * (1.0 / jnp.sum(e, axis=0, keepdims=True)), 0.0) for e in es]
        for hh, p in enumerate(ps):
            acc_s[cmp_slot + HPG * g + hh] = jnp.dot(vcg, p.astype(BF16), preferred_element_type=F32)
        psum = (ps[0] + ps[1]) + (ps[2] + ps[3])

        p_hi = psum.astype(BF16)
        p_lo = (psum - p_hi.astype(F32)).astype(BF16)
        imp = (jnp.dot(ovt[...], p_hi, preferred_element_type=F32)
               + jnp.dot(ovt[...], p_lo, preferred_element_type=F32))
        imp = jnp.where(force, FORCE, jnp.where(valid, imp, -FORCE))
        imp_s[...] = imp
        bands = [imp[8 * r:8 * r + 8] for r in range(N_SLC // 8)]
        ranks = [jnp.zeros((8, TQ), F32) for _ in bands]
        for j in range(N_SLC):
            vj = imp_s[j:j + 1, :]
            for r, x in enumerate(bands):
                if r < j // 8:
                    hit = vj > x
                elif r > j // 8:
                    hit = vj >= x
                else:
                    hit = (vj > x) | ((vj == x) & (sub8 > j % 8))
                ranks[r] = ranks[r] + jnp.where(hit, 1.0, 0.0)
        imp_s[...] = jnp.where(jnp.concatenate(ranks, axis=0) < SLC_TOPN, 0.0, NEG)

        def fill(j, carry):
            r0 = pl.multiple_of(j * KV_TILE, KV_TILE)
            nb = KV_TILE // SLC_BLK
            tile = jnp.concatenate([jnp.broadcast_to(imp_s[pl.ds(nb * j + u, 1), :], (SLC_BLK, TQ))
                                    for u in range(nb)], axis=0)
            bias_s[g, pl.ds(r0, KV_TILE), :] = jnp.where(r0 + rowkv <= tq, tile, NEG)
            return carry
        lax.fori_loop(0, n_kv, fill, 0)

    qn2 = _query_norms(qr)
    _flash_heads(qr, qn2, slc_slot, [(ks.at[0, g], vst.at[0, g], head_cols[g], slc_bias(g), kn_s[g:g + 1, :])
                                for g in range(N_KV)], 0, n_kv, m_s, l_s, acc_s)
    _flash_heads(qr, qn2, win_slot, [(kw.at[0, g], vwt.at[0, g], head_cols[g], win_bias,
                                 kn_s[N_KV + g:N_KV + g + 1, :]) for g in range(N_KV)],
                 jnp.maximum(n_kv - 1 - WINDOW // KV_TILE, 0), n_kv, m_s, l_s, acc_s)
    for h in range(N_HEADS):
        c = slice(h * LANES, (h + 1) * LANES)
        tot = (sg[h:h + 1] * acc_s[cmp_slot + h]
               + sg[N_HEADS + h:N_HEADS + h + 1] * _flash_out(slc_slot + h, l_s, acc_s)
               + sg[2 * N_HEADS + h:2 * N_HEADS + h + 1] * _flash_out(win_slot + h, l_s, acc_s))
        yb[:, c] = (tot.T * _silu(bg[:, c])).astype(BF16)


def _nsa(z, qn, qr, cmp_, cmpt, ks, vst, kw, vwt, ovt):
    nq = SEQ // TQ
    qspec = pl.BlockSpec((TQ, GROUP_W), lambda b, i: (b * nq + i, 0))
    kspec = pl.BlockSpec((1, N_KV, SEQ, LANES), lambda b, i: (b, 0, 0, 0))
    vtspec = pl.BlockSpec((1, N_KV, SEQ // KV_TILE, LANES, KV_TILE), lambda b, i: (b, 0, 0, 0, 0))
    return pl.pallas_call(
        _nsa_body,
        grid=(BATCH, nq),
        in_specs=[qspec, qspec,
                  pl.BlockSpec((1, 1, N_KV, N_CMP_PAD, LANES), lambda b, i: (b, 0, 0, 0, 0)),
                  pl.BlockSpec((1, 1, N_KV, LANES, N_CMP_PAD), lambda b, i: (b, 1, 0, 0, 0)),
                  kspec, vtspec, kspec, vtspec,
                  pl.BlockSpec((TQ, LANES), lambda b, i: (b * nq + i, COL_B_GATE // LANES)),
                  pl.BlockSpec((TQ, GROUP_W), lambda b, i: (b * nq + i, COL_B_G // GROUP_W)),
                  pl.BlockSpec((N_SLC, N_CMP_PAD), lambda b, i: (0, 0))],
        out_specs=qspec,
        out_shape=jax.ShapeDtypeStruct((BATCH * SEQ, GROUP_W), BF16),
        scratch_shapes=[pltpu.VMEM((N_SLC, TQ), F32), pltpu.VMEM((N_KV, SEQ, TQ), F32),
                        pltpu.VMEM((3 * N_HEADS, TQ), F32), pltpu.VMEM((3 * N_HEADS, TQ), F32),
                        pltpu.VMEM((3 * N_HEADS, HEAD_DIM, TQ), F32), pltpu.VMEM((8, TQ), F32)],
        compiler_params=_params(2),
        name="nsa_attn",
    )(qn, qr, cmp_, cmpt, ks, vst, kw, vwt, z, z, ovt)


def _dsa_body(qd, kd, vdt, iq, ik, iw, cg, yc, sc_s, m_s, l_s, acc_s, kn_s):
    i = pl.program_id(1)

    @pl.when(i == 0)
    def _():
        for g in range(N_KV):
            kn_s[g:g + 1, :] = _key_norm_bound(kd.at[0, g])

    tq = i * TQ + lax.broadcasted_iota(I32, (1, TQ), 1)
    n_kv = i + 1
    iwt = iw[...].T[0:IDX_HEADS, :] * IW_SCALE
    rowkv = lax.broadcasted_iota(I32, (KV_TILE, TQ), 0)
    kf = float(DSA_TOPK)

    def score_tile(j):
        r0 = pl.multiple_of(j * KV_TILE, KV_TILE)
        ikc = ik[0, pl.ds(r0, KV_TILE), :]
        sc = jnp.zeros((KV_TILE, TQ), F32)
        for h in range(IDX_HEADS):
            r = lax.dot_general(ikc, iq[:, 2 * h * LANES:(2 * h + 2) * LANES], NT_DIMS,
                                preferred_element_type=F32)
            sc = sc + jnp.maximum(r, 0.0) * iwt[h:h + 1, :]
        return r0, sc

    def fill(j, carry):
        mn, mx = carry
        r0, sc = score_tile(j)
        sc_s[pl.ds(r0, KV_TILE), :] = sc
        return (jnp.minimum(mn, _fold8(sc, jnp.minimum)), jnp.maximum(mx, _fold8(sc, jnp.maximum)))

    def fill2(jj, carry):
        return fill(2 * jj + 1, fill(2 * jj, carry))

    half = lax.shift_right_logical(i, 1)
    mm = lax.fori_loop(0, half, fill2, (jnp.full((8, TQ), jnp.inf, F32), jnp.full((8, TQ), -jnp.inf, F32)))
    mn8, mx8 = lax.fori_loop(2 * half, i, fill, mm)
    r0, sc = score_tile(i)
    ok = r0 + rowkv <= tq
    sc_s[pl.ds(r0, KV_TILE), :] = jnp.where(ok, sc, -jnp.inf)
    lo0 = jnp.min(jnp.minimum(mn8, _fold8(jnp.where(ok, sc, jnp.inf), jnp.minimum)), axis=0, keepdims=True)
    hi0 = jnp.max(jnp.maximum(mx8, _fold8(jnp.where(ok, sc, -jnp.inf), jnp.maximum)), axis=0, keepdims=True)

    def sweep(fn, n_out):
        def body(c, accs):
            r0 = pl.multiple_of(c * KV_TILE, KV_TILE)
            inds = fn(sc_s[pl.ds(r0, KV_TILE), :], r0 + rowkv)
            return tuple(a + _fold8(jnp.where(ind, 1.0, 0.0)) for a, ind in zip(accs, inds))
        accs = lax.fori_loop(0, n_kv, body, tuple(jnp.zeros((8, TQ), F32) for _ in range(n_out)))
        return tuple(jnp.sum(a, axis=0, keepdims=True) for a in accs)

    def data_span(lo, hi):
        def body(c, carry):
            mn, mx = carry
            x = sc_s[pl.ds(pl.multiple_of(c * KV_TILE, KV_TILE), KV_TILE), :]
            mn = jnp.minimum(mn, _fold8(jnp.where(x >= lo, x, jnp.inf), jnp.minimum))
            mx = jnp.maximum(mx, _fold8(jnp.where(x <= hi, x, -jnp.inf), jnp.maximum))
            return mn, mx
        mn, mx = lax.fori_loop(0, n_kv, body, (jnp.full((8, TQ), jnp.inf, F32), jnp.full((8, TQ), -jnp.inf, F32)))
        return jnp.min(mn, axis=0, keepdims=True), jnp.max(mx, axis=0, keepdims=True)

    nvalid = (tq + 1).astype(F32)
    act0 = jnp.where(nvalid > kf, 1.0, 0.0)

    def bisect(_, c):
        lo, hi, clo, act = c
        on = act > 0.0
        mid = lo + (hi - lo) * 0.5
        exh = (mid <= lo) | (mid >= hi)
        p = jnp.where(exh, hi, mid)
        (cnt,) = sweep(lambda x, srow: (x >= p,), 1)
        up = on & (cnt >= kf)
        dn = on & (cnt < kf)
        clo = jnp.where(up, cnt, clo)
        act = jnp.where(on & (clo != kf) & jnp.logical_not(exh), 1.0, 0.0)
        return jnp.where(up, p, lo), jnp.where(dn, p, hi), clo, act

    def snap(c):
        lo, hi, clo, act = c
        on = act > 0.0
        dmin, dmax = data_span(lo, hi)
        act = jnp.where(on & (dmin < dmax), 1.0, 0.0)
        return jnp.where(on, dmin, lo), jnp.where(on, dmax, hi), clo, act

    def cond(c):
        return c[0] > 0.0

    def round_(c):
        state = lax.fori_loop(0, SNAP_EVERY - 1, bisect, snap(c[1:]))
        return (jnp.sum(state[3]),) + state

    state = lax.fori_loop(0, SNAP_FIRST, bisect, (lo0, hi0, nvalid, act0))
    _, thr, _, _, _ = lax.while_loop(cond, round_, (jnp.sum(state[3]),) + state)

    cnt_g, cnt_e = sweep(lambda x, srow: (x > thr, x == thr), 2)
    need = kf - cnt_g
    excess = (cnt_g + cnt_e) > kf

    def tie_break():
        def it(_, c):
            jl, jh = c
            jm = lax.shift_right_arithmetic(jl + jh, 1)
            (cnt,) = sweep(lambda x, srow: ((x == thr) & (srow <= jm),), 1)
            ge = cnt >= need
            return jnp.where(ge, jl, jm), jnp.where(ge, jm, jh)
        _, jh = lax.fori_loop(0, 12, it, (jnp.full((1, TQ), -1, I32), jnp.full((1, TQ), SEQ - 1, I32)))
        return jnp.where(excess, jh, SEQ)

    last = lax.cond(jnp.sum(jnp.where(excess, 1.0, 0.0)) > 0.0, tie_break,
                    lambda: jnp.full((1, TQ), SEQ, I32))

    def to_bias(c, carry):
        r0 = pl.multiple_of(c * KV_TILE, KV_TILE)
        x = sc_s[pl.ds(r0, KV_TILE), :]
        sel = (x > thr) | ((x == thr) & (r0 + rowkv <= last))
        sc_s[pl.ds(r0, KV_TILE), :] = jnp.where(sel, 0.0, NEG)
        return carry
    lax.fori_loop(0, n_kv, to_bias, 0)

    def bias(j):
        return sc_s[pl.ds(pl.multiple_of(j * KV_TILE, KV_TILE), KV_TILE), :]

    groups = [(kd.at[0, g], vdt.at[0, g], [slice(h * LANES, (h + 1) * LANES)
                                           for h in range(HPG * g, HPG * (g + 1))], bias, kn_s[g:g + 1, :])
              for g in range(N_KV)]
    _flash_heads(qd, _query_norms(qd), 0, groups, 0, n_kv, m_s, l_s, acc_s)
    for h in range(N_HEADS):
        c = slice(h * LANES, (h + 1) * LANES)
        yc[:, c] = (_flash_out(h, l_s, acc_s).T * _silu(cg[:, c])).astype(BF16)


def _dsa(z, qd, kd, vdt, iq, ik):
    nq = SEQ // TQ
    qspec = pl.BlockSpec((TQ, GROUP_W), lambda b, i: (b * nq + i, 0))
    return pl.pallas_call(
        _dsa_body,
        grid=(BATCH, nq),
        in_specs=[qspec,
                  pl.BlockSpec((1, N_KV, SEQ, LANES), lambda b, i: (b, 0, 0, 0)),
                  pl.BlockSpec((1, N_KV, SEQ // KV_TILE, LANES, KV_TILE), lambda b, i: (b, 0, 0, 0, 0)),
                  pl.BlockSpec((TQ, 2 * LANES * IDX_HEADS), lambda b, i: (b * nq + i, 0)),
                  pl.BlockSpec((1, SEQ, 2 * LANES), lambda b, i: (b, 0, 0)),
                  pl.BlockSpec((TQ, LANES), lambda b, i: (b * nq + i, COL_C_IW // LANES)),
                  pl.BlockSpec((TQ, GROUP_W), lambda b, i: (b * nq + i, COL_C_G // GROUP_W))],
        out_specs=qspec,
        out_shape=jax.ShapeDtypeStruct((BATCH * SEQ, GROUP_W), BF16),
        scratch_shapes=[pltpu.VMEM((SEQ, TQ), F32), pltpu.VMEM((N_HEADS, TQ), F32),
                        pltpu.VMEM((N_HEADS, TQ), F32), pltpu.VMEM((N_HEADS, HEAD_DIM, TQ), F32),
                        pltpu.VMEM((8, TQ), F32)],
        compiler_params=_params(2),
        name="dsa_attn",
    )(qd, kd, vdt, iq, ik, z, z)


def _segment_moves():
    src = {}
    o = 0
    names = ("a_x", "a_g", "b_q", "b_g", "b_kv", "b_gate", "c_q", "c_g", "c_kv", "c_iq", "c_ik", "c_iw",
             "d_in", "d_b", "d_c", "d_g")
    for n, s in zip(names, SPLIT_SIZES):
        src[n] = (o, s)
        o += s
    dst = dict(a_x=COL_A_X, a_g=COL_A_G, b_q=COL_B_Q, b_g=COL_B_G, c_q=COL_C_Q, c_g=COL_C_G,
               d_in=COL_D_IN, d_b=COL_D_B, d_c=COL_D_C, d_g=COL_D_G, b_kv=COL_B_KV, c_kv=COL_C_KV,
               c_iq=COL_C_IQ, b_gate=COL_B_GATE, c_ik=COL_C_IK, c_iw=COL_C_IW)
    return [(src[n][0], dst[n], src[n][1]) for n in names]


def _wprep_body(w_ref, o_ref):
    o_ref[COL_B_GATE:PROJ_W, :] = jnp.zeros((PROJ_W - COL_B_GATE, TR_W), BF16)
    for s, d, width in _segment_moves():
        for c in range(0, width, GROUP_W):
            n = min(GROUP_W, width - c)
            o_ref[d + c:d + c + n, :] = w_ref[s + c:s + c + n, :].astype(BF16)


def _proj_weight(wt, layer):
    _, n, k = wt.shape
    return pl.pallas_call(
        _wprep_body,
        grid=(k // TR_W,),
        in_specs=[pl.BlockSpec((None, n, TR_W), lambda i: (layer, 0, i))],
        out_specs=pl.BlockSpec((PROJ_W, TR_W), lambda i: (0, i)),
        out_shape=jax.ShapeDtypeStruct((PROJ_W, k), BF16),
        compiler_params=_params(1),
        name="w_in_layout",
    )(wt)


def _rope_tables(head_dim):
    r = head_dim // 4
    half = r // 2
    pos = jnp.arange(SEQ, dtype=jnp.int32)
    inv = ROPE_THETA ** (-jnp.arange(half, dtype=F32) * 2.0 / r)
    ang = pos.astype(F32)[:, None] * inv[None, :]
    cos, sin = jnp.cos(ang), jnp.sin(ang)
    one = jnp.ones((SEQ, head_dim - r), F32)
    zero = jnp.zeros((SEQ, head_dim - r), F32)
    zh = jnp.zeros((SEQ, half), F32)
    reps = LANES // head_dim
    return tuple(jnp.tile(t, (1, reps)) for t in (
        jnp.concatenate([cos, cos, one], axis=1),
        jnp.concatenate([-sin, zh, zero], axis=1),
        jnp.concatenate([zh, sin, zero], axis=1)))


def _overlap_t():
    cs = jnp.arange(N_CMP_PAD) * CMP_STRIDE
    ss = jnp.arange(N_SLC) * SLC_BLK
    ov = jnp.clip(jnp.minimum(cs[None, :] + CMP_LEN, ss[:, None] + SLC_BLK)
                  - jnp.maximum(cs[None, :], ss[:, None]), 0, None).astype(F32) / CMP_LEN
    return jnp.where(jnp.arange(N_CMP_PAD)[None, :] < N_CMP_PAD - 1, ov, 0.0).astype(BF16)


def kernel(x, norm_g, w_in, w_out, lru_conv_w, lru_conv_b, lru_wa, lru_ba, lru_wx, lru_bx, lru_lambda,
           nsa_q_gain, nsa_k_gain, cmp_pe_k, cmp_w1_k, cmp_w2_k, cmp_pe_v, cmp_w1_v, cmp_w2_v,
           dsa_q_gain, dsa_k_gain, sc_conv_w):
    b, s, d = x.shape
    assert (b, s, d) == (BATCH, SEQ, D_MODEL)
    tabs128 = _rope_tables(HEAD_DIM)
    tabs64 = _rope_tables(IDX_DIM)
    ovt = _overlap_t()
    x2 = x.reshape(b * s, d)
    w_in_t = jnp.swapaxes(w_in, 1, 2)

    def row(v):
        return v.reshape(1, -1)

    for l in range(DEPTH):
        z = _inproj(x2, row(norm_g[l]), _proj_weight(w_in_t, l))
        ya, yd = _mix_ad(z, lru_conv_w[l], row(lru_conv_b[l]), lru_wa[l].astype(BF16), row(lru_ba[l]),
                         lru_wx[l].astype(BF16), row(lru_bx[l]), row(lru_lambda[l]), sc_conv_w[l])
        qn, qr, ks, vst, kw, vwt, qd, kd, vdt, iq, ik = _prep(
            z, tabs128, tabs64, row(nsa_q_gain[l]), row(nsa_k_gain[l]), row(dsa_q_gain[l]), row(dsa_k_gain[l]))
        cmp_, cmpt = _compress(z, jnp.stack([cmp_pe_k[l], cmp_pe_v[l]]),
                               jnp.stack([cmp_w1_k[l], cmp_w1_v[l]]).astype(BF16),
                               jnp.stack([cmp_w2_k[l], cmp_w2_v[l]]).astype(BF16), row(nsa_k_gain[l]))
        yb = _nsa(z, qn, qr, cmp_, cmpt, ks, vst, kw, vwt, ovt)
        yc = _dsa(z, qd, kd, vdt, iq, ik)
        x2 = _outproj((ya, yb, yc, yd), w_out, l, x2)
    return x2.reshape(b, s, d)
```

```python
import functools

import jax
import jax.numpy as jnp
from jax import lax
from jax.experimental import pallas as pl
from jax.experimental.pallas import tpu as pltpu

F32 = jnp.float32
BF16 = jnp.bfloat16
I32 = jnp.int32

D_MODEL = 4096
BATCH = 2
SEQ = 4096
DEPTH = 2
GROUP_W = D_MODEL // 4
HEAD_DIM = 128
N_HEADS = GROUP_W // HEAD_DIM
N_KV = N_HEADS // 4
HPG = N_HEADS // N_KV
ROPE_THETA = 500000.0
NORM_EPS = 1e-6
NEG = -1e30
FORCE = 1e6
LRU_BLOCKS = N_HEADS
LRU_C = 8.0
CONV_A = 4
CONV_D = 3
CMP_LEN = 32
CMP_STRIDE = 16
SLC_BLK = 64
SLC_TOPN = 16
WINDOW = 512
IDX_HEADS = 8
IDX_DIM = 64
DSA_TOPK = 256
SCALE = HEAD_DIM ** -0.5
Q_PRESCALE = SCALE * 1.4426950408889634
IW_SCALE = IDX_HEADS ** -0.5 * IDX_DIM ** -0.5

SPLIT_SIZES = (
    GROUP_W, GROUP_W,
    N_HEADS * HEAD_DIM, GROUP_W, 6 * N_KV * HEAD_DIM, 3 * N_HEADS,
    N_HEADS * HEAD_DIM, GROUP_W, 2 * N_KV * HEAD_DIM,
    IDX_HEADS * IDX_DIM, IDX_DIM, IDX_HEADS,
    GROUP_W, GROUP_W, GROUP_W, GROUP_W,
)

LANES = 128
COL_A_X, COL_A_G, COL_B_Q, COL_B_G, COL_C_Q, COL_C_G = 0, 1024, 2048, 3072, 4096, 5120
COL_D_IN, COL_D_B, COL_D_C, COL_D_G = 6144, 7168, 8192, 9216
COL_B_KV, COL_C_KV, COL_C_IQ = 10240, 11776, 12288
COL_B_GATE, COL_C_IK, COL_C_IW = 12800, 12928, 13056
PROJ_W = 13312

N_CMP_PAD = SEQ // CMP_STRIDE
N_SLC = SEQ // SLC_BLK
KV_TILE = 256
M_FLOOR = -1e20
BOUND_SLACK = 1.01
BOUND_LIMIT = 60.0
SNAP_FIRST, SNAP_EVERY = 8, 4

VMEM_LIMIT_BYTES = 56 * 1024 * 1024
INPROJ_VMEM_LIMIT_BYTES = 60 * 1024 * 1024

TM_IN, TN_IN = 1024, 1024
TM_OUT, TN_OUT = 1024, 512
T_TOK = 256
TR_W = 128
TQ = 256

NT_DIMS = (((1,), (1,)), ((), ()))


def _params(n_axes, vmem_limit_bytes=VMEM_LIMIT_BYTES):
    return pltpu.CompilerParams(dimension_semantics=("arbitrary",) * n_axes,
                                vmem_limit_bytes=vmem_limit_bytes)


def _sigmoid(x):
    return 0.5 * jnp.tanh(0.5 * x) + 0.5


def _silu(x):
    return x * _sigmoid(x)


def _fold8(x, op=jnp.add):
    parts = [x[r:r + 8] for r in range(0, x.shape[0], 8)]
    while len(parts) > 1:
        nxt = [op(parts[a], parts[a + 1]) for a in range(0, len(parts) - 1, 2)]
        if len(parts) % 2:
            nxt.append(parts[-1])
        parts = nxt
    return parts[0]


def _inproj_body(x_hbm, g_ref, w_ref, o_ref, x_buf, h_ref, sem):
    i = pl.program_id(0)
    n_row_tiles = pl.num_programs(0)

    def x_copy(tile):
        return pltpu.make_async_copy(x_hbm.at[pl.ds(pl.multiple_of(tile * TM_IN, TM_IN), TM_IN), :], x_buf, sem)

    @pl.when(pl.program_id(1) == 0)
    def _():
        @pl.when(i == 0)
        def _():
            x_copy(0).start()

        x_copy(i).wait()

        def rows(r, carry):
            r0 = pl.multiple_of(r * 64, 64)
            x = x_buf[pl.ds(r0, 64), :]
            ms = jnp.mean(x * x, axis=-1, keepdims=True)
            h_ref[pl.ds(r0, 64), :] = ((x * lax.rsqrt(ms + NORM_EPS)) * g_ref[...]).astype(BF16)
            return carry
        lax.fori_loop(0, TM_IN // 64, rows, 0)

        @pl.when(i + 1 < n_row_tiles)
        def _():
            x_copy(i + 1).start()

    o_ref[...] = lax.dot_general(h_ref[...], w_ref[...], NT_DIMS, preferred_element_type=F32)


def _inproj(x2, g, wt):
    m = x2.shape[0]
    return pl.pallas_call(
        _inproj_body,
        grid=(m // TM_IN, PROJ_W // TN_IN),
        in_specs=[pl.BlockSpec(memory_space=pl.ANY),
                  pl.BlockSpec((1, D_MODEL), lambda i, j: (0, 0)),
                  pl.BlockSpec((TN_IN, D_MODEL), lambda i, j: (j, 0))],
        out_specs=pl.BlockSpec((TM_IN, TN_IN), lambda i, j: (i, j)),
        out_shape=jax.ShapeDtypeStruct((m, PROJ_W), F32),
        scratch_shapes=[pltpu.VMEM((TM_IN, D_MODEL), F32), pltpu.VMEM((TM_IN, D_MODEL), BF16),
                        pltpu.SemaphoreType.DMA(())],
        compiler_params=_params(2, INPROJ_VMEM_LIMIT_BYTES),
        name="inproj",
    )(x2, g, wt)


def _outproj_body(ya, yb, yc, yd, wa, wb, wc, wd, x_ref, o_ref):
    acc = jnp.dot(ya[...], wa[...].astype(BF16), preferred_element_type=F32)
    acc = acc + jnp.dot(yb[...], wb[...].astype(BF16), preferred_element_type=F32)
    acc = acc + jnp.dot(yc[...], wc[...].astype(BF16), preferred_element_type=F32)
    acc = acc + jnp.dot(yd[...], wd[...].astype(BF16), preferred_element_type=F32)
    o_ref[...] = x_ref[...] + acc


def _outproj(ys, w, layer, x2):
    m = x2.shape[0]
    yspec = pl.BlockSpec((TM_OUT, GROUP_W), lambda i, j: (i, 0))
    wspecs = [pl.BlockSpec((None, GROUP_W, TN_OUT), functools.partial(lambda i, j, k: (layer, k, j), k=k))
              for k in range(4)]
    xspec = pl.BlockSpec((TM_OUT, TN_OUT), lambda i, j: (i, j))
    return pl.pallas_call(
        _outproj_body,
        grid=(m // TM_OUT, D_MODEL // TN_OUT),
        in_specs=[yspec] * 4 + wspecs + [xspec],
        out_specs=xspec,
        out_shape=jax.ShapeDtypeStruct((m, D_MODEL), F32),
        compiler_params=_params(2),
        name="outproj",
    )(*ys, w, w, w, w, x2)


def _mix_ad_body(ax, ag, din, db, dc, dg, cw, cb, wa, ba, wx, bx, lam, scw,
                 ya, yd, xbuf, vbuf, a_s, b_s, hst):
    t = T_TOK

    @pl.when(pl.program_id(1) == 0)
    def _():
        xbuf[0:8, :] = jnp.zeros((8, GROUP_W), F32)
        vbuf[0:8, :] = jnp.zeros((8, GROUP_W), F32)
        hst[...] = jnp.zeros((8, GROUP_W), F32)

    xbuf[8:8 + t, :] = ax[...]
    for blk in range(LRU_BLOCKS):
        c = slice(blk * LANES, (blk + 1) * LANES)
        u = (cw[0:1, c] * xbuf[5:5 + t, c] + cw[1:2, c] * xbuf[6:6 + t, c]
             + cw[2:3, c] * xbuf[7:7 + t, c] + cw[3:4, c] * xbuf[8:8 + t, c]) + cb[:, c]
        ub = u.astype(BF16)
        r = _sigmoid(jnp.dot(ub, wa[blk], preferred_element_type=F32) + ba[:, c])
        ig = _sigmoid(jnp.dot(ub, wx[blk], preferred_element_type=F32) + bx[:, c])
        nl = -lam[:, c]
        sp = jnp.maximum(nl, 0.0) + jnp.log1p(jnp.exp(-jnp.abs(nl)))
        log_a = (-LRU_C * r) * sp
        a = jnp.exp(log_a)
        a_s[:, c] = a
        em1 = jnp.tanh(log_a) * (a * a + 1.0)
        b_s[:, c] = jnp.sqrt(-em1) * (ig * u)
    xbuf[0:8, :] = xbuf[t:t + 8, :]

    row = lax.broadcasted_iota(I32, (8, GROUP_W), 0)

    def group(gi, hprev):
        r0 = pl.multiple_of(gi * 8, 8)
        a = a_s[pl.ds(r0, 8), :]
        b = b_s[pl.ds(r0, 8), :]
        for d in (1, 2, 4):
            ok = row >= d
            b = jnp.where(ok, a * pltpu.roll(b, d, axis=0) + b, b)
            a = jnp.where(ok, a * pltpu.roll(a, d, axis=0), a)
        h = a * hprev + b
        b_s[pl.ds(r0, 8), :] = h
        return jnp.broadcast_to(h[7:8, :], (8, GROUP_W))

    hst[...] = lax.fori_loop(0, t // 8, group, hst[...])

    vbuf[8:8 + t, :] = dc[...] * din[...]
    for blk in range(LRU_BLOCKS):
        c = slice(blk * LANES, (blk + 1) * LANES)
        ya[:, c] = (b_s[:, c] * _silu(ag[:, c])).astype(BF16)
        conv = (scw[0:1, c] * vbuf[6:6 + t, c] + scw[1:2, c] * vbuf[7:7 + t, c]
                + scw[2:3, c] * vbuf[8:8 + t, c])
        yd[:, c] = ((db[:, c] * conv) * _silu(dg[:, c])).astype(BF16)
    vbuf[0:8, :] = vbuf[t:t + 8, :]


def _mix_ad(z, cw, cb, wa, ba, wx, bx, lam, scw):
    nt = SEQ // T_TOK

    def zspec(col):
        return pl.BlockSpec((T_TOK, GROUP_W), lambda b, i: (b * nt + i, col // GROUP_W))

    def full(a):
        return pl.BlockSpec(a.shape, lambda b, i: (0,) * a.ndim)

    small = (cw, cb, wa, ba, wx, bx, lam, scw)
    yspec = pl.BlockSpec((T_TOK, GROUP_W), lambda b, i: (b * nt + i, 0))
    return pl.pallas_call(
        _mix_ad_body,
        grid=(BATCH, nt),
        in_specs=[zspec(COL_A_X), zspec(COL_A_G), zspec(COL_D_IN), zspec(COL_D_B), zspec(COL_D_C),
                  zspec(COL_D_G)] + [full(a) for a in small],
        out_specs=[yspec, yspec],
        out_shape=[jax.ShapeDtypeStruct((BATCH * SEQ, GROUP_W), BF16)] * 2,
        scratch_shapes=[pltpu.VMEM((T_TOK + 8, GROUP_W), F32), pltpu.VMEM((T_TOK + 8, GROUP_W), F32),
                        pltpu.VMEM((T_TOK, GROUP_W), F32), pltpu.VMEM((T_TOK, GROUP_W), F32),
                        pltpu.VMEM((8, GROUP_W), F32)],
        compiler_params=_params(2),
        name="mix_ad",
    )(z, z, z, z, z, z, *small)


def _rms_head(x, gain):
    ms = jnp.mean(x * x, axis=-1, keepdims=True)
    return (x * lax.rsqrt(ms + NORM_EPS)) * gain


def _rope(y, cos, nsin, psin, half):
    return y * cos + pltpu.roll(y, LANES - half, axis=1) * nsin + pltpu.roll(y, half, axis=1) * psin


def _store_vt(dst, g, v):
    for cc in range(T_TOK // KV_TILE):
        dst[0, g, cc] = v[cc * KV_TILE:(cc + 1) * KV_TILE, :].T.astype(BF16)


def _rms_head_mxu(x, gain):
    ssq = jnp.dot((x * x).astype(BF16), jnp.ones((LANES, LANES), BF16), preferred_element_type=F32)
    return (x * lax.rsqrt(ssq * (1.0 / LANES) + NORM_EPS)) * gain


def _prep_body(bq, bkv_s, bkv_w, cq, ckv, ciq, cik, c128, n128, p128, c64, n64, p64,
               qgn, kgn, qgd, kgd,
               qn_o, qr_o, ks_o, vst_o, kw_o, vwt_o, qd_o, kd_o, vdt_o, iq_o, ik_o):
    r128 = (c128[...], n128[...], p128[...], HEAD_DIM // 8)
    r64 = (c64[...], n64[...], p64[...], IDX_DIM // 8)
    for h in range(N_HEADS):
        c = slice(h * LANES, (h + 1) * LANES)
        y = _rms_head_mxu(bq[:, c], qgn[...])
        qn_o[:, c] = (y * Q_PRESCALE).astype(BF16)
        qr_o[:, c] = (_rope(y, *r128) * Q_PRESCALE).astype(BF16)
        qd_o[:, c] = (_rope(_rms_head_mxu(cq[:, c], qgd[...]), *r128) * Q_PRESCALE).astype(BF16)
    for g in range(N_KV):
        ck = slice(g * LANES, (g + 1) * LANES)
        cv = slice((N_KV + g) * LANES, (N_KV + g + 1) * LANES)
        ks_o[0, g] = _rope(_rms_head_mxu(bkv_s[:, ck], kgn[...]), *r128).astype(BF16)
        _store_vt(vst_o, g, bkv_s[:, cv])
        kw_o[0, g] = _rope(_rms_head_mxu(bkv_w[:, ck], kgn[...]), *r128).astype(BF16)
        _store_vt(vwt_o, g, bkv_w[:, cv])
        kd_o[0, g] = _rope(_rms_head_mxu(ckv[:, ck], kgd[...]), *r128).astype(BF16)
        _store_vt(vdt_o, g, ckv[:, cv])

    low = lax.broadcasted_iota(I32, (T_TOK, LANES), 1) < IDX_DIM
    ik = _rope(cik[...], *r64)
    hi = ik.astype(BF16).astype(F32)
    ik_o[0, :, 0:LANES] = (hi + pltpu.roll(ik - hi, IDX_DIM, axis=1)).astype(BF16)
    ik_o[0, :, LANES:2 * LANES] = hi.astype(BF16)
    for v in range(IDX_HEADS // 2):
        x = _rope(ciq[:, v * LANES:(v + 1) * LANES], *r64)
        for half in range(2):
            h = 2 * v + half
            xs = x if half == 0 else pltpu.roll(x, IDX_DIM, axis=1)
            xh = jnp.where(low, xs, 0.0)
            hi = xh.astype(BF16).astype(F32)
            iq_o[:, 2 * h * LANES:(2 * h + 1) * LANES] = (hi + pltpu.roll(hi, IDX_DIM, axis=1)).astype(BF16)
            iq_o[:, (2 * h + 1) * LANES:(2 * h + 2) * LANES] = (xh - hi).astype(BF16)


def _prep(z, tabs128, tabs64, qgn, kgn, qgd, kgd):
    nt = SEQ // T_TOK

    def zspec(col, width):
        return pl.BlockSpec((T_TOK, width), lambda b, i: (b * nt + i, col // width))

    tab = pl.BlockSpec((T_TOK, LANES), lambda b, i: (i, 0))
    gain = pl.BlockSpec((1, LANES), lambda b, i: (0, 0))
    qspec = pl.BlockSpec((T_TOK, GROUP_W), lambda b, i: (b * nt + i, 0))
    kspec = pl.BlockSpec((1, N_KV, T_TOK, LANES), lambda b, i: (b, 0, i, 0))
    vtspec = pl.BlockSpec((1, N_KV, T_TOK // KV_TILE, LANES, KV_TILE), lambda b, i: (b, 0, i, 0, 0))
    qshape = jax.ShapeDtypeStruct((BATCH * SEQ, GROUP_W), BF16)
    kshape = jax.ShapeDtypeStruct((BATCH, N_KV, SEQ, LANES), BF16)
    vtshape = jax.ShapeDtypeStruct((BATCH, N_KV, SEQ // KV_TILE, LANES, KV_TILE), BF16)
    return pl.pallas_call(
        _prep_body,
        grid=(BATCH, nt),
        in_specs=[zspec(COL_B_Q, GROUP_W), zspec(COL_B_KV + 512, 512), zspec(COL_B_KV + 1024, 512),
                  zspec(COL_C_Q, GROUP_W), zspec(COL_C_KV, 512), zspec(COL_C_IQ, 512), zspec(COL_C_IK, LANES)]
                 + [tab] * 6 + [gain] * 4,
        out_specs=[qspec, qspec, kspec, vtspec, kspec, vtspec, qspec, kspec, vtspec,
                   pl.BlockSpec((T_TOK, 2 * LANES * IDX_HEADS), lambda b, i: (b * nt + i, 0)),
                   pl.BlockSpec((1, T_TOK, 2 * LANES), lambda b, i: (b, i, 0))],
        out_shape=[qshape, qshape, kshape, vtshape, kshape, vtshape, qshape, kshape, vtshape,
                   jax.ShapeDtypeStruct((BATCH * SEQ, 2 * LANES * IDX_HEADS), BF16),
                   jax.ShapeDtypeStruct((BATCH, SEQ, 2 * LANES), BF16)],
        compiler_params=_params(2),
        name="attn_prep",
    )(z, z, z, z, z, z, z, *tabs128, *tabs64, qgn, kgn, qgd, kgd)


def _cmp_body(zc, pe, w1, w2, kg, o, ot, kbuf):
    kbuf[0:SEQ, :] = zc[...]
    kbuf[SEQ:SEQ + LANES, :] = jnp.zeros((LANES, LANES), F32)
    acc = jnp.zeros((N_CMP_PAD, LANES), F32)
    for l in range(CMP_LEN):
        rows = kbuf[pl.ds(l, N_CMP_PAD, stride=CMP_STRIDE), :] + pe[0, l:l + 1, :]
        acc = acc + jnp.dot(rows.astype(BF16), w1[0, l], preferred_element_type=F32)
    out = jnp.dot(_silu(acc).astype(BF16), w2[0], preferred_element_type=F32)
    res = jnp.where(pl.program_id(1) == 0, _rms_head(out, kg[...]), out)
    o[0, 0, 0] = res.astype(BF16)
    ot[0, 0, 0] = res.T.astype(BF16)


def _compress(z, pe, w1, w2, kg):
    base = COL_B_KV // LANES
    return pl.pallas_call(
        _cmp_body,
        grid=(BATCH, 2, N_KV),
        in_specs=[pl.BlockSpec((SEQ, LANES), lambda b, kv, g: (b, base + kv * N_KV + g)),
                  pl.BlockSpec((1, CMP_LEN, LANES), lambda b, kv, g: (kv, 0, 0)),
                  pl.BlockSpec((1, CMP_LEN, LANES, LANES), lambda b, kv, g: (kv, 0, 0, 0)),
                  pl.BlockSpec((1, LANES, LANES), lambda b, kv, g: (kv, 0, 0)),
                  pl.BlockSpec((1, LANES), lambda b, kv, g: (0, 0))],
        out_specs=[pl.BlockSpec((1, 1, 1, N_CMP_PAD, LANES), lambda b, kv, g: (b, kv, g, 0, 0)),
                   pl.BlockSpec((1, 1, 1, LANES, N_CMP_PAD), lambda b, kv, g: (b, kv, g, 0, 0))],
        out_shape=[jax.ShapeDtypeStruct((BATCH, 2, N_KV, N_CMP_PAD, LANES), BF16),
                   jax.ShapeDtypeStruct((BATCH, 2, N_KV, LANES, N_CMP_PAD), BF16)],
        scratch_shapes=[pltpu.VMEM((SEQ + LANES, LANES), F32)],
        compiler_params=_params(3),
        name="nsa_compress",
    )(z, pe, w1, w2, kg)


def _key_norm_bound(k_ref):
    def body(c, mx):
        x = k_ref[pl.ds(pl.multiple_of(c * 512, 512), 512), :].astype(F32)
        return jnp.maximum(mx, jnp.max(jnp.sum(x * x, axis=1, keepdims=True), axis=0, keepdims=True))
    return jnp.broadcast_to(lax.fori_loop(0, SEQ // 512, body, jnp.zeros((1, 1), F32)), (1, TQ))


def _query_norms(q_ref):
    head_of_lane = lax.shift_right_logical(lax.broadcasted_iota(I32, (N_HEADS, GROUP_W), 1), 7)
    pick = jnp.where(head_of_lane == lax.broadcasted_iota(I32, (N_HEADS, GROUP_W), 0), 1.0, 0.0).astype(BF16)
    q = q_ref[...].astype(F32)
    return lax.dot_general(pick, (q * q).astype(BF16), NT_DIMS, preferred_element_type=F32)


def _flash_heads(q_ref, qn2, slot0, groups, jlo, jhi, m_s, l_s, acc_s):
    n_heads = sum(len(g[2]) for g in groups)
    lo, hi = slot0, slot0 + n_heads
    l_s[lo:hi, :] = jnp.zeros((n_heads, TQ), F32)
    for slot in range(lo, hi):
        acc_s[slot] = jnp.zeros((HEAD_DIM, TQ), F32)

    bounds = []
    for _, _, cs, _, kmax2 in groups:
        for c in cs:
            h = c.start // LANES
            bounds.append(jnp.sqrt(qn2[h:h + 1, :] * kmax2) * BOUND_SLACK)
    m_bound = jnp.concatenate(bounds, axis=0)

    def scores(j):
        r0 = pl.multiple_of(j * KV_TILE, KV_TILE)
        ss, vts, biases = [], [], {}
        for k_ref, vt_ref, cs, bias_fn, _ in groups:
            k = k_ref[pl.ds(r0, KV_TILE), :]
            vt = vt_ref[j]
            if bias_fn not in biases:
                biases[bias_fn] = bias_fn(j)
            b = biases[bias_fn]
            for c in cs:
                ss.append(lax.dot_general(k, q_ref[:, c], NT_DIMS, preferred_element_type=F32) + b)
                vts.append(vt)
        return ss, vts

    def max_pass(j, m):
        ss, _ = scores(j)
        return jnp.maximum(m, jnp.concatenate([_fold8(s, jnp.maximum) for s in ss], axis=0))

    def exact_max():
        m8 = lax.fori_loop(jlo, jhi, max_pass, jnp.full((8 * n_heads, TQ), NEG, F32))
        m_fin = jnp.concatenate([jnp.max(m8[8 * n:8 * n + 8], axis=0, keepdims=True)
                                 for n in range(n_heads)], axis=0)
        return jnp.maximum(m_fin, M_FLOOR)

    m_s[lo:hi, :] = lax.cond(jnp.max(m_bound) > BOUND_LIMIT, exact_max, lambda: m_bound)

    def tile(j):
        ss, vts = scores(j)
        m_ref = m_s[lo:hi, :]
        ps = [jnp.exp2(s - m_ref[n:n + 1, :]) for n, s in enumerate(ss)]
        l_add = jnp.concatenate([jnp.sum(p, axis=0, keepdims=True) for p in ps], axis=0)
        return l_add, [jnp.dot(vt, p.astype(BF16), preferred_element_type=F32) for vt, p in zip(vts, ps)]

    def one(j, carry):
        l_add, pv = tile(j)
        l_s[lo:hi, :] = l_s[lo:hi, :] + l_add
        for n in range(n_heads):
            acc_s[lo + n] = acc_s[lo + n] + pv[n]
        return carry

    def two(jj, carry):
        la, pa = tile(jlo + 2 * jj)
        lb, pb = tile(jlo + 2 * jj + 1)
        l_s[lo:hi, :] = l_s[lo:hi, :] + (la + lb)
        for n in range(n_heads):
            acc_s[lo + n] = acc_s[lo + n] + (pa[n] + pb[n])
        return carry

    pairs = lax.shift_right_logical(jhi - jlo, 1)
    lax.fori_loop(0, pairs, two, 0)
    lax.fori_loop(jlo + 2 * pairs, jhi, one, 0)


def _flash_out(slot, l_s, acc_s):
    return acc_s[slot] * (1.0 / l_s[slot:slot + 1, :])


def _nsa_body(qn, qr, kc, vct, ks, vst, kw, vwt, gate, bg, ovt, yb, imp_s, bias_s, m_s, l_s, acc_s, kn_s):
    i = pl.program_id(1)

    @pl.when(i == 0)
    def _():
        for g in range(N_KV):
            kn_s[g:g + 1, :] = _key_norm_bound(ks.at[0, g])
            kn_s[N_KV + g:N_KV + g + 1, :] = _key_norm_bound(kw.at[0, g])

    tq = i * TQ + lax.broadcasted_iota(I32, (1, TQ), 1)
    n_kv = i + 1
    sg = _sigmoid(gate[...].T[0:3 * N_HEADS, :])
    cm = (lax.broadcasted_iota(I32, (N_CMP_PAD, TQ), 0) * CMP_STRIDE + (CMP_LEN - 1)) <= tq
    kk = lax.broadcasted_iota(I32, (N_SLC, TQ), 0)
    sub8 = lax.broadcasted_iota(I32, (8, TQ), 0)
    blk_t = lax.shift_right_logical(tq, 6)
    valid = kk <= blk_t
    force = (kk == 0) | (kk == blk_t) | (kk == blk_t - 1)
    rowkv = lax.broadcasted_iota(I32, (KV_TILE, TQ), 0)
    cmp_slot, slc_slot, win_slot = 0, N_HEADS, 2 * N_HEADS
    head_cols = [[slice(h * LANES, (h + 1) * LANES) for h in range(HPG * g, HPG * (g + 1))]
                 for g in range(N_KV)]

    def win_bias(j):
        d = tq - (j * KV_TILE + rowkv)
        return jnp.where((d >= 0) & (d < WINDOW), 0.0, NEG)

    def slc_bias(g):
        return lambda j: bias_s[g, pl.ds(pl.multiple_of(j * KV_TILE, KV_TILE), KV_TILE), :]

    for g in range(N_KV):
        kcg = kc[0, 0, g]
        vcg = vct[0, 0, g]
        cols = head_cols[g]
        ss = [jnp.where(cm, lax.dot_general(kcg, qn[:, c], NT_DIMS, preferred_element_type=F32), NEG)
              for c in cols]
        es = [jnp.exp2(s - jnp.max(s, axis=0, keepdims=True)) for s in ss]
        ps = [jnp.where(cm, e * (1.0 / jnp.sum(e, axis=0, keepdims=True)), 0.0) for e in es]
        for hh, p in enumerate(ps):
            acc_s[cmp_slot + HPG * g + hh] = jnp.dot(vcg, p.astype(BF16), preferred_element_type=F32)
        psum = (ps[0] + ps[1]) + (ps[2] + ps[3])

        p_hi = psum.astype(BF16)
        p_lo = (psum - p_hi.astype(F32)).astype(BF16)
        imp = (jnp.dot(ovt[...], p_hi, preferred_element_type=F32)
               + jnp.dot(ovt[...], p_lo, preferred_element_type=F32))
        imp = jnp.where(force, FORCE, jnp.where(valid, imp, -FORCE))
        imp_s[...] = imp
        bands = [imp[8 * r:8 * r + 8] for r in range(N_SLC // 8)]
        ranks = [jnp.zeros((8, TQ), F32) for _ in bands]
        for j in range(N_SLC):
            vj = imp_s[j:j + 1, :]
            for r, x in enumerate(bands):
                if r < j // 8:
                    hit = vj > x
                elif r > j // 8:
                    hit = vj >= x
                else:
                    hit = (vj > x) | ((vj == x) & (sub8 > j % 8))
                ranks[r] = ranks[r] + jnp.where(hit, 1.0, 0.0)
        imp_s[...] = jnp.where(jnp.concatenate(ranks, axis=0) < SLC_TOPN, 0.0, NEG)

        def fill(j, carry):
            r0 = pl.multiple_of(j * KV_TILE, KV_TILE)
            nb = KV_TILE // SLC_BLK
            tile = jnp.concatenate([jnp.broadcast_to(imp_s[pl.ds(nb * j + u, 1), :], (SLC_BLK, TQ))
                                    for u in range(nb)], axis=0)
            bias_s[g, pl.ds(r0, KV_TILE), :] = jnp.where(r0 + rowkv <= tq, tile, NEG)
            return carry
        lax.fori_loop(0, n_kv, fill, 0)

    qn2 = _query_norms(qr)
    _flash_heads(qr, qn2, slc_slot, [(ks.at[0, g], vst.at[0, g], head_cols[g], slc_bias(g), kn_s[g:g + 1, :])
                                for g in range(N_KV)], 0, n_kv, m_s, l_s, acc_s)
    _flash_heads(qr, qn2, win_slot, [(kw.at[0, g], vwt.at[0, g], head_cols[g], win_bias,
                                 kn_s[N_KV + g:N_KV + g + 1, :]) for g in range(N_KV)],
                 jnp.maximum(n_kv - 1 - WINDOW // KV_TILE, 0), n_kv, m_s, l_s, acc_s)
    for h in range(N_HEADS):
        c = slice(h * LANES, (h + 1) * LANES)
        tot = (sg[h:h + 1] * acc_s[cmp_slot + h]
               + sg[N_HEADS + h:N_HEADS + h + 1] * _flash_out(slc_slot + h, l_s, acc_s)
               + sg[2 * N_HEADS + h:2 * N_HEADS + h + 1] * _flash_out(win_slot + h, l_s, acc_s))
        yb[:, c] = (tot.T * _silu(bg[:, c])).astype(BF16)


def _nsa(z, qn, qr, cmp_, cmpt, ks, vst, kw, vwt, ovt):
    nq = SEQ // TQ
    qspec = pl.BlockSpec((TQ, GROUP_W), lambda b, i: (b * nq + i, 0))
    kspec = pl.BlockSpec((1, N_KV, SEQ, LANES), lambda b, i: (b, 0, 0, 0))
    vtspec = pl.BlockSpec((1, N_KV, SEQ // KV_TILE, LANES, KV_TILE), lambda b, i: (b, 0, 0, 0, 0))
    return pl.pallas_call(
        _nsa_body,
        grid=(BATCH, nq),
        in_specs=[qspec, qspec,
                  pl.BlockSpec((1, 1, N_KV, N_CMP_PAD, LANES), lambda b, i: (b, 0, 0, 0, 0)),
                  pl.BlockSpec((1, 1, N_KV, LANES, N_CMP_PAD), lambda b, i: (b, 1, 0, 0, 0)),
                  kspec, vtspec, kspec, vtspec,
                  pl.BlockSpec((TQ, LANES), lambda b, i: (b * nq + i, COL_B_GATE // LANES)),
                  pl.BlockSpec((TQ, GROUP_W), lambda b, i: (b * nq + i, COL_B_G // GROUP_W)),
                  pl.BlockSpec((N_SLC, N_CMP_PAD), lambda b, i: (0, 0))],
        out_specs=qspec,
        out_shape=jax.ShapeDtypeStruct((BATCH * SEQ, GROUP_W), BF16),
        scratch_shapes=[pltpu.VMEM((N_SLC, TQ), F32), pltpu.VMEM((N_KV, SEQ, TQ), F32),
                        pltpu.VMEM((3 * N_HEADS, TQ), F32), pltpu.VMEM((3 * N_HEADS, TQ), F32),
                        pltpu.VMEM((3 * N_HEADS, HEAD_DIM, TQ), F32), pltpu.VMEM((8, TQ), F32)],
        compiler_params=_params(2),
        name="nsa_attn",
    )(qn, qr, cmp_, cmpt, ks, vst, kw, vwt, z, z, ovt)


def _dsa_body(qd, kd, vdt, iq, ik, iw, cg, yc, sc_s, m_s, l_s, acc_s, kn_s):
    i = pl.program_id(1)

    @pl.when(i == 0)
    def _():
        for g in range(N_KV):
            kn_s[g:g + 1, :] = _key_norm_bound(kd.at[0, g])

    tq = i * TQ + lax.broadcasted_iota(I32, (1, TQ), 1)
    n_kv = i + 1
    iwt = iw[...].T[0:IDX_HEADS, :] * IW_SCALE
    rowkv = lax.broadcasted_iota(I32, (KV_TILE, TQ), 0)
    kf = float(DSA_TOPK)

    def score_tile(j):
        r0 = pl.multiple_of(j * KV_TILE, KV_TILE)
        ikc = ik[0, pl.ds(r0, KV_TILE), :]
        sc = jnp.zeros((KV_TILE, TQ), F32)
        for h in range(IDX_HEADS):
            r = lax.dot_general(ikc, iq[:, 2 * h * LANES:(2 * h + 2) * LANES], NT_DIMS,
                                preferred_element_type=F32)
            sc = sc + jnp.maximum(r, 0.0) * iwt[h:h + 1, :]
        return r0, sc

    def fill(j, carry):
        mn, mx = carry
        r0, sc = score_tile(j)
        sc_s[pl.ds(r0, KV_TILE), :] = sc
        return (jnp.minimum(mn, _fold8(sc, jnp.minimum)), jnp.maximum(mx, _fold8(sc, jnp.maximum)))

    def fill2(jj, carry):
        return fill(2 * jj + 1, fill(2 * jj, carry))

    half = lax.shift_right_logical(i, 1)
    mm = lax.fori_loop(0, half, fill2, (jnp.full((8, TQ), jnp.inf, F32), jnp.full((8, TQ), -jnp.inf, F32)))
    mn8, mx8 = lax.fori_loop(2 * half, i, fill, mm)
    r0, sc = score_tile(i)
    ok = r0 + rowkv <= tq
    sc_s[pl.ds(r0, KV_TILE), :] = jnp.where(ok, sc, -jnp.inf)
    lo0 = jnp.min(jnp.minimum(mn8, _fold8(jnp.where(ok, sc, jnp.inf), jnp.minimum)), axis=0, keepdims=True)
    hi0 = jnp.max(jnp.maximum(mx8, _fold8(jnp.where(ok, sc, -jnp.inf), jnp.maximum)), axis=0, keepdims=True)

    def sweep(fn, n_out):
        def body(c, accs):
            r0 = pl.multiple_of(c * KV_TILE, KV_TILE)
            inds = fn(sc_s[pl.ds(r0, KV_TILE), :], r0 + rowkv)
            return tuple(a + _fold8(jnp.where(ind, 1.0, 0.0)) for a, ind in zip(accs, inds))
        accs = lax.fori_loop(0, n_kv, body, tuple(jnp.zeros((8, TQ), F32) for _ in range(n_out)))
        return tuple(jnp.sum(a, axis=0, keepdims=True) for a in accs)

    def data_span(lo, hi):
        def body(c, carry):
            mn, mx = carry
            x = sc_s[pl.ds(pl.multiple_of(c * KV_TILE, KV_TILE), KV_TILE), :]
            mn = jnp.minimum(mn, _fold8(jnp.where(x >= lo, x, jnp.inf), jnp.minimum))
            mx = jnp.maximum(mx, _fold8(jnp.where(x <= hi, x, -jnp.inf), jnp.maximum))
            return mn, mx
        mn, mx = lax.fori_loop(0, n_kv, body, (jnp.full((8, TQ), jnp.inf, F32), jnp.full((8, TQ), -jnp.inf, F32)))
        return jnp.min(mn, axis=0, keepdims=True), jnp.max(mx, axis=0, keepdims=True)

    nvalid = (tq + 1).astype(F32)
    act0 = jnp.where(nvalid > kf, 1.0, 0.0)

    def bisect(_, c):
        lo, hi, clo, act = c
        on = act > 0.0
        mid = lo + (hi - lo) * 0.5
        exh = (mid <= lo) | (mid >= hi)
        p = jnp.where(exh, hi, mid)
        (cnt,) = sweep(lambda x, srow: (x >= p,), 1)
        up = on & (cnt >= kf)
        dn = on & (cnt < kf)
        clo = jnp.where(up, cnt, clo)
        act = jnp.where(on & (clo != kf) & jnp.logical_not(exh), 1.0, 0.0)
        return jnp.where(up, p, lo), jnp.where(dn, p, hi), clo, act

    def snap(c):
        lo, hi, clo, act = c
        on = act > 0.0
        dmin, dmax = data_span(lo, hi)
        act = jnp.where(on & (dmin < dmax), 1.0, 0.0)
        return jnp.where(on, dmin, lo), jnp.where(on, dmax, hi), clo, act

    def cond(c):
        return c[0] > 0.0

    def round_(c):
        state = lax.fori_loop(0, SNAP_EVERY - 1, bisect, snap(c[1:]))
        return (jnp.sum(state[3]),) + state

    state = lax.fori_loop(0, SNAP_FIRST, bisect, (lo0, hi0, nvalid, act0))
    _, thr, _, _, _ = lax.while_loop(cond, round_, (jnp.sum(state[3]),) + state)

    cnt_g, cnt_e = sweep(lambda x, srow: (x > thr, x == thr), 2)
    need = kf - cnt_g
    excess = (cnt_g + cnt_e) > kf

    def tie_break():
        def it(_, c):
            jl, jh = c
            jm = lax.shift_right_arithmetic(jl + jh, 1)
            (cnt,) = sweep(lambda x, srow: ((x == thr) & (srow <= jm),), 1)
            ge = cnt >= need
            return jnp.where(ge, jl, jm), jnp.where(ge, jm, jh)
        _, jh = lax.fori_loop(0, 12, it, (jnp.full((1, TQ), -1, I32), jnp.full((1, TQ), SEQ - 1, I32)))
        return jnp.where(excess, jh, SEQ)

    last = lax.cond(jnp.sum(jnp.where(excess, 1.0, 0.0)) > 0.0, tie_break,
                    lambda: jnp.full((1, TQ), SEQ, I32))

    def to_bias(c, carry):
        r0 = pl.multiple_of(c * KV_TILE, KV_TILE)
        x = sc_s[pl.ds(r0, KV_TILE), :]
        sel = (x > thr) | ((x == thr) & (r0 + rowkv <= last))
        sc_s[pl.ds(r0, KV_TILE), :] = jnp.where(sel, 0.0, NEG)
        return carry
    lax.fori_loop(0, n_kv, to_bias, 0)

    def bias(j):
        return sc_s[pl.ds(pl.multiple_of(j * KV_TILE, KV_TILE), KV_TILE), :]

    groups = [(kd.at[0, g], vdt.at[0, g], [slice(h * LANES, (h + 1) * LANES)
                                           for h in range(HPG * g, HPG * (g + 1))], bias, kn_s[g:g + 1, :])
              for g in range(N_KV)]
    _flash_heads(qd, _query_norms(qd), 0, groups, 0, n_kv, m_s, l_s, acc_s)
    for h in range(N_HEADS):
        c = slice(h * LANES, (h + 1) * LANES)
        yc[:, c] = (_flash_out(h, l_s, acc_s).T * _silu(cg[:, c])).astype(BF16)


def _dsa(z, qd, kd, vdt, iq, ik):
    nq = SEQ // TQ
    qspec = pl.BlockSpec((TQ, GROUP_W), lambda b, i: (b * nq + i, 0))
    return pl.pallas_call(
        _dsa_body,
        grid=(BATCH, nq),
        in_specs=[qspec,
                  pl.BlockSpec((1, N_KV, SEQ, LANES), lambda b, i: (b, 0, 0, 0)),
                  pl.BlockSpec((1, N_KV, SEQ // KV_TILE, LANES, KV_TILE), lambda b, i: (b, 0, 0, 0, 0)),
                  pl.BlockSpec((TQ, 2 * LANES * IDX_HEADS), lambda b, i: (b * nq + i, 0)),
                  pl.BlockSpec((1, SEQ, 2 * LANES), lambda b, i: (b, 0, 0)),
                  pl.BlockSpec((TQ, LANES), lambda b, i: (b * nq + i, COL_C_IW // LANES)),
                  pl.BlockSpec((TQ, GROUP_W), lambda b, i: (b * nq + i, COL_C_G // GROUP_W))],
        out_specs=qspec,
        out_shape=jax.ShapeDtypeStruct((BATCH * SEQ, GROUP_W), BF16),
        scratch_shapes=[pltpu.VMEM((SEQ, TQ), F32), pltpu.VMEM((N_HEADS, TQ), F32),
                        pltpu.VMEM((N_HEADS, TQ), F32), pltpu.VMEM((N_HEADS, HEAD_DIM, TQ), F32),
                        pltpu.VMEM((8, TQ), F32)],
        compiler_params=_params(2),
        name="dsa_attn",
    )(qd, kd, vdt, iq, ik, z, z)


def _segment_moves():
    src = {}
    o = 0
    names = ("a_x", "a_g", "b_q", "b_g", "b_kv", "b_gate", "c_q", "c_g", "c_kv", "c_iq", "c_ik", "c_iw",
             "d_in", "d_b", "d_c", "d_g")
    for n, s in zip(names, SPLIT_SIZES):
        src[n] = (o, s)
        o += s
    dst = dict(a_x=COL_A_X, a_g=COL_A_G, b_q=COL_B_Q, b_g=COL_B_G, c_q=COL_C_Q, c_g=COL_C_G,
               d_in=COL_D_IN, d_b=COL_D_B, d_c=COL_D_C, d_g=COL_D_G, b_kv=COL_B_KV, c_kv=COL_C_KV,
               c_iq=COL_C_IQ, b_gate=COL_B_GATE, c_ik=COL_C_IK, c_iw=COL_C_IW)
    return [(src[n][0], dst[n], src[n][1]) for n in names]


def _wprep_body(w_ref, o_ref):
    o_ref[COL_B_GATE:PROJ_W, :] = jnp.zeros((PROJ_W - COL_B_GATE, TR_W), BF16)
    for s, d, width in _segment_moves():
        for c in range(0, width, GROUP_W):
            n = min(GROUP_W, width - c)
            o_ref[d + c:d + c + n, :] = w_ref[s + c:s + c + n, :].astype(BF16)


def _proj_weight(wt, layer):
    _, n, k = wt.shape
    return pl.pallas_call(
        _wprep_body,
        grid=(k // TR_W,),
        in_specs=[pl.BlockSpec((None, n, TR_W), lambda i: (layer, 0, i))],
        out_specs=pl.BlockSpec((PROJ_W, TR_W), lambda i: (0, i)),
        out_shape=jax.ShapeDtypeStruct((PROJ_W, k), BF16),
        compiler_params=_params(1),
        name="w_in_layout",
    )(wt)


def _rope_tables(head_dim):
    r = head_dim // 4
    half = r // 2
    pos = jnp.arange(SEQ, dtype=jnp.int32)
    inv = ROPE_THETA ** (-jnp.arange(half, dtype=F32) * 2.0 / r)
    ang = pos.astype(F32)[:, None] * inv[None, :]
    cos, sin = jnp.cos(ang), jnp.sin(ang)
    one = jnp.ones((SEQ, head_dim - r), F32)
    zero = jnp.zeros((SEQ, head_dim - r), F32)
    zh = jnp.zeros((SEQ, half), F32)
    reps = LANES // head_dim
    return tuple(jnp.tile(t, (1, reps)) for t in (
        jnp.concatenate([cos, cos, one], axis=1),
        jnp.concatenate([-sin, zh, zero], axis=1),
        jnp.concatenate([zh, sin, zero], axis=1)))


def _overlap_t():
    cs = jnp.arange(N_CMP_PAD) * CMP_STRIDE
    ss = jnp.arange(N_SLC) * SLC_BLK
    ov = jnp.clip(jnp.minimum(cs[None, :] + CMP_LEN, ss[:, None] + SLC_BLK)
                  - jnp.maximum(cs[None, :], ss[:, None]), 0, None).astype(F32) / CMP_LEN
    return jnp.where(jnp.arange(N_CMP_PAD)[None, :] < N_CMP_PAD - 1, ov, 0.0).astype(BF16)


def kernel(x, norm_g, w_in, w_out, lru_conv_w, lru_conv_b, lru_wa, lru_ba, lru_wx, lru_bx, lru_lambda,
           nsa_q_gain, nsa_k_gain, cmp_pe_k, cmp_w1_k, cmp_w2_k, cmp_pe_v, cmp_w1_v, cmp_w2_v,
           dsa_q_gain, dsa_k_gain, sc_conv_w):
    b, s, d = x.shape
    assert (b, s, d) == (BATCH, SEQ, D_MODEL)
    tabs128 = _rope_tables(HEAD_DIM)
    tabs64 = _rope_tables(IDX_DIM)
    ovt = _overlap_t()
    x2 = x.reshape(b * s, d)
    w_in_t = jnp.swapaxes(w_in, 1, 2)

    def row(v):
        return v.reshape(1, -1)

    for l in range(DEPTH):
        z = _inproj(x2, row(norm_g[l]), _proj_weight(w_in_t, l))
        ya, yd = _mix_ad(z, lru_conv_w[l], row(lru_conv_b[l]), lru_wa[l].astype(BF16), row(lru_ba[l]),
                         lru_wx[l].astype(BF16), row(lru_bx[l]), row(lru_lambda[l]), sc_conv_w[l])
        qn, qr, ks, vst, kw, vwt, qd, kd, vdt, iq, ik = _prep(
            z, tabs128, tabs64, row(nsa_q_gain[l]), row(nsa_k_gain[l]), row(dsa_q_gain[l]), row(dsa_k_gain[l]))
        cmp_, cmpt = _compress(z, jnp.stack([cmp_pe_k[l], cmp_pe_v[l]]),
                               jnp.stack([cmp_w1_k[l], cmp_w1_v[l]]).astype(BF16),
                               jnp.stack([cmp_w2_k[l], cmp_w2_v[l]]).astype(BF16), row(nsa_k_gain[l]))
        yb = _nsa(z, qn, qr, cmp_, cmpt, ks, vst, kw, vwt, ovt)
        yc = _dsa(z, qd, kd, vdt, iq, ik)
        x2 = _outproj((ya, yb, yc, yd), w_out, l, x2)
    return x2.reshape(b, s, d)
```

```python
import functools

import jax
import jax.numpy as jnp
from jax import lax
from jax.experimental import pallas as pl
from jax.experimental.pallas import tpu as pltpu

F32 = jnp.float32
BF16 = jnp.bfloat16
I32 = jnp.int32

D_MODEL = 4096
BATCH = 2
SEQ = 4096
DEPTH = 2
GROUP_W = D_MODEL // 4
HEAD_DIM = 128
N_HEADS = GROUP_W // HEAD_DIM
N_KV = N_HEADS // 4
HPG = N_HEADS // N_KV
ROPE_THETA = 500000.0
NORM_EPS = 1e-6
NEG = -1e30
FORCE = 1e6
LRU_BLOCKS = N_HEADS
LRU_C = 8.0
CONV_A = 4
CONV_D = 3
CMP_LEN = 32
CMP_STRIDE = 16
SLC_BLK = 64
SLC_TOPN = 16
WINDOW = 512
IDX_HEADS = 8
IDX_DIM = 64
DSA_TOPK = 256
SCALE = HEAD_DIM ** -0.5
Q_PRESCALE = SCALE * 1.4426950408889634
IW_SCALE = IDX_HEADS ** -0.5 * IDX_DIM ** -0.5

SPLIT_SIZES = (
    GROUP_W, GROUP_W,
    N_HEADS * HEAD_DIM, GROUP_W, 6 * N_KV * HEAD_DIM, 3 * N_HEADS,
    N_HEADS * HEAD_DIM, GROUP_W, 2 * N_KV * HEAD_DIM,
    IDX_HEADS * IDX_DIM, IDX_DIM, IDX_HEADS,
    GROUP_W, GROUP_W, GROUP_W, GROUP_W,
)

LANES = 128
COL_A_X, COL_A_G, COL_B_Q, COL_B_G, COL_C_Q, COL_C_G = 0, 1024, 2048, 3072, 4096, 5120
COL_D_IN, COL_D_B, COL_D_C, COL_D_G = 6144, 7168, 8192, 9216
COL_B_KV, COL_C_KV, COL_C_IQ = 10240, 11776, 12288
COL_B_GATE, COL_C_IK, COL_C_IW = 12800, 12928, 13056
PROJ_W = 13312

N_CMP_PAD = SEQ // CMP_STRIDE
N_SLC = SEQ // SLC_BLK
KV_TILE = 256
M_FLOOR = -1e20
BOUND_SLACK = 1.01
BOUND_LIMIT = 60.0
SNAP_FIRST, SNAP_EVERY = 16, 2

VMEM_LIMIT_BYTES = 56 * 1024 * 1024
INPROJ_VMEM_LIMIT_BYTES = 60 * 1024 * 1024

TM_IN, TN_IN = 1024, 1024
TM_OUT, TN_OUT = 1024, 512
T_TOK = 256
TR_W = 128
TQ = 256

NT_DIMS = (((1,), (1,)), ((), ()))


def _params(n_axes, vmem_limit_bytes=VMEM_LIMIT_BYTES):
    return pltpu.CompilerParams(dimension_semantics=("arbitrary",) * n_axes,
                                vmem_limit_bytes=vmem_limit_bytes)


def _sigmoid(x):
    return 0.5 * jnp.tanh(0.5 * x) + 0.5


def _silu(x):
    return x * _sigmoid(x)


def _fold8(x, op=jnp.add):
    parts = [x[r:r + 8] for r in range(0, x.shape[0], 8)]
    while len(parts) > 1:
        nxt = [op(parts[a], parts[a + 1]) for a in range(0, len(parts) - 1, 2)]
        if len(parts) % 2:
            nxt.append(parts[-1])
        parts = nxt
    return parts[0]


def _inproj_body(x_hbm, g_ref, w_ref, o_ref, x_buf, h_ref, sem):
    i = pl.program_id(0)
    n_row_tiles = pl.num_programs(0)

    def x_copy(tile):
        return pltpu.make_async_copy(x_hbm.at[pl.ds(pl.multiple_of(tile * TM_IN, TM_IN), TM_IN), :], x_buf, sem)

    @pl.when(pl.program_id(1) == 0)
    def _():
        @pl.when(i == 0)
        def _():
            x_copy(0).start()

        x_copy(i).wait()

        def rows(r, carry):
            r0 = pl.multiple_of(r * 64, 64)
            x = x_buf[pl.ds(r0, 64), :]
            ms = jnp.mean(x * x, axis=-1, keepdims=True)
            h_ref[pl.ds(r0, 64), :] = ((x * lax.rsqrt(ms + NORM_EPS)) * g_ref[...]).astype(BF16)
            return carry
        lax.fori_loop(0, TM_IN // 64, rows, 0)

        @pl.when(i + 1 < n_row_tiles)
        def _():
            x_copy(i + 1).start()

    o_ref[...] = lax.dot_general(h_ref[...], w_ref[...], NT_DIMS, preferred_element_type=F32)


def _inproj(x2, g, wt):
    m = x2.shape[0]
    return pl.pallas_call(
        _inproj_body,
        grid=(m // TM_IN, PROJ_W // TN_IN),
        in_specs=[pl.BlockSpec(memory_space=pl.ANY),
                  pl.BlockSpec((1, D_MODEL), lambda i, j: (0, 0)),
                  pl.BlockSpec((TN_IN, D_MODEL), lambda i, j: (j, 0))],
        out_specs=pl.BlockSpec((TM_IN, TN_IN), lambda i, j: (i, j)),
        out_shape=jax.ShapeDtypeStruct((m, PROJ_W), F32),
        scratch_shapes=[pltpu.VMEM((TM_IN, D_MODEL), F32), pltpu.VMEM((TM_IN, D_MODEL), BF16),
                        pltpu.SemaphoreType.DMA(())],
        compiler_params=_params(2, INPROJ_VMEM_LIMIT_BYTES),
        name="inproj",
    )(x2, g, wt)


def _outproj_body(ya, yb, yc, yd, wa, wb, wc, wd, x_ref, o_ref):
    acc = jnp.dot(ya[...], wa[...].astype(BF16), preferred_element_type=F32)
    acc = acc + jnp.dot(yb[...], wb[...].astype(BF16), preferred_element_type=F32)
    acc = acc + jnp.dot(yc[...], wc[...].astype(BF16), preferred_element_type=F32)
    acc = acc + jnp.dot(yd[...], wd[...].astype(BF16), preferred_element_type=F32)
    o_ref[...] = x_ref[...] + acc


def _outproj(ys, w, layer, x2):
    m = x2.shape[0]
    yspec = pl.BlockSpec((TM_OUT, GROUP_W), lambda i, j: (i, 0))
    wspecs = [pl.BlockSpec((None, GROUP_W, TN_OUT), functools.partial(lambda i, j, k: (layer, k, j), k=k))
              for k in range(4)]
    xspec = pl.BlockSpec((TM_OUT, TN_OUT), lambda i, j: (i, j))
    return pl.pallas_call(
        _outproj_body,
        grid=(m // TM_OUT, D_MODEL // TN_OUT),
        in_specs=[yspec] * 4 + wspecs + [xspec],
        out_specs=xspec,
        out_shape=jax.ShapeDtypeStruct((m, D_MODEL), F32),
        compiler_params=_params(2),
        name="outproj",
    )(*ys, w, w, w, w, x2)


def _mix_ad_body(ax, ag, din, db, dc, dg, cw, cb, wa, ba, wx, bx, lam, scw,
                 ya, yd, xbuf, vbuf, a_s, b_s, hst):
    t = T_TOK

    @pl.when(pl.program_id(1) == 0)
    def _():
        xbuf[0:8, :] = jnp.zeros((8, GROUP_W), F32)
        vbuf[0:8, :] = jnp.zeros((8, GROUP_W), F32)
        hst[...] = jnp.zeros((8, GROUP_W), F32)

    xbuf[8:8 + t, :] = ax[...]
    for blk in range(LRU_BLOCKS):
        c = slice(blk * LANES, (blk + 1) * LANES)
        u = (cw[0:1, c] * xbuf[5:5 + t, c] + cw[1:2, c] * xbuf[6:6 + t, c]
             + cw[2:3, c] * xbuf[7:7 + t, c] + cw[3:4, c] * xbuf[8:8 + t, c]) + cb[:, c]
        ub = u.astype(BF16)
        r = _sigmoid(jnp.dot(ub, wa[blk], preferred_element_type=F32) + ba[:, c])
        ig = _sigmoid(jnp.dot(ub, wx[blk], preferred_element_type=F32) + bx[:, c])
        nl = -lam[:, c]
        sp = jnp.maximum(nl, 0.0) + jnp.log1p(jnp.exp(-jnp.abs(nl)))
        log_a = (-LRU_C * r) * sp
        a = jnp.exp(log_a)
        a_s[:, c] = a
        em1 = jnp.tanh(log_a) * (a * a + 1.0)
        b_s[:, c] = jnp.sqrt(-em1) * (ig * u)
    xbuf[0:8, :] = xbuf[t:t + 8, :]

    row = lax.broadcasted_iota(I32, (8, GROUP_W), 0)

    def group(gi, hprev):
        r0 = pl.multiple_of(gi * 8, 8)
        a = a_s[pl.ds(r0, 8), :]
        b = b_s[pl.ds(r0, 8), :]
        for d in (1, 2, 4):
            ok = row >= d
            b = jnp.where(ok, a * pltpu.roll(b, d, axis=0) + b, b)
            a = jnp.where(ok, a * pltpu.roll(a, d, axis=0), a)
        h = a * hprev + b
        b_s[pl.ds(r0, 8), :] = h
        return jnp.broadcast_to(h[7:8, :], (8, GROUP_W))

    hst[...] = lax.fori_loop(0, t // 8, group, hst[...])

    vbuf[8:8 + t, :] = dc[...] * din[...]
    for blk in range(LRU_BLOCKS):
        c = slice(blk * LANES, (blk + 1) * LANES)
        ya[:, c] = (b_s[:, c] * _silu(ag[:, c])).astype(BF16)
        conv = (scw[0:1, c] * vbuf[6:6 + t, c] + scw[1:2, c] * vbuf[7:7 + t, c]
                + scw[2:3, c] * vbuf[8:8 + t, c])
        yd[:, c] = ((db[:, c] * conv) * _silu(dg[:, c])).astype(BF16)
    vbuf[0:8, :] = vbuf[t:t + 8, :]


def _mix_ad(z, cw, cb, wa, ba, wx, bx, lam, scw):
    nt = SEQ // T_TOK

    def zspec(col):
        return pl.BlockSpec((T_TOK, GROUP_W), lambda b, i: (b * nt + i, col // GROUP_W))

    def full(a):
        return pl.BlockSpec(a.shape, lambda b, i: (0,) * a.ndim)

    small = (cw, cb, wa, ba, wx, bx, lam, scw)
    yspec = pl.BlockSpec((T_TOK, GROUP_W), lambda b, i: (b * nt + i, 0))
    return pl.pallas_call(
        _mix_ad_body,
        grid=(BATCH, nt),
        in_specs=[zspec(COL_A_X), zspec(COL_A_G), zspec(COL_D_IN), zspec(COL_D_B), zspec(COL_D_C),
                  zspec(COL_D_G)] + [full(a) for a in small],
        out_specs=[yspec, yspec],
        out_shape=[jax.ShapeDtypeStruct((BATCH * SEQ, GROUP_W), BF16)] * 2,
        scratch_shapes=[pltpu.VMEM((T_TOK + 8, GROUP_W), F32), pltpu.VMEM((T_TOK + 8, GROUP_W), F32),
                        pltpu.VMEM((T_TOK, GROUP_W), F32), pltpu.VMEM((T_TOK, GROUP_W), F32),
                        pltpu.VMEM((8, GROUP_W), F32)],
        compiler_params=_params(2),
        name="mix_ad",
    )(z, z, z, z, z, z, *small)


def _rms_head(x, gain):
    ms = jnp.mean(x * x, axis=-1, keepdims=True)
    return (x * lax.rsqrt(ms + NORM_EPS)) * gain


def _rope(y, cos, nsin, psin, half):
    return y * cos + pltpu.roll(y, LANES - half, axis=1) * nsin + pltpu.roll(y, half, axis=1) * psin


def _store_vt(dst, g, v):
    for cc in range(T_TOK // KV_TILE):
        dst[0, g, cc] = v[cc * KV_TILE:(cc + 1) * KV_TILE, :].T.astype(BF16)


def _rms_head_mxu(x, gain):
    ssq = jnp.dot((x * x).astype(BF16), jnp.ones((LANES, LANES), BF16), preferred_element_type=F32)
    return (x * lax.rsqrt(ssq * (1.0 / LANES) + NORM_EPS)) * gain


def _prep_body(bq, bkv_s, bkv_w, cq, ckv, ciq, cik, c128, n128, p128, c64, n64, p64,
               qgn, kgn, qgd, kgd,
               qn_o, qr_o, ks_o, vst_o, kw_o, vwt_o, qd_o, kd_o, vdt_o, iq_o, ik_o):
    r128 = (c128[...], n128[...], p128[...], HEAD_DIM // 8)
    r64 = (c64[...], n64[...], p64[...], IDX_DIM // 8)
    for h in range(N_HEADS):
        c = slice(h * LANES, (h + 1) * LANES)
        y = _rms_head_mxu(bq[:, c], qgn[...])
        qn_o[:, c] = (y * Q_PRESCALE).astype(BF16)
        qr_o[:, c] = (_rope(y, *r128) * Q_PRESCALE).astype(BF16)
        qd_o[:, c] = (_rope(_rms_head_mxu(cq[:, c], qgd[...]), *r128) * Q_PRESCALE).astype(BF16)
    for g in range(N_KV):
        ck = slice(g * LANES, (g + 1) * LANES)
        cv = slice((N_KV + g) * LANES, (N_KV + g + 1) * LANES)
        ks_o[0, g] = _rope(_rms_head_mxu(bkv_s[:, ck], kgn[...]), *r128).astype(BF16)
        _store_vt(vst_o, g, bkv_s[:, cv])
        kw_o[0, g] = _rope(_rms_head_mxu(bkv_w[:, ck], kgn[...]), *r128).astype(BF16)
        _store_vt(vwt_o, g, bkv_w[:, cv])
        kd_o[0, g] = _rope(_rms_head_mxu(ckv[:, ck], kgd[...]), *r128).astype(BF16)
        _store_vt(vdt_o, g, ckv[:, cv])

    low = lax.broadcasted_iota(I32, (T_TOK, LANES), 1) < IDX_DIM
    ik = _rope(cik[...], *r64)
    hi = ik.astype(BF16).astype(F32)
    ik_o[0, :, 0:LANES] = (hi + pltpu.roll(ik - hi, IDX_DIM, axis=1)).astype(BF16)
    ik_o[0, :, LANES:2 * LANES] = hi.astype(BF16)
    for v in range(IDX_HEADS // 2):
        x = _rope(ciq[:, v * LANES:(v + 1) * LANES], *r64)
        for half in range(2):
            h = 2 * v + half
            xs = x if half == 0 else pltpu.roll(x, IDX_DIM, axis=1)
            xh = jnp.where(low, xs, 0.0)
            hi = xh.astype(BF16).astype(F32)
            iq_o[:, 2 * h * LANES:(2 * h + 1) * LANES] = (hi + pltpu.roll(hi, IDX_DIM, axis=1)).astype(BF16)
            iq_o[:, (2 * h + 1) * LANES:(2 * h + 2) * LANES] = (xh - hi).astype(BF16)


def _prep(z, tabs128, tabs64, qgn, kgn, qgd, kgd):
    nt = SEQ // T_TOK

    def zspec(col, width):
        return pl.BlockSpec((T_TOK, width), lambda b, i: (b * nt + i, col // width))

    tab = pl.BlockSpec((T_TOK, LANES), lambda b, i: (i, 0))
    gain = pl.BlockSpec((1, LANES), lambda b, i: (0, 0))
    qspec = pl.BlockSpec((T_TOK, GROUP_W), lambda b, i: (b * nt + i, 0))
    kspec = pl.BlockSpec((1, N_KV, T_TOK, LANES), lambda b, i: (b, 0, i, 0))
    vtspec = pl.BlockSpec((1, N_KV, T_TOK // KV_TILE, LANES, KV_TILE), lambda b, i: (b, 0, i, 0, 0))
    qshape = jax.ShapeDtypeStruct((BATCH * SEQ, GROUP_W), BF16)
    kshape = jax.ShapeDtypeStruct((BATCH, N_KV, SEQ, LANES), BF16)
    vtshape = jax.ShapeDtypeStruct((BATCH, N_KV, SEQ // KV_TILE, LANES, KV_TILE), BF16)
    return pl.pallas_call(
        _prep_body,
        grid=(BATCH, nt),
        in_specs=[zspec(COL_B_Q, GROUP_W), zspec(COL_B_KV + 512, 512), zspec(COL_B_KV + 1024, 512),
                  zspec(COL_C_Q, GROUP_W), zspec(COL_C_KV, 512), zspec(COL_C_IQ, 512), zspec(COL_C_IK, LANES)]
                 + [tab] * 6 + [gain] * 4,
        out_specs=[qspec, qspec, kspec, vtspec, kspec, vtspec, qspec, kspec, vtspec,
                   pl.BlockSpec((T_TOK, 2 * LANES * IDX_HEADS), lambda b, i: (b * nt + i, 0)),
                   pl.BlockSpec((1, T_TOK, 2 * LANES), lambda b, i: (b, i, 0))],
        out_shape=[qshape, qshape, kshape, vtshape, kshape, vtshape, qshape, kshape, vtshape,
                   jax.ShapeDtypeStruct((BATCH * SEQ, 2 * LANES * IDX_HEADS), BF16),
                   jax.ShapeDtypeStruct((BATCH, SEQ, 2 * LANES), BF16)],
        compiler_params=_params(2),
        name="attn_prep",
    )(z, z, z, z, z, z, z, *tabs128, *tabs64, qgn, kgn, qgd, kgd)


def _cmp_body(zc, pe, w1, w2, kg, o, ot, kbuf):
    kbuf[0:SEQ, :] = zc[...]
    kbuf[SEQ:SEQ + LANES, :] = jnp.zeros((LANES, LANES), F32)
    acc = jnp.zeros((N_CMP_PAD, LANES), F32)
    for l in range(CMP_LEN):
        rows = kbuf[pl.ds(l, N_CMP_PAD, stride=CMP_STRIDE), :] + pe[0, l:l + 1, :]
        acc = acc + jnp.dot(rows.astype(BF16), w1[0, l], preferred_element_type=F32)
    out = jnp.dot(_silu(acc).astype(BF16), w2[0], preferred_element_type=F32)
    res = jnp.where(pl.program_id(1) == 0, _rms_head(out, kg[...]), out)
    o[0, 0, 0] = res.astype(BF16)
    ot[0, 0, 0] = res.T.astype(BF16)


def _compress(z, pe, w1, w2, kg):
    base = COL_B_KV // LANES
    return pl.pallas_call(
        _cmp_body,
        grid=(BATCH, 2, N_KV),
        in_specs=[pl.BlockSpec((SEQ, LANES), lambda b, kv, g: (b, base + kv * N_KV + g)),
                  pl.BlockSpec((1, CMP_LEN, LANES), lambda b, kv, g: (kv, 0, 0)),
                  pl.BlockSpec((1, CMP_LEN, LANES, LANES), lambda b, kv, g: (kv, 0, 0, 0)),
                  pl.BlockSpec((1, LANES, LANES), lambda b, kv, g: (kv, 0, 0)),
                  pl.BlockSpec((1, LANES), lambda b, kv, g: (0, 0))],
        out_specs=[pl.BlockSpec((1, 1, 1, N_CMP_PAD, LANES), lambda b, kv, g: (b, kv, g, 0, 0)),
                   pl.BlockSpec((1, 1, 1, LANES, N_CMP_PAD), lambda b, kv, g: (b, kv, g, 0, 0))],
        out_shape=[jax.ShapeDtypeStruct((BATCH, 2, N_KV, N_CMP_PAD, LANES), BF16),
                   jax.ShapeDtypeStruct((BATCH, 2, N_KV, LANES, N_CMP_PAD), BF16)],
        scratch_shapes=[pltpu.VMEM((SEQ + LANES, LANES), F32)],
        compiler_params=_params(3),
        name="nsa_compress",
    )(z, pe, w1, w2, kg)


def _key_norm_bound(k_ref):
    def body(c, mx):
        x = k_ref[pl.ds(pl.multiple_of(c * 512, 512), 512), :].astype(F32)
        return jnp.maximum(mx, jnp.max(jnp.sum(x * x, axis=1, keepdims=True), axis=0, keepdims=True))
    return jnp.broadcast_to(lax.fori_loop(0, SEQ // 512, body, jnp.zeros((1, 1), F32)), (1, TQ))


def _query_norms(q_ref):
    head_of_lane = lax.shift_right_logical(lax.broadcasted_iota(I32, (N_HEADS, GROUP_W), 1), 7)
    pick = jnp.where(head_of_lane == lax.broadcasted_iota(I32, (N_HEADS, GROUP_W), 0), 1.0, 0.0).astype(BF16)
    q = q_ref[...].astype(F32)
    return lax.dot_general(pick, (q * q).astype(BF16), NT_DIMS, preferred_element_type=F32)


def _flash_heads(q_ref, qn2, slot0, groups, jlo, jhi, m_s, l_s, acc_s):
    n_heads = sum(len(g[2]) for g in groups)
    lo, hi = slot0, slot0 + n_heads
    l_s[lo:hi, :] = jnp.zeros((n_heads, TQ), F32)
    for slot in range(lo, hi):
        acc_s[slot] = jnp.zeros((HEAD_DIM, TQ), F32)

    bounds = []
    for _, _, cs, _, kmax2 in groups:
        for c in cs:
            h = c.start // LANES
            bounds.append(jnp.sqrt(qn2[h:h + 1, :] * kmax2) * BOUND_SLACK)
    m_bound = jnp.concatenate(bounds, axis=0)

    def scores(j):
        r0 = pl.multiple_of(j * KV_TILE, KV_TILE)
        ss, vts, biases = [], [], {}
        for k_ref, vt_ref, cs, bias_fn, _ in groups:
            k = k_ref[pl.ds(r0, KV_TILE), :]
            vt = vt_ref[j]
            if bias_fn not in biases:
                biases[bias_fn] = bias_fn(j)
            b = biases[bias_fn]
            for c in cs:
                ss.append(lax.dot_general(k, q_ref[:, c], NT_DIMS, preferred_element_type=F32) + b)
                vts.append(vt)
        return ss, vts

    def max_pass(j, m):
        ss, _ = scores(j)
        return jnp.maximum(m, jnp.concatenate([_fold8(s, jnp.maximum) for s in ss], axis=0))

    def exact_max():
        m8 = lax.fori_loop(jlo, jhi, max_pass, jnp.full((8 * n_heads, TQ), NEG, F32))
        m_fin = jnp.concatenate([jnp.max(m8[8 * n:8 * n + 8], axis=0, keepdims=True)
                                 for n in range(n_heads)], axis=0)
        return jnp.maximum(m_fin, M_FLOOR)

    m_s[lo:hi, :] = lax.cond(jnp.max(m_bound) > BOUND_LIMIT, exact_max, lambda: m_bound)

    def tile(j):
        ss, vts = scores(j)
        m_ref = m_s[lo:hi, :]
        ps = [jnp.exp2(s - m_ref[n:n + 1, :]) for n, s in enumerate(ss)]
        l_add = jnp.concatenate([jnp.sum(p, axis=0, keepdims=True) for p in ps], axis=0)
        return l_add, [jnp.dot(vt, p.astype(BF16), preferred_element_type=F32) for vt, p in zip(vts, ps)]

    def one(j, carry):
        l_add, pv = tile(j)
        l_s[lo:hi, :] = l_s[lo:hi, :] + l_add
        for n in range(n_heads):
            acc_s[lo + n] = acc_s[lo + n] + pv[n]
        return carry

    def two(jj, carry):
        la, pa = tile(jlo + 2 * jj)
        lb, pb = tile(jlo + 2 * jj + 1)
        l_s[lo:hi, :] = l_s[lo:hi, :] + (la + lb)
        for n in range(n_heads):
            acc_s[lo + n] = acc_s[lo + n] + (pa[n] + pb[n])
        return carry

    pairs = lax.shift_right_logical(jhi - jlo, 1)
    lax.fori_loop(0, pairs, two, 0)
    lax.fori_loop(jlo + 2 * pairs, jhi, one, 0)


def _flash_out(slot, l_s, acc_s):
    return acc_s[slot] * (1.0 / l_s[slot:slot + 1, :])


def _nsa_body(qn, qr, kc, vct, ks, vst, kw, vwt, gate, bg, ovt, yb, imp_s, sel_s, m_s, l_s, acc_s, kn_s):
    i = pl.program_id(1)

    @pl.when(i == 0)
    def _():
        for g in range(N_KV):
            kn_s[g:g + 1, :] = _key_norm_bound(ks.at[0, g])
            kn_s[N_KV + g:N_KV + g + 1, :] = _key_norm_bound(kw.at[0, g])

    tq = i * TQ + lax.broadcasted_iota(I32, (1, TQ), 1)
    n_kv = i + 1
    sg = _sigmoid(gate[...].T[0:3 * N_HEADS, :])
    cm = (lax.broadcasted_iota(I32, (N_CMP_PAD, TQ), 0) * CMP_STRIDE + (CMP_LEN - 1)) <= tq
    kk = lax.broadcasted_iota(I32, (N_SLC, TQ), 0)
    sub8 = lax.broadcasted_iota(I32, (8, TQ), 0)
    blk_t = lax.shift_right_logical(tq, 6)
    valid = kk <= blk_t
    force = (kk == 0) | (kk == blk_t) | (kk == blk_t - 1)
    rowkv = lax.broadcasted_iota(I32, (KV_TILE, TQ), 0)
    cmp_slot, slc_slot, win_slot = 0, N_HEADS, 2 * N_HEADS
    head_cols = [[slice(h * LANES, (h + 1) * LANES) for h in range(HPG * g, HPG * (g + 1))]
                 for g in range(N_KV)]

    def win_bias(j):
        d = tq - (j * KV_TILE + rowkv)
        return jnp.where((d >= 0) & (d < WINDOW), 0.0, NEG)

    def slc_bias(g):
        def bias(j):
            nb = KV_TILE // SLC_BLK
            tile = jnp.concatenate([jnp.broadcast_to(sel_s[g, pl.ds(nb * j + u, 1), :], (SLC_BLK, TQ))
                                    for u in range(nb)], axis=0)
            return jnp.where(j * KV_TILE + rowkv <= tq, tile, NEG)
        return bias

    for g in range(N_KV):
        kcg = kc[0, 0, g]
        vcg = vct[0, 0, g]
        cols = head_cols[g]
        ss = [jnp.where(cm, lax.dot_general(kcg, qn[:, c], NT_DIMS, preferred_element_type=F32), NEG)
              for c in cols]
        es = [jnp.exp2(s - jnp.max(s, axis=0, keepdims=True)) for s in ss]
        ps = [jnp.where(cm, e * (1.0 / jnp.sum(e, axis=0, keepdims=True)), 0.0) for e in es]
        for hh, p in enumerate(ps):
            acc_s[cmp_slot + HPG * g + hh] = jnp.dot(vcg, p.astype(BF16), preferred_element_type=F32)
        psum = (ps[0] + ps[1]) + (ps[2] + ps[3])

        p_hi = psum.astype(BF16)
        p_lo = (psum - p_hi.astype(F32)).astype(BF16)
        imp = (jnp.dot(ovt[...], p_hi, preferred_element_type=F32)
               + jnp.dot(ovt[...], p_lo, preferred_element_type=F32))
        imp = jnp.where(force, FORCE, jnp.where(valid, imp, -FORCE))
        imp_s[...] = imp
        bands = [imp[8 * r:8 * r + 8] for r in range(N_SLC // 8)]
        ranks = [jnp.zeros((8, TQ), F32) for _ in bands]
        for j in range(N_SLC):
            vj = imp_s[j:j + 1, :]
            for r, x in enumerate(bands):
                if r < j // 8:
                    hit = vj > x
                elif r > j // 8:
                    hit = vj >= x
                else:
                    hit = (vj > x) | ((vj == x) & (sub8 > j % 8))
                ranks[r] = ranks[r] + jnp.where(hit, 1.0, 0.0)
        sel_s[g] = jnp.where(jnp.concatenate(ranks, axis=0) < SLC_TOPN, 0.0, NEG)

    qn2 = _query_norms(qr)
    _flash_heads(qr, qn2, slc_slot, [(ks.at[0, g], vst.at[0, g], head_cols[g], slc_bias(g), kn_s[g:g + 1, :])
                                for g in range(N_KV)], 0, n_kv, m_s, l_s, acc_s)
    _flash_heads(qr, qn2, win_slot, [(kw.at[0, g], vwt.at[0, g], head_cols[g], win_bias,
                                 kn_s[N_KV + g:N_KV + g + 1, :]) for g in range(N_KV)],
                 jnp.maximum(n_kv - 1 - WINDOW // KV_TILE, 0), n_kv, m_s, l_s, acc_s)
    for h in range(N_HEADS):
        c = slice(h * LANES, (h + 1) * LANES)
        tot = (sg[h:h + 1] * acc_s[cmp_slot + h]
               + sg[N_HEADS + h:N_HEADS + h + 1] * _flash_out(slc_slot + h, l_s, acc_s)
               + sg[2 * N_HEADS + h:2 * N_HEADS + h + 1] * _flash_out(win_slot + h, l_s, acc_s))
        yb[:, c] = (tot.T * _silu(bg[:, c])).astype(BF16)


def _nsa(z, qn, qr, cmp_, cmpt, ks, vst, kw, vwt, ovt):
    nq = SEQ // TQ
    qspec = pl.BlockSpec((TQ, GROUP_W), lambda b, i: (b * nq + i, 0))
    kspec = pl.BlockSpec((1, N_KV, SEQ, LANES), lambda b, i: (b, 0, 0, 0))
    vtspec = pl.BlockSpec((1, N_KV, SEQ // KV_TILE, LANES, KV_TILE), lambda b, i: (b, 0, 0, 0, 0))
    return pl.pallas_call(
        _nsa_body,
        grid=(BATCH, nq),
        in_specs=[qspec, qspec,
                  pl.BlockSpec((1, 1, N_KV, N_CMP_PAD, LANES), lambda b, i: (b, 0, 0, 0, 0)),
                  pl.BlockSpec((1, 1, N_KV, LANES, N_CMP_PAD), lambda b, i: (b, 1, 0, 0, 0)),
                  kspec, vtspec, kspec, vtspec,
                  pl.BlockSpec((TQ, LANES), lambda b, i: (b * nq + i, COL_B_GATE // LANES)),
                  pl.BlockSpec((TQ, GROUP_W), lambda b, i: (b * nq + i, COL_B_G // GROUP_W)),
                  pl.BlockSpec((N_SLC, N_CMP_PAD), lambda b, i: (0, 0))],
        out_specs=qspec,
        out_shape=jax.ShapeDtypeStruct((BATCH * SEQ, GROUP_W), BF16),
        scratch_shapes=[pltpu.VMEM((N_SLC, TQ), F32), pltpu.VMEM((N_KV, N_SLC, TQ), F32),
                        pltpu.VMEM((3 * N_HEADS, TQ), F32), pltpu.VMEM((3 * N_HEADS, TQ), F32),
                        pltpu.VMEM((3 * N_HEADS, HEAD_DIM, TQ), F32), pltpu.VMEM((8, TQ), F32)],
        compiler_params=_params(2),
        name="nsa_attn",
    )(qn, qr, cmp_, cmpt, ks, vst, kw, vwt, z, z, ovt)


def _dsa_body(qd, kd, vdt, iq, ik, iw, cg, yc, sc_s, m_s, l_s, acc_s, kn_s):
    i = pl.program_id(1)

    @pl.when(i == 0)
    def _():
        for g in range(N_KV):
            kn_s[g:g + 1, :] = _key_norm_bound(kd.at[0, g])

    tq = i * TQ + lax.broadcasted_iota(I32, (1, TQ), 1)
    n_kv = i + 1
    iwt = iw[...].T[0:IDX_HEADS, :] * IW_SCALE
    rowkv = lax.broadcasted_iota(I32, (KV_TILE, TQ), 0)
    kf = float(DSA_TOPK)

    def score_tile(j):
        r0 = pl.multiple_of(j * KV_TILE, KV_TILE)
        ikc = ik[0, pl.ds(r0, KV_TILE), :]
        sc = jnp.zeros((KV_TILE, TQ), F32)
        for h in range(IDX_HEADS):
            r = lax.dot_general(ikc, iq[:, 2 * h * LANES:(2 * h + 2) * LANES], NT_DIMS,
                                preferred_element_type=F32)
            sc = sc + jnp.maximum(r, 0.0) * iwt[h:h + 1, :]
        return r0, sc

    def fill(j, carry):
        mn, mx = carry
        r0, sc = score_tile(j)
        sc_s[pl.ds(r0, KV_TILE), :] = sc
        return (jnp.minimum(mn, _fold8(sc, jnp.minimum)), jnp.maximum(mx, _fold8(sc, jnp.maximum)))

    def fill2(jj, carry):
        return fill(2 * jj + 1, fill(2 * jj, carry))

    half = lax.shift_right_logical(i, 1)
    mm = lax.fori_loop(0, half, fill2, (jnp.full((8, TQ), jnp.inf, F32), jnp.full((8, TQ), -jnp.inf, F32)))
    mn8, mx8 = lax.fori_loop(2 * half, i, fill, mm)
    r0, sc = score_tile(i)
    ok = r0 + rowkv <= tq
    sc_s[pl.ds(r0, KV_TILE), :] = jnp.where(ok, sc, -jnp.inf)
    lo0 = jnp.min(jnp.minimum(mn8, _fold8(jnp.where(ok, sc, jnp.inf), jnp.minimum)), axis=0, keepdims=True)
    hi0 = jnp.max(jnp.maximum(mx8, _fold8(jnp.where(ok, sc, -jnp.inf), jnp.maximum)), axis=0, keepdims=True)

    def sweep(fn, n_out):
        def body(c, accs):
            r0 = pl.multiple_of(c * KV_TILE, KV_TILE)
            inds = fn(sc_s[pl.ds(r0, KV_TILE), :], r0 + rowkv)
            return tuple(a + _fold8(jnp.where(ind, 1.0, 0.0)) for a, ind in zip(accs, inds))
        accs = lax.fori_loop(0, n_kv, body, tuple(jnp.zeros((8, TQ), F32) for _ in range(n_out)))
        return tuple(jnp.sum(a, axis=0, keepdims=True) for a in accs)

    def data_span(lo, hi):
        def body(c, carry):
            mn, mx = carry
            x = sc_s[pl.ds(pl.multiple_of(c * KV_TILE, KV_TILE), KV_TILE), :]
            mn = jnp.minimum(mn, _fold8(jnp.where(x >= lo, x, jnp.inf), jnp.minimum))
            mx = jnp.maximum(mx, _fold8(jnp.where(x <= hi, x, -jnp.inf), jnp.maximum))
            return mn, mx
        mn, mx = lax.fori_loop(0, n_kv, body, (jnp.full((8, TQ), jnp.inf, F32), jnp.full((8, TQ), -jnp.inf, F32)))
        return jnp.min(mn, axis=0, keepdims=True), jnp.max(mx, axis=0, keepdims=True)

    nvalid = (tq + 1).astype(F32)
    act0 = jnp.where(nvalid > kf, 1.0, 0.0)

    def bisect(_, c):
        lo, hi, clo, act = c
        on = act > 0.0
        mid = lo + (hi - lo) * 0.5
        exh = (mid <= lo) | (mid >= hi)
        p = jnp.where(exh, hi, mid)
        (cnt,) = sweep(lambda x, srow: (x >= p,), 1)
        up = on & (cnt >= kf)
        dn = on & (cnt < kf)
        clo = jnp.where(up, cnt, clo)
        act = jnp.where(on & (clo != kf) & jnp.logical_not(exh), 1.0, 0.0)
        return jnp.where(up, p, lo), jnp.where(dn, p, hi), clo, act

    def snap(c):
        lo, hi, clo, act = c
        on = act > 0.0
        dmin, dmax = data_span(lo, hi)
        act = jnp.where(on & (dmin < dmax), 1.0, 0.0)
        return jnp.where(on, dmin, lo), jnp.where(on, dmax, hi), clo, act

    def cond(c):
        return c[0] > 0.0

    def round_(c):
        state = lax.fori_loop(0, SNAP_EVERY - 1, bisect, snap(c[1:]))
        return (jnp.sum(state[3]),) + state

    state = lax.fori_loop(0, SNAP_FIRST, bisect, (lo0, hi0, nvalid, act0))
    _, thr, _, _, _ = lax.while_loop(cond, round_, (jnp.sum(state[3]),) + state)

    cnt_g, cnt_e = sweep(lambda x, srow: (x > thr, x == thr), 2)
    need = kf - cnt_g
    excess = (cnt_g + cnt_e) > kf

    def tie_break():
        def it(_, c):
            jl, jh = c
            jm = lax.shift_right_arithmetic(jl + jh, 1)
            (cnt,) = sweep(lambda x, srow: ((x == thr) & (srow <= jm),), 1)
            ge = cnt >= need
            return jnp.where(ge, jl, jm), jnp.where(ge, jm, jh)
        _, jh = lax.fori_loop(0, 12, it, (jnp.full((1, TQ), -1, I32), jnp.full((1, TQ), SEQ - 1, I32)))
        return jnp.where(excess, jh, SEQ)

    last = lax.cond(jnp.sum(jnp.where(excess, 1.0, 0.0)) > 0.0, tie_break,
                    lambda: jnp.full((1, TQ), SEQ, I32))

    def to_bias(c, carry):
        r0 = pl.multiple_of(c * KV_TILE, KV_TILE)
        x = sc_s[pl.ds(r0, KV_TILE), :]
        sel = (x > thr) | ((x == thr) & (r0 + rowkv <= last))
        sc_s[pl.ds(r0, KV_TILE), :] = jnp.where(sel, 0.0, NEG)
        return carry
    lax.fori_loop(0, n_kv, to_bias, 0)

    def bias(j):
        return sc_s[pl.ds(pl.multiple_of(j * KV_TILE, KV_TILE), KV_TILE), :]

    groups = [(kd.at[0, g], vdt.at[0, g], [slice(h * LANES, (h + 1) * LANES)
                                           for h in range(HPG * g, HPG * (g + 1))], bias, kn_s[g:g + 1, :])
              for g in range(N_KV)]
    _flash_heads(qd, _query_norms(qd), 0, groups, 0, n_kv, m_s, l_s, acc_s)
    for h in range(N_HEADS):
        c = slice(h * LANES, (h + 1) * LANES)
        yc[:, c] = (_flash_out(h, l_s, acc_s).T * _silu(cg[:, c])).astype(BF16)


def _dsa(z, qd, kd, vdt, iq, ik):
    nq = SEQ // TQ
    qspec = pl.BlockSpec((TQ, GROUP_W), lambda b, i: (b * nq + i, 0))
    return pl.pallas_call(
        _dsa_body,
        grid=(BATCH, nq),
        in_specs=[qspec,
                  pl.BlockSpec((1, N_KV, SEQ, LANES), lambda b, i: (b, 0, 0, 0)),
                  pl.BlockSpec((1, N_KV, SEQ // KV_TILE, LANES, KV_TILE), lambda b, i: (b, 0, 0, 0, 0)),
                  pl.BlockSpec((TQ, 2 * LANES * IDX_HEADS), lambda b, i: (b * nq + i, 0)),
                  pl.BlockSpec((1, SEQ, 2 * LANES), lambda b, i: (b, 0, 0)),
                  pl.BlockSpec((TQ, LANES), lambda b, i: (b * nq + i, COL_C_IW // LANES)),
                  pl.BlockSpec((TQ, GROUP_W), lambda b, i: (b * nq + i, COL_C_G // GROUP_W))],
        out_specs=qspec,
        out_shape=jax.ShapeDtypeStruct((BATCH * SEQ, GROUP_W), BF16),
        scratch_shapes=[pltpu.VMEM((SEQ, TQ), F32), pltpu.VMEM((N_HEADS, TQ), F32),
                        pltpu.VMEM((N_HEADS, TQ), F32), pltpu.VMEM((N_HEADS, HEAD_DIM, TQ), F32),
                        pltpu.VMEM((8, TQ), F32)],
        compiler_params=_params(2),
        name="dsa_attn",
    )(qd, kd, vdt, iq, ik, z, z)


def _segment_moves():
    src = {}
    o = 0
    names = ("a_x", "a_g", "b_q", "b_g", "b_kv", "b_gate", "c_q", "c_g", "c_kv", "c_iq", "c_ik", "c_iw",
             "d_in", "d_b", "d_c", "d_g")
    for n, s in zip(names, SPLIT_SIZES):
        src[n] = (o, s)
        o += s
    dst = dict(a_x=COL_A_X, a_g=COL_A_G, b_q=COL_B_Q, b_g=COL_B_G, c_q=COL_C_Q, c_g=COL_C_G,
               d_in=COL_D_IN, d_b=COL_D_B, d_c=COL_D_C, d_g=COL_D_G, b_kv=COL_B_KV, c_kv=COL_C_KV,
               c_iq=COL_C_IQ, b_gate=COL_B_GATE, c_ik=COL_C_IK, c_iw=COL_C_IW)
    return [(src[n][0], dst[n], src[n][1]) for n in names]


def _wprep_body(w_ref, o_ref):
    o_ref[COL_B_GATE:PROJ_W, :] = jnp.zeros((PROJ_W - COL_B_GATE, TR_W), BF16)
    for s, d, width in _segment_moves():
        for c in range(0, width, GROUP_W):
            n = min(GROUP_W, width - c)
            o_ref[d + c:d + c + n, :] = w_ref[s + c:s + c + n, :].astype(BF16)


def _proj_weight(wt, layer):
    _, n, k = wt.shape
    return pl.pallas_call(
        _wprep_body,
        grid=(k // TR_W,),
        in_specs=[pl.BlockSpec((None, n, TR_W), lambda i: (layer, 0, i))],
        out_specs=pl.BlockSpec((PROJ_W, TR_W), lambda i: (0, i)),
        out_shape=jax.ShapeDtypeStruct((PROJ_W, k), BF16),
        compiler_params=_params(1),
        name="w_in_layout",
    )(wt)


def _rope_tables(head_dim):
    r = head_dim // 4
    half = r // 2
    pos = jnp.arange(SEQ, dtype=jnp.int32)
    inv = ROPE_THETA ** (-jnp.arange(half, dtype=F32) * 2.0 / r)
    ang = pos.astype(F32)[:, None] * inv[None, :]
    cos, sin = jnp.cos(ang), jnp.sin(ang)
    one = jnp.ones((SEQ, head_dim - r), F32)
    zero = jnp.zeros((SEQ, head_dim - r), F32)
    zh = jnp.zeros((SEQ, half), F32)
    reps = LANES // head_dim
    return tuple(jnp.tile(t, (1, reps)) for t in (
        jnp.concatenate([cos, cos, one], axis=1),
        jnp.concatenate([-sin, zh, zero], axis=1),
        jnp.concatenate([zh, sin, zero], axis=1)))


def _overlap_t():
    cs = jnp.arange(N_CMP_PAD) * CMP_STRIDE
    ss = jnp.arange(N_SLC) * SLC_BLK
    ov = jnp.clip(jnp.minimum(cs[None, :] + CMP_LEN, ss[:, None] + SLC_BLK)
                  - jnp.maximum(cs[None, :], ss[:, None]), 0, None).astype(F32) / CMP_LEN
    return jnp.where(jnp.arange(N_CMP_PAD)[None, :] < N_CMP_PAD - 1, ov, 0.0).astype(BF16)


def kernel(x, norm_g, w_in, w_out, lru_conv_w, lru_conv_b, lru_wa, lru_ba, lru_wx, lru_bx, lru_lambda,
           nsa_q_gain, nsa_k_gain, cmp_pe_k, cmp_w1_k, cmp_w2_k, cmp_pe_v, cmp_w1_v, cmp_w2_v,
           dsa_q_gain, dsa_k_gain, sc_conv_w):
    b, s, d = x.shape
    assert (b, s, d) == (BATCH, SEQ, D_MODEL)
    tabs128 = _rope_tables(HEAD_DIM)
    tabs64 = _rope_tables(IDX_DIM)
    ovt = _overlap_t()
    x2 = x.reshape(b * s, d)
    w_in_t = jnp.swapaxes(w_in, 1, 2)

    def row(v):
        return v.reshape(1, -1)

    for l in range(DEPTH):
        z = _inproj(x2, row(norm_g[l]), _proj_weight(w_in_t, l))
        ya, yd = _mix_ad(z, lru_conv_w[l], row(lru_conv_b[l]), lru_wa[l].astype(BF16), row(lru_ba[l]),
                         lru_wx[l].astype(BF16), row(lru_bx[l]), row(lru_lambda[l]), sc_conv_w[l])
        qn, qr, ks, vst, kw, vwt, qd, kd, vdt, iq, ik = _prep(
            z, tabs128, tabs64, row(nsa_q_gain[l]), row(nsa_k_gain[l]), row(dsa_q_gain[l]), row(dsa_k_gain[l]))
        cmp_, cmpt = _compress(z, jnp.stack([cmp_pe_k[l], cmp_pe_v[l]]),
                               jnp.stack([cmp_w1_k[l], cmp_w1_v[l]]).astype(BF16),
                               jnp.stack([cmp_w2_k[l], cmp_w2_v[l]]).astype(BF16), row(nsa_k_gain[l]))
        yb = _nsa(z, qn, qr, cmp_, cmpt, ks, vst, kw, vwt, ovt)
        yc = _dsa(z, qd, kd, vdt, iq, ik)
        x2 = _outproj((ya, yb, yc, yd), w_out, l, x2)
    return x2.reshape(b, s, d)
```

```python
import functools

import jax
import jax.numpy as jnp
from jax import lax
from jax.experimental import pallas as pl
from jax.experimental.pallas import tpu as pltpu

F32 = jnp.float32
BF16 = jnp.bfloat16
I32 = jnp.int32

D_MODEL = 4096
BATCH = 2
SEQ = 4096
DEPTH = 2
GROUP_W = D_MODEL // 4
HEAD_DIM = 128
N_HEADS = GROUP_W // HEAD_DIM
N_KV = N_HEADS // 4
HPG = N_HEADS // N_KV
ROPE_THETA = 500000.0
NORM_EPS = 1e-6
NEG = -1e30
FORCE = 1e6
LRU_BLOCKS = N_HEADS
LRU_C = 8.0
CONV_A = 4
CONV_D = 3
CMP_LEN = 32
CMP_STRIDE = 16
SLC_BLK = 64
SLC_TOPN = 16
WINDOW = 512
IDX_HEADS = 8
IDX_DIM = 64
DSA_TOPK = 256
SCALE = HEAD_DIM ** -0.5
Q_PRESCALE = SCALE * 1.4426950408889634
IW_SCALE = IDX_HEADS ** -0.5 * IDX_DIM ** -0.5

SPLIT_SIZES = (
    GROUP_W, GROUP_W,
    N_HEADS * HEAD_DIM, GROUP_W, 6 * N_KV * HEAD_DIM, 3 * N_HEADS,
    N_HEADS * HEAD_DIM, GROUP_W, 2 * N_KV * HEAD_DIM,
    IDX_HEADS * IDX_DIM, IDX_DIM, IDX_HEADS,
    GROUP_W, GROUP_W, GROUP_W, GROUP_W,
)

LANES = 128
COL_A_X, COL_A_G, COL_B_Q, COL_B_G, COL_C_Q, COL_C_G = 0, 1024, 2048, 3072, 4096, 5120
COL_D_IN, COL_D_B, COL_D_C, COL_D_G = 6144, 7168, 8192, 9216
COL_B_KV, COL_C_KV, COL_C_IQ = 10240, 11776, 12288
COL_B_GATE, COL_C_IK, COL_C_IW = 12800, 12928, 13056
PROJ_W = 13312

N_CMP_PAD = SEQ // CMP_STRIDE
N_SLC = SEQ // SLC_BLK
KV_TILE = 256
M_FLOOR = -1e20
BOUND_SLACK = 1.01
BOUND_LIMIT = 60.0
SNAP_FIRST, SNAP_EVERY = 16, 2

VMEM_LIMIT_BYTES = 56 * 1024 * 1024
INPROJ_VMEM_LIMIT_BYTES = 60 * 1024 * 1024

TM_IN, TN_IN = 1024, 1024
TM_OUT, TN_OUT = 1024, 512
T_TOK = 256
TR_W = 128
TQ = 256

NT_DIMS = (((1,), (1,)), ((), ()))


def _params(n_axes, vmem_limit_bytes=VMEM_LIMIT_BYTES):
    return pltpu.CompilerParams(dimension_semantics=("arbitrary",) * n_axes,
                                vmem_limit_bytes=vmem_limit_bytes)


def _sigmoid(x):
    return 0.5 * jnp.tanh(0.5 * x) + 0.5


def _silu(x):
    return x * _sigmoid(x)


def _fold8(x, op=jnp.add):
    parts = [x[r:r + 8] for r in range(0, x.shape[0], 8)]
    while len(parts) > 1:
        nxt = [op(parts[a], parts[a + 1]) for a in range(0, len(parts) - 1, 2)]
        if len(parts) % 2:
            nxt.append(parts[-1])
        parts = nxt
    return parts[0]


def _inproj_body(x_hbm, g_ref, w_ref, o_ref, x_buf, h_ref, sem):
    i = pl.program_id(0)
    n_row_tiles = pl.num_programs(0)

    def x_copy(tile):
        return pltpu.make_async_copy(x_hbm.at[pl.ds(pl.multiple_of(tile * TM_IN, TM_IN), TM_IN), :], x_buf, sem)

    @pl.when(pl.program_id(1) == 0)
    def _():
        @pl.when(i == 0)
        def _():
            x_copy(0).start()

        x_copy(i).wait()

        def rows(r, carry):
            r0 = pl.multiple_of(r * 64, 64)
            x = x_buf[pl.ds(r0, 64), :]
            ms = jnp.mean(x * x, axis=-1, keepdims=True)
            h_ref[pl.ds(r0, 64), :] = ((x * lax.rsqrt(ms + NORM_EPS)) * g_ref[...]).astype(BF16)
            return carry
        lax.fori_loop(0, TM_IN // 64, rows, 0)

        @pl.when(i + 1 < n_row_tiles)
        def _():
            x_copy(i + 1).start()

    o_ref[...] = lax.dot_general(h_ref[...], w_ref[...], NT_DIMS, preferred_element_type=F32)


def _inproj(x2, g, wt):
    m = x2.shape[0]
    return pl.pallas_call(
        _inproj_body,
        grid=(m // TM_IN, PROJ_W // TN_IN),
        in_specs=[pl.BlockSpec(memory_space=pl.ANY),
                  pl.BlockSpec((1, D_MODEL), lambda i, j: (0, 0)),
                  pl.BlockSpec((TN_IN, D_MODEL), lambda i, j: (j, 0))],
        out_specs=pl.BlockSpec((TM_IN, TN_IN), lambda i, j: (i, j)),
        out_shape=jax.ShapeDtypeStruct((m, PROJ_W), F32),
        scratch_shapes=[pltpu.VMEM((TM_IN, D_MODEL), F32), pltpu.VMEM((TM_IN, D_MODEL), BF16),
                        pltpu.SemaphoreType.DMA(())],
        compiler_params=_params(2, INPROJ_VMEM_LIMIT_BYTES),
        name="inproj",
    )(x2, g, wt)


def _outproj_body(ya, yb, yc, yd, wa, wb, wc, wd, x_ref, o_ref):
    acc = jnp.dot(ya[...], wa[...].astype(BF16), preferred_element_type=F32)
    acc = acc + jnp.dot(yb[...], wb[...].astype(BF16), preferred_element_type=F32)
    acc = acc + jnp.dot(yc[...], wc[...].astype(BF16), preferred_element_type=F32)
    acc = acc + jnp.dot(yd[...], wd[...].astype(BF16), preferred_element_type=F32)
    o_ref[...] = x_ref[...] + acc


def _outproj(ys, w, layer, x2):
    m = x2.shape[0]
    yspec = pl.BlockSpec((TM_OUT, GROUP_W), lambda i, j: (i, 0))
    wspecs = [pl.BlockSpec((None, GROUP_W, TN_OUT), functools.partial(lambda i, j, k: (layer, k, j), k=k))
              for k in range(4)]
    xspec = pl.BlockSpec((TM_OUT, TN_OUT), lambda i, j: (i, j))
    return pl.pallas_call(
        _outproj_body,
        grid=(m // TM_OUT, D_MODEL // TN_OUT),
        in_specs=[yspec] * 4 + wspecs + [xspec],
        out_specs=xspec,
        out_shape=jax.ShapeDtypeStruct((m, D_MODEL), F32),
        compiler_params=_params(2),
        name="outproj",
    )(*ys, w, w, w, w, x2)


def _mix_ad_body(ax, ag, din, db, dc, dg, cw, cb, wa, ba, wx, bx, lam, scw,
                 ya, yd, xbuf, vbuf, a_s, b_s, hst):
    t = T_TOK

    @pl.when(pl.program_id(1) == 0)
    def _():
        xbuf[0:8, :] = jnp.zeros((8, GROUP_W), F32)
        vbuf[0:8, :] = jnp.zeros((8, GROUP_W), F32)
        hst[...] = jnp.zeros((8, GROUP_W), F32)

    xbuf[8:8 + t, :] = ax[...]
    for blk in range(LRU_BLOCKS):
        c = slice(blk * LANES, (blk + 1) * LANES)
        u = (cw[0:1, c] * xbuf[5:5 + t, c] + cw[1:2, c] * xbuf[6:6 + t, c]
             + cw[2:3, c] * xbuf[7:7 + t, c] + cw[3:4, c] * xbuf[8:8 + t, c]) + cb[:, c]
        ub = u.astype(BF16)
        r = _sigmoid(jnp.dot(ub, wa[blk], preferred_element_type=F32) + ba[:, c])
        ig = _sigmoid(jnp.dot(ub, wx[blk], preferred_element_type=F32) + bx[:, c])
        nl = -lam[:, c]
        sp = jnp.maximum(nl, 0.0) + jnp.log1p(jnp.exp(-jnp.abs(nl)))
        log_a = (-LRU_C * r) * sp
        a = jnp.exp(log_a)
        a_s[:, c] = a
        em1 = jnp.tanh(log_a) * (a * a + 1.0)
        b_s[:, c] = jnp.sqrt(-em1) * (ig * u)
    xbuf[0:8, :] = xbuf[t:t + 8, :]

    row = lax.broadcasted_iota(I32, (8, GROUP_W), 0)

    def group(gi, hprev):
        r0 = pl.multiple_of(gi * 8, 8)
        a = a_s[pl.ds(r0, 8), :]
        b = b_s[pl.ds(r0, 8), :]
        for d in (1, 2, 4):
            ok = row >= d
            b = jnp.where(ok, a * pltpu.roll(b, d, axis=0) + b, b)
            a = jnp.where(ok, a * pltpu.roll(a, d, axis=0), a)
        h = a * hprev + b
        b_s[pl.ds(r0, 8), :] = h
        return jnp.broadcast_to(h[7:8, :], (8, GROUP_W))

    hst[...] = lax.fori_loop(0, t // 8, group, hst[...])

    vbuf[8:8 + t, :] = dc[...] * din[...]
    for blk in range(LRU_BLOCKS):
        c = slice(blk * LANES, (blk + 1) * LANES)
        ya[:, c] = (b_s[:, c] * _silu(ag[:, c])).astype(BF16)
        conv = (scw[0:1, c] * vbuf[6:6 + t, c] + scw[1:2, c] * vbuf[7:7 + t, c]
                + scw[2:3, c] * vbuf[8:8 + t, c])
        yd[:, c] = ((db[:, c] * conv) * _silu(dg[:, c])).astype(BF16)
    vbuf[0:8, :] = vbuf[t:t + 8, :]


def _mix_ad(z, cw, cb, wa, ba, wx, bx, lam, scw):
    nt = SEQ // T_TOK

    def zspec(col):
        return pl.BlockSpec((T_TOK, GROUP_W), lambda b, i: (b * nt + i, col // GROUP_W))

    def full(a):
        return pl.BlockSpec(a.shape, lambda b, i: (0,) * a.ndim)

    small = (cw, cb, wa, ba, wx, bx, lam, scw)
    yspec = pl.BlockSpec((T_TOK, GROUP_W), lambda b, i: (b * nt + i, 0))
    return pl.pallas_call(
        _mix_ad_body,
        grid=(BATCH, nt),
        in_specs=[zspec(COL_A_X), zspec(COL_A_G), zspec(COL_D_IN), zspec(COL_D_B), zspec(COL_D_C),
                  zspec(COL_D_G)] + [full(a) for a in small],
        out_specs=[yspec, yspec],
        out_shape=[jax.ShapeDtypeStruct((BATCH * SEQ, GROUP_W), BF16)] * 2,
        scratch_shapes=[pltpu.VMEM((T_TOK + 8, GROUP_W), F32), pltpu.VMEM((T_TOK + 8, GROUP_W), F32),
                        pltpu.VMEM((T_TOK, GROUP_W), F32), pltpu.VMEM((T_TOK, GROUP_W), F32),
                        pltpu.VMEM((8, GROUP_W), F32)],
        compiler_params=_params(2),
        name="mix_ad",
    )(z, z, z, z, z, z, *small)


def _rms_head(x, gain):
    ms = jnp.mean(x * x, axis=-1, keepdims=True)
    return (x * lax.rsqrt(ms + NORM_EPS)) * gain


def _rope(y, cos, nsin, psin, half):
    return y * cos + pltpu.roll(y, LANES - half, axis=1) * nsin + pltpu.roll(y, half, axis=1) * psin


def _store_vt(dst, g, v):
    for cc in range(T_TOK // KV_TILE):
        dst[0, g, cc] = v[cc * KV_TILE:(cc + 1) * KV_TILE, :].T.astype(BF16)


def _rms_head_mxu(x, gain):
    ssq = jnp.dot((x * x).astype(BF16), jnp.ones((LANES, LANES), BF16), preferred_element_type=F32)
    return (x * lax.rsqrt(ssq * (1.0 / LANES) + NORM_EPS)) * gain


def _prep_body(bq, bkv_s, bkv_w, cq, ckv, ciq, cik, c128, n128, p128, c64, n64, p64,
               qgn, kgn, qgd, kgd,
               qn_o, qr_o, ks_o, vst_o, kw_o, vwt_o, qd_o, kd_o, vdt_o, iq_o, ik_o):
    r128 = (c128[...], n128[...], p128[...], HEAD_DIM // 8)
    r64 = (c64[...], n64[...], p64[...], IDX_DIM // 8)
    for h in range(N_HEADS):
        c = slice(h * LANES, (h + 1) * LANES)
        y = _rms_head_mxu(bq[:, c], qgn[...])
        qn_o[:, c] = (y * Q_PRESCALE).astype(BF16)
        qr_o[:, c] = (_rope(y, *r128) * Q_PRESCALE).astype(BF16)
        qd_o[:, c] = (_rope(_rms_head_mxu(cq[:, c], qgd[...]), *r128) * Q_PRESCALE).astype(BF16)
    for g in range(N_KV):
        ck = slice(g * LANES, (g + 1) * LANES)
        cv = slice((N_KV + g) * LANES, (N_KV + g + 1) * LANES)
        ks_o[0, g] = _rope(_rms_head_mxu(bkv_s[:, ck], kgn[...]), *r128).astype(BF16)
        _store_vt(vst_o, g, bkv_s[:, cv])
        kw_o[0, g] = _rope(_rms_head_mxu(bkv_w[:, ck], kgn[...]), *r128).astype(BF16)
        _store_vt(vwt_o, g, bkv_w[:, cv])
        kd_o[0, g] = _rope(_rms_head_mxu(ckv[:, ck], kgd[...]), *r128).astype(BF16)
        _store_vt(vdt_o, g, ckv[:, cv])

    low = lax.broadcasted_iota(I32, (T_TOK, LANES), 1) < IDX_DIM
    ik = _rope(cik[...], *r64)
    hi = ik.astype(BF16).astype(F32)
    ik_o[0, :, 0:LANES] = (hi + pltpu.roll(ik - hi, IDX_DIM, axis=1)).astype(BF16)
    ik_o[0, :, LANES:2 * LANES] = hi.astype(BF16)
    for v in range(IDX_HEADS // 2):
        x = _rope(ciq[:, v * LANES:(v + 1) * LANES], *r64)
        for half in range(2):
            h = 2 * v + half
            xs = x if half == 0 else pltpu.roll(x, IDX_DIM, axis=1)
            xh = jnp.where(low, xs, 0.0)
            hi = xh.astype(BF16).astype(F32)
            iq_o[:, 2 * h * LANES:(2 * h + 1) * LANES] = (hi + pltpu.roll(hi, IDX_DIM, axis=1)).astype(BF16)
            iq_o[:, (2 * h + 1) * LANES:(2 * h + 2) * LANES] = (xh - hi).astype(BF16)


def _prep(z, tabs128, tabs64, qgn, kgn, qgd, kgd):
    nt = SEQ // T_TOK

    def zspec(col, width):
        return pl.BlockSpec((T_TOK, width), lambda b, i: (b * nt + i, col // width))

    tab = pl.BlockSpec((T_TOK, LANES), lambda b, i: (i, 0))
    gain = pl.BlockSpec((1, LANES), lambda b, i: (0, 0))
    qspec = pl.BlockSpec((T_TOK, GROUP_W), lambda b, i: (b * nt + i, 0))
    kspec = pl.BlockSpec((1, N_KV, T_TOK, LANES), lambda b, i: (b, 0, i, 0))
    vtspec = pl.BlockSpec((1, N_KV, T_TOK // KV_TILE, LANES, KV_TILE), lambda b, i: (b, 0, i, 0, 0))
    qshape = jax.ShapeDtypeStruct((BATCH * SEQ, GROUP_W), BF16)
    kshape = jax.ShapeDtypeStruct((BATCH, N_KV, SEQ, LANES), BF16)
    vtshape = jax.ShapeDtypeStruct((BATCH, N_KV, SEQ // KV_TILE, LANES, KV_TILE), BF16)
    return pl.pallas_call(
        _prep_body,
        grid=(BATCH, nt),
        in_specs=[zspec(COL_B_Q, GROUP_W), zspec(COL_B_KV + 512, 512), zspec(COL_B_KV + 1024, 512),
                  zspec(COL_C_Q, GROUP_W), zspec(COL_C_KV, 512), zspec(COL_C_IQ, 512), zspec(COL_C_IK, LANES)]
                 + [tab] * 6 + [gain] * 4,
        out_specs=[qspec, qspec, kspec, vtspec, kspec, vtspec, qspec, kspec, vtspec,
                   pl.BlockSpec((T_TOK, 2 * LANES * IDX_HEADS), lambda b, i: (b * nt + i, 0)),
                   pl.BlockSpec((1, T_TOK, 2 * LANES), lambda b, i: (b, i, 0))],
        out_shape=[qshape, qshape, kshape, vtshape, kshape, vtshape, qshape, kshape, vtshape,
                   jax.ShapeDtypeStruct((BATCH * SEQ, 2 * LANES * IDX_HEADS), BF16),
                   jax.ShapeDtypeStruct((BATCH, SEQ, 2 * LANES), BF16)],
        compiler_params=_params(2),
        name="attn_prep",
    )(z, z, z, z, z, z, z, *tabs128, *tabs64, qgn, kgn, qgd, kgd)


def _cmp_body(zc, pe, w1, w2, kg, o, ot, kbuf):
    kbuf[0:SEQ, :] = zc[...]
    kbuf[SEQ:SEQ + LANES, :] = jnp.zeros((LANES, LANES), F32)
    acc = jnp.zeros((N_CMP_PAD, LANES), F32)
    for l in range(CMP_LEN):
        rows = kbuf[pl.ds(l, N_CMP_PAD, stride=CMP_STRIDE), :] + pe[0, l:l + 1, :]
        acc = acc + jnp.dot(rows.astype(BF16), w1[0, l], preferred_element_type=F32)
    out = jnp.dot(_silu(acc).astype(BF16), w2[0], preferred_element_type=F32)
    res = jnp.where(pl.program_id(1) == 0, _rms_head(out, kg[...]), out)
    o[0, 0, 0] = res.astype(BF16)
    ot[0, 0, 0] = res.T.astype(BF16)


def _compress(z, pe, w1, w2, kg):
    base = COL_B_KV // LANES
    return pl.pallas_call(
        _cmp_body,
        grid=(BATCH, 2, N_KV),
        in_specs=[pl.BlockSpec((SEQ, LANES), lambda b, kv, g: (b, base + kv * N_KV + g)),
                  pl.BlockSpec((1, CMP_LEN, LANES), lambda b, kv, g: (kv, 0, 0)),
                  pl.BlockSpec((1, CMP_LEN, LANES, LANES), lambda b, kv, g: (kv, 0, 0, 0)),
                  pl.BlockSpec((1, LANES, LANES), lambda b, kv, g: (kv, 0, 0)),
                  pl.BlockSpec((1, LANES), lambda b, kv, g: (0, 0))],
        out_specs=[pl.BlockSpec((1, 1, 1, N_CMP_PAD, LANES), lambda b, kv, g: (b, kv, g, 0, 0)),
                   pl.BlockSpec((1, 1, 1, LANES, N_CMP_PAD), lambda b, kv, g: (b, kv, g, 0, 0))],
        out_shape=[jax.ShapeDtypeStruct((BATCH, 2, N_KV, N_CMP_PAD, LANES), BF16),
                   jax.ShapeDtypeStruct((BATCH, 2, N_KV, LANES, N_CMP_PAD), BF16)],
        scratch_shapes=[pltpu.VMEM((SEQ + LANES, LANES), F32)],
        compiler_params=_params(3),
        name="nsa_compress",
    )(z, pe, w1, w2, kg)


def _key_norm_bound(k_ref):
    def body(c, mx):
        x = k_ref[pl.ds(pl.multiple_of(c * 512, 512), 512), :].astype(F32)
        return jnp.maximum(mx, jnp.max(jnp.sum(x * x, axis=1, keepdims=True), axis=0, keepdims=True))
    return jnp.broadcast_to(lax.fori_loop(0, SEQ // 512, body, jnp.zeros((1, 1), F32)), (1, TQ))


def _query_norms(q_ref):
    head_of_lane = lax.shift_right_logical(lax.broadcasted_iota(I32, (N_HEADS, GROUP_W), 1), 7)
    pick = jnp.where(head_of_lane == lax.broadcasted_iota(I32, (N_HEADS, GROUP_W), 0), 1.0, 0.0).astype(BF16)
    q = q_ref[...].astype(F32)
    return lax.dot_general(pick, (q * q).astype(BF16), NT_DIMS, preferred_element_type=F32)


def _flash_heads(q_ref, qn2, slot0, groups, jlo, jhi, m_s, l_s, acc_s):
    n_heads = sum(len(g[2]) for g in groups)
    lo, hi = slot0, slot0 + n_heads
    l_s[lo:hi, :] = jnp.zeros((n_heads, TQ), F32)
    for slot in range(lo, hi):
        acc_s[slot] = jnp.zeros((HEAD_DIM, TQ), F32)

    bounds = []
    for _, _, cs, _, kmax2 in groups:
        for c in cs:
            h = c.start // LANES
            bounds.append(jnp.sqrt(qn2[h:h + 1, :] * kmax2) * BOUND_SLACK)
    m_bound = jnp.concatenate(bounds, axis=0)

    def scores(j):
        r0 = pl.multiple_of(j * KV_TILE, KV_TILE)
        ss, vts, biases = [], [], {}
        for k_ref, vt_ref, cs, bias_fn, _ in groups:
            k = k_ref[pl.ds(r0, KV_TILE), :]
            vt = vt_ref[j]
            if bias_fn not in biases:
                biases[bias_fn] = bias_fn(j)
            b = biases[bias_fn]
            for c in cs:
                ss.append(lax.dot_general(k, q_ref[:, c], NT_DIMS, preferred_element_type=F32) + b)
                vts.append(vt)
        return ss, vts

    def max_pass(j, m):
        ss, _ = scores(j)
        return jnp.maximum(m, jnp.concatenate([_fold8(s, jnp.maximum) for s in ss], axis=0))

    def exact_max():
        m8 = lax.fori_loop(jlo, jhi, max_pass, jnp.full((8 * n_heads, TQ), NEG, F32))
        m_fin = jnp.concatenate([jnp.max(m8[8 * n:8 * n + 8], axis=0, keepdims=True)
                                 for n in range(n_heads)], axis=0)
        return jnp.maximum(m_fin, M_FLOOR)

    m_s[lo:hi, :] = lax.cond(jnp.max(m_bound) > BOUND_LIMIT, exact_max, lambda: m_bound)

    def tile(j):
        ss, vts = scores(j)
        m_ref = m_s[lo:hi, :]
        ps = [jnp.exp2(s - m_ref[n:n + 1, :]) for n, s in enumerate(ss)]
        l_add = jnp.concatenate([jnp.sum(p, axis=0, keepdims=True) for p in ps], axis=0)
        return l_add, [jnp.dot(vt, p.astype(BF16), preferred_element_type=F32) for vt, p in zip(vts, ps)]

    def one(j, carry):
        l_add, pv = tile(j)
        l_s[lo:hi, :] = l_s[lo:hi, :] + l_add
        for n in range(n_heads):
            acc_s[lo + n] = acc_s[lo + n] + pv[n]
        return carry

    def two(jj, carry):
        la, pa = tile(jlo + 2 * jj)
        lb, pb = tile(jlo + 2 * jj + 1)
        l_s[lo:hi, :] = l_s[lo:hi, :] + (la + lb)
        for n in range(n_heads):
            acc_s[lo + n] = acc_s[lo + n] + (pa[n] + pb[n])
        return carry

    pairs = lax.shift_right_logical(jhi - jlo, 1)
    lax.fori_loop(0, pairs, two, 0)
    lax.fori_loop(jlo + 2 * pairs, jhi, one, 0)


def _flash_out(slot, l_s, acc_s):
    return acc_s[slot] * (1.0 / l_s[slot:slot + 1, :])


def _nsa_body(qn, qr, kc, vct, ks, vst, kw, vwt, gate, bg, ovt, yb, imp_s, sel_s, m_s, l_s, acc_s, kn_s):
    i = pl.program_id(1)

    @pl.when(i == 0)
    def _():
        for g in range(N_KV):
            kn_s[g:g + 1, :] = _key_norm_bound(ks.at[0, g])
            kn_s[N_KV + g:N_KV + g + 1, :] = _key_norm_bound(kw.at[0, g])

    tq = i * TQ + lax.broadcasted_iota(I32, (1, TQ), 1)
    n_kv = i + 1
    sg = _sigmoid(gate[...].T[0:3 * N_HEADS, :])
    cm = (lax.broadcasted_iota(I32, (N_CMP_PAD, TQ), 0) * CMP_STRIDE + (CMP_LEN - 1)) <= tq
    kk = lax.broadcasted_iota(I32, (N_SLC, TQ), 0)
    sub8 = lax.broadcasted_iota(I32, (8, TQ), 0)
    blk_t = lax.shift_right_logical(tq, 6)
    valid = kk <= blk_t
    force = (kk == 0) | (kk == blk_t) | (kk == blk_t - 1)
    rowkv = lax.broadcasted_iota(I32, (KV_TILE, TQ), 0)
    cmp_slot, slc_slot, win_slot = 0, N_HEADS, 2 * N_HEADS
    head_cols = [[slice(h * LANES, (h + 1) * LANES) for h in range(HPG * g, HPG * (g + 1))]
                 for g in range(N_KV)]

    def win_bias(j):
        d = tq - (j * KV_TILE + rowkv)
        return jnp.where((d >= 0) & (d < WINDOW), 0.0, NEG)

    def slc_bias(g):
        def bias(j):
            nb = KV_TILE // SLC_BLK
            tile = jnp.concatenate([jnp.broadcast_to(sel_s[g, pl.ds(nb * j + u, 1), :], (SLC_BLK, TQ))
                                    for u in range(nb)], axis=0)
            return jnp.where(j * KV_TILE + rowkv <= tq, tile, NEG)
        return bias

    for g in range(N_KV):
        kcg = kc[0, 0, g]
        vcg = vct[0, 0, g]
        cols = head_cols[g]
        ss = [jnp.where(cm, lax.dot_general(kcg, qn[:, c], NT_DIMS, preferred_element_type=F32), NEG)
              for c in cols]
        es = [jnp.exp2(s - jnp.max(s, axis=0, keepdims=True)) for s in ss]
        ps = [jnp.where(cm, e * (1.0 / jnp.sum(e, axis=0, keepdims=True)), 0.0) for e in es]
        for hh, p in enumerate(ps):
            acc_s[cmp_slot + HPG * g + hh] = jnp.dot(vcg, p.astype(BF16), preferred_element_type=F32)
        psum = (ps[0] + ps[1]) + (ps[2] + ps[3])

        p_hi = psum.astype(BF16)
        p_lo = (psum - p_hi.astype(F32)).astype(BF16)
        imp = (jnp.dot(ovt[...], p_hi, preferred_element_type=F32)
               + jnp.dot(ovt[...], p_lo, preferred_element_type=F32))
        imp = jnp.where(force, FORCE, jnp.where(valid, imp, -FORCE))
        imp_s[...] = imp
        bands = [imp[8 * r:8 * r + 8] for r in range(N_SLC // 8)]
        ranks = [jnp.zeros((8, TQ), F32) for _ in bands]
        for j in range(N_SLC):
            vj = imp_s[j:j + 1, :]
            for r, x in enumerate(bands):
                if r < j // 8:
                    hit = vj > x
                elif r > j // 8:
                    hit = vj >= x
                else:
                    hit = (vj > x) | ((vj == x) & (sub8 > j % 8))
                ranks[r] = ranks[r] + jnp.where(hit, 1.0, 0.0)
        sel_s[g] = jnp.where(jnp.concatenate(ranks, axis=0) < SLC_TOPN, 0.0, NEG)

    qn2 = _query_norms(qr)
    _flash_heads(qr, qn2, slc_slot, [(ks.at[0, g], vst.at[0, g], head_cols[g], slc_bias(g), kn_s[g:g + 1, :])
                                for g in range(N_KV)], 0, n_kv, m_s, l_s, acc_s)
    _flash_heads(qr, qn2, win_slot, [(kw.at[0, g], vwt.at[0, g], head_cols[g], win_bias,
                                 kn_s[N_KV + g:N_KV + g + 1, :]) for g in range(N_KV)],
                 jnp.maximum(n_kv - 1 - WINDOW // KV_TILE, 0), n_kv, m_s, l_s, acc_s)
    for h in range(N_HEADS):
        c = slice(h * LANES, (h + 1) * LANES)
        tot = (sg[h:h + 1] * acc_s[cmp_slot + h]
               + sg[N_HEADS + h:N_HEADS + h + 1] * _flash_out(slc_slot + h, l_s, acc_s)
               + sg[2 * N_HEADS + h:2 * N_HEADS + h + 1] * _flash_out(win_slot + h, l_s, acc_s))
        yb[:, c] = (tot.T * _silu(bg[:, c])).astype(BF16)


def _nsa(z, qn, qr, cmp_, cmpt, ks, vst, kw, vwt, ovt):
    nq = SEQ // TQ
    qspec = pl.BlockSpec((TQ, GROUP_W), lambda b, i: (b * nq + i, 0))
    kspec = pl.BlockSpec((1, N_KV, SEQ, LANES), lambda b, i: (b, 0, 0, 0))
    vtspec = pl.BlockSpec((1, N_KV, SEQ // KV_TILE, LANES, KV_TILE), lambda b, i: (b, 0, 0, 0, 0))
    return pl.pallas_call(
        _nsa_body,
        grid=(BATCH, nq),
        in_specs=[qspec, qspec,
                  pl.BlockSpec((1, 1, N_KV, N_CMP_PAD, LANES), lambda b, i: (b, 0, 0, 0, 0)),
                  pl.BlockSpec((1, 1, N_KV, LANES, N_CMP_PAD), lambda b, i: (b, 1, 0, 0, 0)),
                  kspec, vtspec, kspec, vtspec,
                  pl.BlockSpec((TQ, LANES), lambda b, i: (b * nq + i, COL_B_GATE // LANES)),
                  pl.BlockSpec((TQ, GROUP_W), lambda b, i: (b * nq + i, COL_B_G // GROUP_W)),
                  pl.BlockSpec((N_SLC, N_CMP_PAD), lambda b, i: (0, 0))],
        out_specs=qspec,
        out_shape=jax.ShapeDtypeStruct((BATCH * SEQ, GROUP_W), BF16),
        scratch_shapes=[pltpu.VMEM((N_SLC, TQ), F32), pltpu.VMEM((N_KV, N_SLC, TQ), F32),
                        pltpu.VMEM((3 * N_HEADS, TQ), F32), pltpu.VMEM((3 * N_HEADS, TQ), F32),
                        pltpu.VMEM((3 * N_HEADS, HEAD_DIM, TQ), F32), pltpu.VMEM((8, TQ), F32)],
        compiler_params=_params(2),
        name="nsa_attn",
    )(qn, qr, cmp_, cmpt, ks, vst, kw, vwt, z, z, ovt)


def _dsa_body(qd, kd, vdt, iq, ik, iw, cg, yc, sc_s, bias_s, m_s, l_s, acc_s, kn_s):
    i = pl.program_id(1)

    @pl.when(i == 0)
    def _():
        for g in range(N_KV):
            kn_s[g:g + 1, :] = _key_norm_bound(kd.at[0, g])

    tq = i * TQ + lax.broadcasted_iota(I32, (1, TQ), 1)
    n_kv = i + 1
    iwt = iw[...].T[0:IDX_HEADS, :] * IW_SCALE
    rowkv = lax.broadcasted_iota(I32, (KV_TILE, TQ), 0)
    kf = float(DSA_TOPK)

    def score_tile(j):
        r0 = pl.multiple_of(j * KV_TILE, KV_TILE)
        ikc = ik[0, pl.ds(r0, KV_TILE), :]
        sc = jnp.zeros((KV_TILE, TQ), F32)
        for h in range(IDX_HEADS):
            r = lax.dot_general(ikc, iq[:, 2 * h * LANES:(2 * h + 2) * LANES], NT_DIMS,
                                preferred_element_type=F32)
            sc = sc + jnp.maximum(r, 0.0) * iwt[h:h + 1, :]
        return r0, sc

    def fill(j, carry):
        mn, mx = carry
        r0, sc = score_tile(j)
        sc_s[pl.ds(r0, KV_TILE), :] = sc
        return (jnp.minimum(mn, _fold8(sc, jnp.minimum)), jnp.maximum(mx, _fold8(sc, jnp.maximum)))

    def fill2(jj, carry):
        return fill(2 * jj + 1, fill(2 * jj, carry))

    half = lax.shift_right_logical(i, 1)
    mm = lax.fori_loop(0, half, fill2, (jnp.full((8, TQ), jnp.inf, F32), jnp.full((8, TQ), -jnp.inf, F32)))
    mn8, mx8 = lax.fori_loop(2 * half, i, fill, mm)
    r0, sc = score_tile(i)
    ok = r0 + rowkv <= tq
    sc_s[pl.ds(r0, KV_TILE), :] = jnp.where(ok, sc, -jnp.inf)
    lo0 = jnp.min(jnp.minimum(mn8, _fold8(jnp.where(ok, sc, jnp.inf), jnp.minimum)), axis=0, keepdims=True)
    hi0 = jnp.max(jnp.maximum(mx8, _fold8(jnp.where(ok, sc, -jnp.inf), jnp.maximum)), axis=0, keepdims=True)

    def sweep(fn, n_out, side=None):
        def body(c, accs):
            r0 = pl.multiple_of(c * KV_TILE, KV_TILE)
            x = sc_s[pl.ds(r0, KV_TILE), :]
            if side is not None:
                side(r0, x)
            inds = fn(x, r0 + rowkv)
            out = []
            for a, ind in zip(accs, inds):
                lanes = [a, jnp.zeros_like(a), jnp.zeros_like(a), jnp.zeros_like(a)]
                for r in range(KV_TILE // 8):
                    hit = ind[8 * r:8 * r + 8]
                    lanes[r % 4] = jnp.where(hit, lanes[r % 4] + 1.0, lanes[r % 4])
                out.append((lanes[0] + lanes[1]) + (lanes[2] + lanes[3]))
            return tuple(out)
        def body2(cc, accs):
            return body(2 * cc + 1, body(2 * cc, accs))

        pairs = lax.shift_right_logical(n_kv, 1)
        accs = lax.fori_loop(0, pairs, body2, tuple(jnp.zeros((8, TQ), F32) for _ in range(n_out)))
        accs = lax.fori_loop(2 * pairs, n_kv, body, accs)
        return tuple(jnp.sum(a, axis=0, keepdims=True) for a in accs)

    def data_span(lo, hi):
        def body(c, carry):
            mn, mx = carry
            x = sc_s[pl.ds(pl.multiple_of(c * KV_TILE, KV_TILE), KV_TILE), :]
            mn = jnp.minimum(mn, _fold8(jnp.where(x >= lo, x, jnp.inf), jnp.minimum))
            mx = jnp.maximum(mx, _fold8(jnp.where(x <= hi, x, -jnp.inf), jnp.maximum))
            return mn, mx
        mn, mx = lax.fori_loop(0, n_kv, body, (jnp.full((8, TQ), jnp.inf, F32), jnp.full((8, TQ), -jnp.inf, F32)))
        return jnp.min(mn, axis=0, keepdims=True), jnp.max(mx, axis=0, keepdims=True)

    nvalid = (tq + 1).astype(F32)
    act0 = jnp.where(nvalid > kf, 1.0, 0.0)

    def bisect(_, c):
        lo, hi, clo, act = c
        on = act > 0.0
        mid = lo + (hi - lo) * 0.5
        exh = (mid <= lo) | (mid >= hi)
        p = jnp.where(exh, hi, mid)
        (cnt,) = sweep(lambda x, srow: (x >= p,), 1)
        up = on & (cnt >= kf)
        dn = on & (cnt < kf)
        clo = jnp.where(up, cnt, clo)
        act = jnp.where(on & (clo != kf) & jnp.logical_not(exh), 1.0, 0.0)
        return jnp.where(up, p, lo), jnp.where(dn, p, hi), clo, act

    def snap(c):
        lo, hi, clo, act = c
        on = act > 0.0
        dmin, dmax = data_span(lo, hi)
        act = jnp.where(on & (dmin < dmax), 1.0, 0.0)
        return jnp.where(on, dmin, lo), jnp.where(on, dmax, hi), clo, act

    def cond(c):
        return c[0] > 0.0

    def round_(c):
        state = lax.fori_loop(0, SNAP_EVERY - 1, bisect, snap(c[1:]))
        return (jnp.sum(state[3]),) + state

    state = lax.fori_loop(0, SNAP_FIRST, bisect, (lo0, hi0, nvalid, act0))
    _, thr, _, _, _ = lax.while_loop(cond, round_, (jnp.sum(state[3]),) + state)

    def write_all_ge(r0, x):
        bias_s[pl.ds(r0, KV_TILE), :] = jnp.where(x >= thr, 0.0, NEG)

    (cnt_ge,) = sweep(lambda x, srow: (x >= thr,), 1, side=write_all_ge)
    excess = cnt_ge > kf

    @pl.when(jnp.sum(jnp.where(excess, 1.0, 0.0)) > 0.0)
    def _():
        (cnt_g,) = sweep(lambda x, srow: (x > thr,), 1)
        need = kf - cnt_g

        def it(_, c):
            jl, jh = c
            jm = lax.shift_right_arithmetic(jl + jh, 1)
            (cnt,) = sweep(lambda x, srow: ((x == thr) & (srow <= jm),), 1)
            ge = cnt >= need
            return jnp.where(ge, jl, jm), jnp.where(ge, jm, jh)
        _, jh = lax.fori_loop(0, 12, it, (jnp.full((1, TQ), -1, I32), jnp.full((1, TQ), SEQ - 1, I32)))
        last = jnp.where(excess, jh, SEQ)

        def rewrite(c, carry):
            r0 = pl.multiple_of(c * KV_TILE, KV_TILE)
            x = sc_s[pl.ds(r0, KV_TILE), :]
            sel = (x > thr) | ((x == thr) & (r0 + rowkv <= last))
            bias_s[pl.ds(r0, KV_TILE), :] = jnp.where(sel, 0.0, NEG)
            return carry
        lax.fori_loop(0, n_kv, rewrite, 0)

    def bias(j):
        return bias_s[pl.ds(pl.multiple_of(j * KV_TILE, KV_TILE), KV_TILE), :]

    groups = [(kd.at[0, g], vdt.at[0, g], [slice(h * LANES, (h + 1) * LANES)
                                           for h in range(HPG * g, HPG * (g + 1))], bias, kn_s[g:g + 1, :])
              for g in range(N_KV)]
    _flash_heads(qd, _query_norms(qd), 0, groups, 0, n_kv, m_s, l_s, acc_s)
    for h in range(N_HEADS):
        c = slice(h * LANES, (h + 1) * LANES)
        yc[:, c] = (_flash_out(h, l_s, acc_s).T * _silu(cg[:, c])).astype(BF16)


def _dsa(z, qd, kd, vdt, iq, ik):
    nq = SEQ // TQ
    qspec = pl.BlockSpec((TQ, GROUP_W), lambda b, i: (b * nq + i, 0))
    return pl.pallas_call(
        _dsa_body,
        grid=(BATCH, nq),
        in_specs=[qspec,
                  pl.BlockSpec((1, N_KV, SEQ, LANES), lambda b, i: (b, 0, 0, 0)),
                  pl.BlockSpec((1, N_KV, SEQ // KV_TILE, LANES, KV_TILE), lambda b, i: (b, 0, 0, 0, 0)),
                  pl.BlockSpec((TQ, 2 * LANES * IDX_HEADS), lambda b, i: (b * nq + i, 0)),
                  pl.BlockSpec((1, SEQ, 2 * LANES), lambda b, i: (b, 0, 0)),
                  pl.BlockSpec((TQ, LANES), lambda b, i: (b * nq + i, COL_C_IW // LANES)),
                  pl.BlockSpec((TQ, GROUP_W), lambda b, i: (b * nq + i, COL_C_G // GROUP_W))],
        out_specs=qspec,
        out_shape=jax.ShapeDtypeStruct((BATCH * SEQ, GROUP_W), BF16),
        scratch_shapes=[pltpu.VMEM((SEQ, TQ), F32), pltpu.VMEM((SEQ, TQ), F32), pltpu.VMEM((N_HEADS, TQ), F32),
                        pltpu.VMEM((N_HEADS, TQ), F32), pltpu.VMEM((N_HEADS, HEAD_DIM, TQ), F32),
                        pltpu.VMEM((8, TQ), F32)],
        compiler_params=_params(2),
        name="dsa_attn",
    )(qd, kd, vdt, iq, ik, z, z)


def _segment_moves():
    src = {}
    o = 0
    names = ("a_x", "a_g", "b_q", "b_g", "b_kv", "b_gate", "c_q", "c_g", "c_kv", "c_iq", "c_ik", "c_iw",
             "d_in", "d_b", "d_c", "d_g")
    for n, s in zip(names, SPLIT_SIZES):
        src[n] = (o, s)
        o += s
    dst = dict(a_x=COL_A_X, a_g=COL_A_G, b_q=COL_B_Q, b_g=COL_B_G, c_q=COL_C_Q, c_g=COL_C_G,
               d_in=COL_D_IN, d_b=COL_D_B, d_c=COL_D_C, d_g=COL_D_G, b_kv=COL_B_KV, c_kv=COL_C_KV,
               c_iq=COL_C_IQ, b_gate=COL_B_GATE, c_ik=COL_C_IK, c_iw=COL_C_IW)
    return [(src[n][0], dst[n], src[n][1]) for n in names]


def _wprep_body(w_ref, o_ref):
    o_ref[COL_B_GATE:PROJ_W, :] = jnp.zeros((PROJ_W - COL_B_GATE, TR_W), BF16)
    for s, d, width in _segment_moves():
        for c in range(0, width, GROUP_W):
            n = min(GROUP_W, width - c)
            o_ref[d + c:d + c + n, :] = w_ref[s + c:s + c + n, :].astype(BF16)


def _proj_weight(wt, layer):
    _, n, k = wt.shape
    return pl.pallas_call(
        _wprep_body,
        grid=(k // TR_W,),
        in_specs=[pl.BlockSpec((None, n, TR_W), lambda i: (layer, 0, i))],
        out_specs=pl.BlockSpec((PROJ_W, TR_W), lambda i: (0, i)),
        out_shape=jax.ShapeDtypeStruct((PROJ_W, k), BF16),
        compiler_params=_params(1),
        name="w_in_layout",
    )(wt)


def _rope_tables(head_dim):
    r = head_dim // 4
    half = r // 2
    pos = jnp.arange(SEQ, dtype=jnp.int32)
    inv = ROPE_THETA ** (-jnp.arange(half, dtype=F32) * 2.0 / r)
    ang = pos.astype(F32)[:, None] * inv[None, :]
    cos, sin = jnp.cos(ang), jnp.sin(ang)
    one = jnp.ones((SEQ, head_dim - r), F32)
    zero = jnp.zeros((SEQ, head_dim - r), F32)
    zh = jnp.zeros((SEQ, half), F32)
    reps = LANES // head_dim
    return tuple(jnp.tile(t, (1, reps)) for t in (
        jnp.concatenate([cos, cos, one], axis=1),
        jnp.concatenate([-sin, zh, zero], axis=1),
        jnp.concatenate([zh, sin, zero], axis=1)))


def _overlap_t():
    cs = jnp.arange(N_CMP_PAD) * CMP_STRIDE
    ss = jnp.arange(N_SLC) * SLC_BLK
    ov = jnp.clip(jnp.minimum(cs[None, :] + CMP_LEN, ss[:, None] + SLC_BLK)
                  - jnp.maximum(cs[None, :], ss[:, None]), 0, None).astype(F32) / CMP_LEN
    return jnp.where(jnp.arange(N_CMP_PAD)[None, :] < N_CMP_PAD - 1, ov, 0.0).astype(BF16)


def kernel(x, norm_g, w_in, w_out, lru_conv_w, lru_conv_b, lru_wa, lru_ba, lru_wx, lru_bx, lru_lambda,
           nsa_q_gain, nsa_k_gain, cmp_pe_k, cmp_w1_k, cmp_w2_k, cmp_pe_v, cmp_w1_v, cmp_w2_v,
           dsa_q_gain, dsa_k_gain, sc_conv_w):
    b, s, d = x.shape
    assert (b, s, d) == (BATCH, SEQ, D_MODEL)
    tabs128 = _rope_tables(HEAD_DIM)
    tabs64 = _rope_tables(IDX_DIM)
    ovt = _overlap_t()
    x2 = x.reshape(b * s, d)
    w_in_t = jnp.swapaxes(w_in, 1, 2)

    def row(v):
        return v.reshape(1, -1)

    for l in range(DEPTH):
        z = _inproj(x2, row(norm_g[l]), _proj_weight(w_in_t, l))
        ya, yd = _mix_ad(z, lru_conv_w[l], row(lru_conv_b[l]), lru_wa[l].astype(BF16), row(lru_ba[l]),
                         lru_wx[l].astype(BF16), row(lru_bx[l]), row(lru_lambda[l]), sc_conv_w[l])
        qn, qr, ks, vst, kw, vwt, qd, kd, vdt, iq, ik = _prep(
            z, tabs128, tabs64, row(nsa_q_gain[l]), row(nsa_k_gain[l]), row(dsa_q_gain[l]), row(dsa_k_gain[l]))
        cmp_, cmpt = _compress(z, jnp.stack([cmp_pe_k[l], cmp_pe_v[l]]),
                               jnp.stack([cmp_w1_k[l], cmp_w1_v[l]]).astype(BF16),
                               jnp.stack([cmp_w2_k[l], cmp_w2_v[l]]).astype(BF16), row(nsa_k_gain[l]))
        yb = _nsa(z, qn, qr, cmp_, cmpt, ks, vst, kw, vwt, ovt)
        yc = _dsa(z, qd, kd, vdt, iq, ik)
        x2 = _outproj((ya, yb, yc, yd), w_out, l, x2)
    return x2.reshape(b, s, d)
```

```python
import functools

import jax
import jax.numpy as jnp
from jax import lax
from jax.experimental import pallas as pl
from jax.experimental.pallas import tpu as pltpu

F32 = jnp.float32
BF16 = jnp.bfloat16
I32 = jnp.int32

D_MODEL = 4096
BATCH = 2
SEQ = 4096
DEPTH = 2
GROUP_W = D_MODEL // 4
HEAD_DIM = 128
N_HEADS = GROUP_W // HEAD_DIM
N_KV = N_HEADS // 4
HPG = N_HEADS // N_KV
ROPE_THETA = 500000.0
NORM_EPS = 1e-6
NEG = -1e30
FORCE = 1e6
LRU_BLOCKS = N_HEADS
LRU_C = 8.0
CONV_A = 4
CONV_D = 3
CMP_LEN = 32
CMP_STRIDE = 16
SLC_BLK = 64
SLC_TOPN = 16
WINDOW = 512
IDX_HEADS = 8
IDX_DIM = 64
DSA_TOPK = 256
SCALE = HEAD_DIM ** -0.5
Q_PRESCALE = SCALE * 1.4426950408889634
IW_SCALE = IDX_HEADS ** -0.5 * IDX_DIM ** -0.5

SPLIT_SIZES = (
    GROUP_W, GROUP_W,
    N_HEADS * HEAD_DIM, GROUP_W, 6 * N_KV * HEAD_DIM, 3 * N_HEADS,
    N_HEADS * HEAD_DIM, GROUP_W, 2 * N_KV * HEAD_DIM,
    IDX_HEADS * IDX_DIM, IDX_DIM, IDX_HEADS,
    GROUP_W, GROUP_W, GROUP_W, GROUP_W,
)

LANES = 128
COL_A_X, COL_A_G, COL_B_Q, COL_B_G, COL_C_Q, COL_C_G = 0, 1024, 2048, 3072, 4096, 5120
COL_D_IN, COL_D_B, COL_D_C, COL_D_G = 6144, 7168, 8192, 9216
COL_B_KV, COL_C_KV, COL_C_IQ = 10240, 11776, 12288
COL_B_GATE, COL_C_IK, COL_C_IW = 12800, 12928, 13056
PROJ_W = 13312

N_CMP_PAD = SEQ // CMP_STRIDE
N_SLC = SEQ // SLC_BLK
KV_TILE = 256
M_FLOOR = -1e20
BOUND_SLACK = 1.01
BOUND_LIMIT = 60.0
SNAP_FIRST, SNAP_EVERY = 16, 2

VMEM_LIMIT_BYTES = 56 * 1024 * 1024
INPROJ_VMEM_LIMIT_BYTES = 60 * 1024 * 1024

TM_IN, TN_IN = 1024, 1024
TM_OUT, TN_OUT = 1024, 512
T_TOK = 512
T_MIX = 256
TR_W = 128
TQ = 256

NT_DIMS = (((1,), (1,)), ((), ()))


def _params(n_axes, vmem_limit_bytes=VMEM_LIMIT_BYTES):
    return pltpu.CompilerParams(dimension_semantics=("arbitrary",) * n_axes,
                                vmem_limit_bytes=vmem_limit_bytes)


def _sigmoid(x):
    return 0.5 * jnp.tanh(0.5 * x) + 0.5


def _silu(x):
    return x * _sigmoid(x)


def _fold8(x, op=jnp.add):
    parts = [x[r:r + 8] for r in range(0, x.shape[0], 8)]
    while len(parts) > 1:
        nxt = [op(parts[a], parts[a + 1]) for a in range(0, len(parts) - 1, 2)]
        if len(parts) % 2:
            nxt.append(parts[-1])
        parts = nxt
    return parts[0]


def _inproj_body(x_hbm, g_ref, w_ref, o_ref, x_buf, h_ref, sem):
    i = pl.program_id(0)
    n_row_tiles = pl.num_programs(0)

    def x_copy(tile):
        return pltpu.make_async_copy(x_hbm.at[pl.ds(pl.multiple_of(tile * TM_IN, TM_IN), TM_IN), :], x_buf, sem)

    @pl.when(pl.program_id(1) == 0)
    def _():
        @pl.when(i == 0)
        def _():
            x_copy(0).start()

        x_copy(i).wait()

        def rows(r, carry):
            r0 = pl.multiple_of(r * 64, 64)
            x = x_buf[pl.ds(r0, 64), :]
            ms = jnp.mean(x * x, axis=-1, keepdims=True)
            h_ref[pl.ds(r0, 64), :] = ((x * lax.rsqrt(ms + NORM_EPS)) * g_ref[...]).astype(BF16)
            return carry
        lax.fori_loop(0, TM_IN // 64, rows, 0)

        @pl.when(i + 1 < n_row_tiles)
        def _():
            x_copy(i + 1).start()

    o_ref[...] = lax.dot_general(h_ref[...], w_ref[...], NT_DIMS, preferred_element_type=F32)


def _inproj(x2, g, wt):
    m = x2.shape[0]
    return pl.pallas_call(
        _inproj_body,
        grid=(m // TM_IN, PROJ_W // TN_IN),
        in_specs=[pl.BlockSpec(memory_space=pl.ANY),
                  pl.BlockSpec((1, D_MODEL), lambda i, j: (0, 0)),
                  pl.BlockSpec((TN_IN, D_MODEL), lambda i, j: (j, 0))],
        out_specs=pl.BlockSpec((TM_IN, TN_IN), lambda i, j: (i, j)),
        out_shape=jax.ShapeDtypeStruct((m, PROJ_W), F32),
        scratch_shapes=[pltpu.VMEM((TM_IN, D_MODEL), F32), pltpu.VMEM((TM_IN, D_MODEL), BF16),
                        pltpu.SemaphoreType.DMA(())],
        compiler_params=_params(2, INPROJ_VMEM_LIMIT_BYTES),
        name="inproj",
    )(x2, g, wt)


def _outproj_body(ya, yb, yc, yd, wa, wb, wc, wd, x_ref, o_ref):
    acc = jnp.dot(ya[...], wa[...].astype(BF16), preferred_element_type=F32)
    acc = acc + jnp.dot(yb[...], wb[...].astype(BF16), preferred_element_type=F32)
    acc = acc + jnp.dot(yc[...], wc[...].astype(BF16), preferred_element_type=F32)
    acc = acc + jnp.dot(yd[...], wd[...].astype(BF16), preferred_element_type=F32)
    o_ref[...] = x_ref[...] + acc


def _outproj(ys, w, layer, x2):
    m = x2.shape[0]
    yspec = pl.BlockSpec((TM_OUT, GROUP_W), lambda i, j: (i, 0))
    wspecs = [pl.BlockSpec((None, GROUP_W, TN_OUT), functools.partial(lambda i, j, k: (layer, k, j), k=k))
              for k in range(4)]
    xspec = pl.BlockSpec((TM_OUT, TN_OUT), lambda i, j: (i, j))
    return pl.pallas_call(
        _outproj_body,
        grid=(m // TM_OUT, D_MODEL // TN_OUT),
        in_specs=[yspec] * 4 + wspecs + [xspec],
        out_specs=xspec,
        out_shape=jax.ShapeDtypeStruct((m, D_MODEL), F32),
        compiler_params=_params(2),
        name="outproj",
    )(*ys, w, w, w, w, x2)


def _mix_ad_body(ax, ag, din, db, dc, dg, cw, cb, wa, ba, wx, bx, lam, scw,
                 ya, yd, xbuf, vbuf, a_s, b_s, hst):
    t = T_MIX

    @pl.when(pl.program_id(1) == 0)
    def _():
        xbuf[0:8, :] = jnp.zeros((8, GROUP_W), F32)
        vbuf[0:8, :] = jnp.zeros((8, GROUP_W), F32)
        hst[...] = jnp.zeros((8, GROUP_W), F32)

    xbuf[8:8 + t, :] = ax[...]
    for blk in range(LRU_BLOCKS):
        c = slice(blk * LANES, (blk + 1) * LANES)
        u = (cw[0:1, c] * xbuf[5:5 + t, c] + cw[1:2, c] * xbuf[6:6 + t, c]
             + cw[2:3, c] * xbuf[7:7 + t, c] + cw[3:4, c] * xbuf[8:8 + t, c]) + cb[:, c]
        ub = u.astype(BF16)
        r = _sigmoid(jnp.dot(ub, wa[blk], preferred_element_type=F32) + ba[:, c])
        ig = _sigmoid(jnp.dot(ub, wx[blk], preferred_element_type=F32) + bx[:, c])
        nl = -lam[:, c]
        sp = jnp.maximum(nl, 0.0) + jnp.log1p(jnp.exp(-jnp.abs(nl)))
        log_a = (-LRU_C * r) * sp
        a = jnp.exp(log_a)
        a_s[:, c] = a
        em1 = jnp.tanh(log_a) * (a * a + 1.0)
        b_s[:, c] = jnp.sqrt(-em1) * (ig * u)
    xbuf[0:8, :] = xbuf[t:t + 8, :]

    row = lax.broadcasted_iota(I32, (8, GROUP_W), 0)

    def group(gi, hprev):
        r0 = pl.multiple_of(gi * 8, 8)
        a = a_s[pl.ds(r0, 8), :]
        b = b_s[pl.ds(r0, 8), :]
        for d in (1, 2, 4):
            ok = row >= d
            b = jnp.where(ok, a * pltpu.roll(b, d, axis=0) + b, b)
            a = jnp.where(ok, a * pltpu.roll(a, d, axis=0), a)
        h = a * hprev + b
        b_s[pl.ds(r0, 8), :] = h
        return jnp.broadcast_to(h[7:8, :], (8, GROUP_W))

    hst[...] = lax.fori_loop(0, t // 8, group, hst[...])

    vbuf[8:8 + t, :] = dc[...] * din[...]
    for blk in range(LRU_BLOCKS):
        c = slice(blk * LANES, (blk + 1) * LANES)
        ya[:, c] = (b_s[:, c] * _silu(ag[:, c])).astype(BF16)
        conv = (scw[0:1, c] * vbuf[6:6 + t, c] + scw[1:2, c] * vbuf[7:7 + t, c]
                + scw[2:3, c] * vbuf[8:8 + t, c])
        yd[:, c] = ((db[:, c] * conv) * _silu(dg[:, c])).astype(BF16)
    vbuf[0:8, :] = vbuf[t:t + 8, :]


def _mix_ad(z, cw, cb, wa, ba, wx, bx, lam, scw):
    nt = SEQ // T_MIX

    def zspec(col):
        return pl.BlockSpec((T_MIX, GROUP_W), lambda b, i: (b * nt + i, col // GROUP_W))

    def full(a):
        return pl.BlockSpec(a.shape, lambda b, i: (0,) * a.ndim)

    small = (cw, cb, wa, ba, wx, bx, lam, scw)
    yspec = pl.BlockSpec((T_MIX, GROUP_W), lambda b, i: (b * nt + i, 0))
    return pl.pallas_call(
        _mix_ad_body,
        grid=(BATCH, nt),
        in_specs=[zspec(COL_A_X), zspec(COL_A_G), zspec(COL_D_IN), zspec(COL_D_B), zspec(COL_D_C),
                  zspec(COL_D_G)] + [full(a) for a in small],
        out_specs=[yspec, yspec],
        out_shape=[jax.ShapeDtypeStruct((BATCH * SEQ, GROUP_W), BF16)] * 2,
        scratch_shapes=[pltpu.VMEM((T_MIX + 8, GROUP_W), F32), pltpu.VMEM((T_MIX + 8, GROUP_W), F32),
                        pltpu.VMEM((T_MIX, GROUP_W), F32), pltpu.VMEM((T_MIX, GROUP_W), F32),
                        pltpu.VMEM((8, GROUP_W), F32)],
        compiler_params=_params(2),
        name="mix_ad",
    )(z, z, z, z, z, z, *small)


def _rms_head(x, gain):
    ms = jnp.mean(x * x, axis=-1, keepdims=True)
    return (x * lax.rsqrt(ms + NORM_EPS)) * gain


def _rope(y, cos, nsin, psin, half):
    return y * cos + pltpu.roll(y, LANES - half, axis=1) * nsin + pltpu.roll(y, half, axis=1) * psin


def _store_vt(dst, g, v):
    for cc in range(T_TOK // KV_TILE):
        dst[0, g, cc] = v[cc * KV_TILE:(cc + 1) * KV_TILE, :].T.astype(BF16)


def _rms_head_mxu(x, gain):
    ssq = jnp.dot((x * x).astype(BF16), jnp.ones((LANES, LANES), BF16), preferred_element_type=F32)
    return (x * lax.rsqrt(ssq * (1.0 / LANES) + NORM_EPS)) * gain


def _prep_body(bq, bkv_s, bkv_w, cq, ckv, ciq, cik, c128, n128, p128, c64, n64, p64,
               qgn, kgn, qgd, kgd,
               qn_o, qr_o, ks_o, vst_o, kw_o, vwt_o, qd_o, kd_o, vdt_o, iq_o, ik_o):
    r128 = (c128[...], n128[...], p128[...], HEAD_DIM // 8)
    r64 = (c64[...], n64[...], p64[...], IDX_DIM // 8)
    for h in range(N_HEADS):
        c = slice(h * LANES, (h + 1) * LANES)
        y = _rms_head_mxu(bq[:, c], qgn[...])
        qn_o[:, c] = (y * Q_PRESCALE).astype(BF16)
        qr_o[:, c] = (_rope(y, *r128) * Q_PRESCALE).astype(BF16)
        qd_o[:, c] = (_rope(_rms_head_mxu(cq[:, c], qgd[...]), *r128) * Q_PRESCALE).astype(BF16)
    for g in range(N_KV):
        ck = slice(g * LANES, (g + 1) * LANES)
        cv = slice((N_KV + g) * LANES, (N_KV + g + 1) * LANES)
        ks_o[0, g] = _rope(_rms_head_mxu(bkv_s[:, ck], kgn[...]), *r128).astype(BF16)
        _store_vt(vst_o, g, bkv_s[:, cv])
        kw_o[0, g] = _rope(_rms_head_mxu(bkv_w[:, ck], kgn[...]), *r128).astype(BF16)
        _store_vt(vwt_o, g, bkv_w[:, cv])
        kd_o[0, g] = _rope(_rms_head_mxu(ckv[:, ck], kgd[...]), *r128).astype(BF16)
        _store_vt(vdt_o, g, ckv[:, cv])

    low = lax.broadcasted_iota(I32, (T_TOK, LANES), 1) < IDX_DIM
    ik = _rope(cik[...], *r64)
    hi = ik.astype(BF16).astype(F32)
    ik_o[0, :, 0:LANES] = (hi + pltpu.roll(ik - hi, IDX_DIM, axis=1)).astype(BF16)
    ik_o[0, :, LANES:2 * LANES] = hi.astype(BF16)
    for v in range(IDX_HEADS // 2):
        x = _rope(ciq[:, v * LANES:(v + 1) * LANES], *r64)
        for half in range(2):
            h = 2 * v + half
            xs = x if half == 0 else pltpu.roll(x, IDX_DIM, axis=1)
            xh = jnp.where(low, xs, 0.0)
            hi = xh.astype(BF16).astype(F32)
            iq_o[:, 2 * h * LANES:(2 * h + 1) * LANES] = (hi + pltpu.roll(hi, IDX_DIM, axis=1)).astype(BF16)
            iq_o[:, (2 * h + 1) * LANES:(2 * h + 2) * LANES] = (xh - hi).astype(BF16)


def _prep(z, tabs128, tabs64, qgn, kgn, qgd, kgd):
    nt = SEQ // T_TOK

    def zspec(col, width):
        return pl.BlockSpec((T_TOK, width), lambda b, i: (b * nt + i, col // width))

    tab = pl.BlockSpec((T_TOK, LANES), lambda b, i: (i, 0))
    gain = pl.BlockSpec((1, LANES), lambda b, i: (0, 0))
    qspec = pl.BlockSpec((T_TOK, GROUP_W), lambda b, i: (b * nt + i, 0))
    kspec = pl.BlockSpec((1, N_KV, T_TOK, LANES), lambda b, i: (b, 0, i, 0))
    vtspec = pl.BlockSpec((1, N_KV, T_TOK // KV_TILE, LANES, KV_TILE), lambda b, i: (b, 0, i, 0, 0))
    qshape = jax.ShapeDtypeStruct((BATCH * SEQ, GROUP_W), BF16)
    kshape = jax.ShapeDtypeStruct((BATCH, N_KV, SEQ, LANES), BF16)
    vtshape = jax.ShapeDtypeStruct((BATCH, N_KV, SEQ // KV_TILE, LANES, KV_TILE), BF16)
    return pl.pallas_call(
        _prep_body,
        grid=(BATCH, nt),
        in_specs=[zspec(COL_B_Q, GROUP_W), zspec(COL_B_KV + 512, 512), zspec(COL_B_KV + 1024, 512),
                  zspec(COL_C_Q, GROUP_W), zspec(COL_C_KV, 512), zspec(COL_C_IQ, 512), zspec(COL_C_IK, LANES)]
                 + [tab] * 6 + [gain] * 4,
        out_specs=[qspec, qspec, kspec, vtspec, kspec, vtspec, qspec, kspec, vtspec,
                   pl.BlockSpec((T_TOK, 2 * LANES * IDX_HEADS), lambda b, i: (b * nt + i, 0)),
                   pl.BlockSpec((1, T_TOK, 2 * LANES), lambda b, i: (b, i, 0))],
        out_shape=[qshape, qshape, kshape, vtshape, kshape, vtshape, qshape, kshape, vtshape,
                   jax.ShapeDtypeStruct((BATCH * SEQ, 2 * LANES * IDX_HEADS), BF16),
                   jax.ShapeDtypeStruct((BATCH, SEQ, 2 * LANES), BF16)],
        compiler_params=_params(2),
        name="attn_prep",
    )(z, z, z, z, z, z, z, *tabs128, *tabs64, qgn, kgn, qgd, kgd)


def _cmp_body(zc, pe, w1, w2, kg, o, ot, kbuf):
    kbuf[0:SEQ, :] = zc[...]
    kbuf[SEQ:SEQ + LANES, :] = jnp.zeros((LANES, LANES), F32)
    acc = jnp.zeros((N_CMP_PAD, LANES), F32)
    for l in range(CMP_LEN):
        rows = kbuf[pl.ds(l, N_CMP_PAD, stride=CMP_STRIDE), :] + pe[0, l:l + 1, :]
        acc = acc + jnp.dot(rows.astype(BF16), w1[0, l], preferred_element_type=F32)
    out = jnp.dot(_silu(acc).astype(BF16), w2[0], preferred_element_type=F32)
    res = jnp.where(pl.program_id(1) == 0, _rms_head(out, kg[...]), out)
    o[0, 0, 0] = res.astype(BF16)
    ot[0, 0, 0] = res.T.astype(BF16)


def _compress(z, pe, w1, w2, kg):
    base = COL_B_KV // LANES
    return pl.pallas_call(
        _cmp_body,
        grid=(BATCH, 2, N_KV),
        in_specs=[pl.BlockSpec((SEQ, LANES), lambda b, kv, g: (b, base + kv * N_KV + g)),
                  pl.BlockSpec((1, CMP_LEN, LANES), lambda b, kv, g: (kv, 0, 0)),
                  pl.BlockSpec((1, CMP_LEN, LANES, LANES), lambda b, kv, g: (kv, 0, 0, 0)),
                  pl.BlockSpec((1, LANES, LANES), lambda b, kv, g: (kv, 0, 0)),
                  pl.BlockSpec((1, LANES), lambda b, kv, g: (0, 0))],
        out_specs=[pl.BlockSpec((1, 1, 1, N_CMP_PAD, LANES), lambda b, kv, g: (b, kv, g, 0, 0)),
                   pl.BlockSpec((1, 1, 1, LANES, N_CMP_PAD), lambda b, kv, g: (b, kv, g, 0, 0))],
        out_shape=[jax.ShapeDtypeStruct((BATCH, 2, N_KV, N_CMP_PAD, LANES), BF16),
                   jax.ShapeDtypeStruct((BATCH, 2, N_KV, LANES, N_CMP_PAD), BF16)],
        scratch_shapes=[pltpu.VMEM((SEQ + LANES, LANES), F32)],
        compiler_params=_params(3),
        name="nsa_compress",
    )(z, pe, w1, w2, kg)


def _key_norm_bound(k_ref):
    def body(c, mx):
        x = k_ref[pl.ds(pl.multiple_of(c * 512, 512), 512), :].astype(F32)
        return jnp.maximum(mx, jnp.max(jnp.sum(x * x, axis=1, keepdims=True), axis=0, keepdims=True))
    return jnp.broadcast_to(lax.fori_loop(0, SEQ // 512, body, jnp.zeros((1, 1), F32)), (1, TQ))


def _query_norms(q_ref):
    head_of_lane = lax.shift_right_logical(lax.broadcasted_iota(I32, (N_HEADS, GROUP_W), 1), 7)
    pick = jnp.where(head_of_lane == lax.broadcasted_iota(I32, (N_HEADS, GROUP_W), 0), 1.0, 0.0).astype(BF16)
    q = q_ref[...].astype(F32)
    return lax.dot_general(pick, (q * q).astype(BF16), NT_DIMS, preferred_element_type=F32)


def _flash_heads(q_ref, qn2, slot0, groups, jlo, jhi, m_s, l_s, acc_s):
    n_heads = sum(len(g[2]) for g in groups)
    lo, hi = slot0, slot0 + n_heads
    l_s[lo:hi, :] = jnp.zeros((n_heads, TQ), F32)
    for slot in range(lo, hi):
        acc_s[slot] = jnp.zeros((HEAD_DIM, TQ), F32)

    bounds = []
    for _, _, cs, _, kmax2 in groups:
        for c in cs:
            h = c.start // LANES
            bounds.append(jnp.sqrt(qn2[h:h + 1, :] * kmax2) * BOUND_SLACK)
    m_bound = jnp.concatenate(bounds, axis=0)

    def scores(j):
        r0 = pl.multiple_of(j * KV_TILE, KV_TILE)
        ss, vts, biases = [], [], {}
        for k_ref, vt_ref, cs, bias_fn, _ in groups:
            k = k_ref[pl.ds(r0, KV_TILE), :]
            vt = vt_ref[j]
            if bias_fn not in biases:
                biases[bias_fn] = bias_fn(j)
            b = biases[bias_fn]
            for c in cs:
                ss.append(lax.dot_general(k, q_ref[:, c], NT_DIMS, preferred_element_type=F32) + b)
                vts.append(vt)
        return ss, vts

    def max_pass(j, m):
        ss, _ = scores(j)
        return jnp.maximum(m, jnp.concatenate([_fold8(s, jnp.maximum) for s in ss], axis=0))

    def exact_max():
        m8 = lax.fori_loop(jlo, jhi, max_pass, jnp.full((8 * n_heads, TQ), NEG, F32))
        m_fin = jnp.concatenate([jnp.max(m8[8 * n:8 * n + 8], axis=0, keepdims=True)
                                 for n in range(n_heads)], axis=0)
        return jnp.maximum(m_fin, M_FLOOR)

    m_s[lo:hi, :] = lax.cond(jnp.max(m_bound) > BOUND_LIMIT, exact_max, lambda: m_bound)

    def tile(j):
        ss, vts = scores(j)
        m_ref = m_s[lo:hi, :]
        ps = [jnp.exp2(s - m_ref[n:n + 1, :]) for n, s in enumerate(ss)]
        l_add = jnp.concatenate([jnp.sum(p, axis=0, keepdims=True) for p in ps], axis=0)
        return l_add, [jnp.dot(vt, p.astype(BF16), preferred_element_type=F32) for vt, p in zip(vts, ps)]

    def one(j, carry):
        l_add, pv = tile(j)
        l_s[lo:hi, :] = l_s[lo:hi, :] + l_add
        for n in range(n_heads):
            acc_s[lo + n] = acc_s[lo + n] + pv[n]
        return carry

    def two(jj, carry):
        la, pa = tile(jlo + 2 * jj)
        lb, pb = tile(jlo + 2 * jj + 1)
        l_s[lo:hi, :] = l_s[lo:hi, :] + (la + lb)
        for n in range(n_heads):
            acc_s[lo + n] = acc_s[lo + n] + (pa[n] + pb[n])
        return carry

    pairs = lax.shift_right_logical(jhi - jlo, 1)
    lax.fori_loop(0, pairs, two, 0)
    lax.fori_loop(jlo + 2 * pairs, jhi, one, 0)


def _flash_out(slot, l_s, acc_s):
    return acc_s[slot] * (1.0 / l_s[slot:slot + 1, :])


def _nsa_body(qn, qr, kc, vct, ks, vst, kw, vwt, gate, bg, ovt, yb, imp_s, sel_s, m_s, l_s, acc_s, kn_s):
    i = pl.program_id(1)

    @pl.when(i == 0)
    def _():
        for g in range(N_KV):
            kn_s[g:g + 1, :] = _key_norm_bound(ks.at[0, g])
            kn_s[N_KV + g:N_KV + g + 1, :] = _key_norm_bound(kw.at[0, g])

    tq = i * TQ + lax.broadcasted_iota(I32, (1, TQ), 1)
    n_kv = i + 1
    sg = _sigmoid(gate[...].T[0:3 * N_HEADS, :])
    cm = (lax.broadcasted_iota(I32, (N_CMP_PAD, TQ), 0) * CMP_STRIDE + (CMP_LEN - 1)) <= tq
    kk = lax.broadcasted_iota(I32, (N_SLC, TQ), 0)
    sub8 = lax.broadcasted_iota(I32, (8, TQ), 0)
    blk_t = lax.shift_right_logical(tq, 6)
    valid = kk <= blk_t
    force = (kk == 0) | (kk == blk_t) | (kk == blk_t - 1)
    rowkv = lax.broadcasted_iota(I32, (KV_TILE, TQ), 0)
    cmp_slot, slc_slot, win_slot = 0, N_HEADS, 2 * N_HEADS
    head_cols = [[slice(h * LANES, (h + 1) * LANES) for h in range(HPG * g, HPG * (g + 1))]
                 for g in range(N_KV)]

    def win_bias(j):
        d = tq - (j * KV_TILE + rowkv)
        return jnp.where((d >= 0) & (d < WINDOW), 0.0, NEG)

    def slc_bias(g):
        def bias(j):
            nb = KV_TILE // SLC_BLK
            tile = jnp.concatenate([jnp.broadcast_to(sel_s[g, pl.ds(nb * j + u, 1), :], (SLC_BLK, TQ))
                                    for u in range(nb)], axis=0)
            return jnp.where(j * KV_TILE + rowkv <= tq, tile, NEG)
        return bias

    for g in range(N_KV):
        kcg = kc[0, 0, g]
        vcg = vct[0, 0, g]
        cols = head_cols[g]
        ss = [jnp.where(cm, lax.dot_general(kcg, qn[:, c], NT_DIMS, preferred_element_type=F32), NEG)
              for c in cols]
        es = [jnp.exp2(s - jnp.max(s, axis=0, keepdims=True)) for s in ss]
        ps = [jnp.where(cm, e * (1.0 / jnp.sum(e, axis=0, keepdims=True)), 0.0) for e in es]
        for hh, p in enumerate(ps):
            acc_s[cmp_slot + HPG * g + hh] = jnp.dot(vcg, p.astype(BF16), preferred_element_type=F32)
        psum = (ps[0] + ps[1]) + (ps[2] + ps[3])

        p_hi = psum.astype(BF16)
        p_lo = (psum - p_hi.astype(F32)).astype(BF16)
        imp = (jnp.dot(ovt[...], p_hi, preferred_element_type=F32)
               + jnp.dot(ovt[...], p_lo, preferred_element_type=F32))
        imp = jnp.where(force, FORCE, jnp.where(valid, imp, -FORCE))
        imp_s[...] = imp
        bands = [imp[8 * r:8 * r + 8] for r in range(N_SLC // 8)]
        ranks = [jnp.zeros((8, TQ), F32) for _ in bands]
        for j in range(N_SLC):
            vj = imp_s[j:j + 1, :]
            for r, x in enumerate(bands):
                if r < j // 8:
                    hit = vj > x
                elif r > j // 8:
                    hit = vj >= x
                else:
                    hit = (vj > x) | ((vj == x) & (sub8 > j % 8))
                ranks[r] = ranks[r] + jnp.where(hit, 1.0, 0.0)
        sel_s[g] = jnp.where(jnp.concatenate(ranks, axis=0) < SLC_TOPN, 0.0, NEG)

    qn2 = _query_norms(qr)
    _flash_heads(qr, qn2, slc_slot, [(ks.at[0, g], vst.at[0, g], head_cols[g], slc_bias(g), kn_s[g:g + 1, :])
                                for g in range(N_KV)], 0, n_kv, m_s, l_s, acc_s)
    _flash_heads(qr, qn2, win_slot, [(kw.at[0, g], vwt.at[0, g], head_cols[g], win_bias,
                                 kn_s[N_KV + g:N_KV + g + 1, :]) for g in range(N_KV)],
                 jnp.maximum(n_kv - 1 - WINDOW // KV_TILE, 0), n_kv, m_s, l_s, acc_s)
    for h in range(N_HEADS):
        c = slice(h * LANES, (h + 1) * LANES)
        tot = (sg[h:h + 1] * acc_s[cmp_slot + h]
               + sg[N_HEADS + h:N_HEADS + h + 1] * _flash_out(slc_slot + h, l_s, acc_s)
               + sg[2 * N_HEADS + h:2 * N_HEADS + h + 1] * _flash_out(win_slot + h, l_s, acc_s))
        yb[:, c] = (tot.T * _silu(bg[:, c])).astype(BF16)


def _nsa(z, qn, qr, cmp_, cmpt, ks, vst, kw, vwt, ovt):
    nq = SEQ // TQ
    qspec = pl.BlockSpec((TQ, GROUP_W), lambda b, i: (b * nq + i, 0))
    kspec = pl.BlockSpec((1, N_KV, SEQ, LANES), lambda b, i: (b, 0, 0, 0))
    vtspec = pl.BlockSpec((1, N_KV, SEQ // KV_TILE, LANES, KV_TILE), lambda b, i: (b, 0, 0, 0, 0))
    return pl.pallas_call(
        _nsa_body,
        grid=(BATCH, nq),
        in_specs=[qspec, qspec,
                  pl.BlockSpec((1, 1, N_KV, N_CMP_PAD, LANES), lambda b, i: (b, 0, 0, 0, 0)),
                  pl.BlockSpec((1, 1, N_KV, LANES, N_CMP_PAD), lambda b, i: (b, 1, 0, 0, 0)),
                  kspec, vtspec, kspec, vtspec,
                  pl.BlockSpec((TQ, LANES), lambda b, i: (b * nq + i, COL_B_GATE // LANES)),
                  pl.BlockSpec((TQ, GROUP_W), lambda b, i: (b * nq + i, COL_B_G // GROUP_W)),
                  pl.BlockSpec((N_SLC, N_CMP_PAD), lambda b, i: (0, 0))],
        out_specs=qspec,
        out_shape=jax.ShapeDtypeStruct((BATCH * SEQ, GROUP_W), BF16),
        scratch_shapes=[pltpu.VMEM((N_SLC, TQ), F32), pltpu.VMEM((N_KV, N_SLC, TQ), F32),
                        pltpu.VMEM((3 * N_HEADS, TQ), F32), pltpu.VMEM((3 * N_HEADS, TQ), F32),
                        pltpu.VMEM((3 * N_HEADS, HEAD_DIM, TQ), F32), pltpu.VMEM((8, TQ), F32)],
        compiler_params=_params(2),
        name="nsa_attn",
    )(qn, qr, cmp_, cmpt, ks, vst, kw, vwt, z, z, ovt)


def _dsa_body(qd, kd, vdt, iq, ik, iw, cg, yc, sc_s, bias_s, m_s, l_s, acc_s, kn_s):
    i = pl.program_id(1)

    @pl.when(i == 0)
    def _():
        for g in range(N_KV):
            kn_s[g:g + 1, :] = _key_norm_bound(kd.at[0, g])

    tq = i * TQ + lax.broadcasted_iota(I32, (1, TQ), 1)
    n_kv = i + 1
    iwt = iw[...].T[0:IDX_HEADS, :] * IW_SCALE
    rowkv = lax.broadcasted_iota(I32, (KV_TILE, TQ), 0)
    kf = float(DSA_TOPK)

    def score_tile(j):
        r0 = pl.multiple_of(j * KV_TILE, KV_TILE)
        ikc = ik[0, pl.ds(r0, KV_TILE), :]
        sc = jnp.zeros((KV_TILE, TQ), F32)
        for h in range(IDX_HEADS):
            r = lax.dot_general(ikc, iq[:, 2 * h * LANES:(2 * h + 2) * LANES], NT_DIMS,
                                preferred_element_type=F32)
            sc = sc + jnp.maximum(r, 0.0) * iwt[h:h + 1, :]
        return r0, sc

    def fill(j, carry):
        mn, mx = carry
        r0, sc = score_tile(j)
        sc_s[pl.ds(r0, KV_TILE), :] = sc
        return (jnp.minimum(mn, _fold8(sc, jnp.minimum)), jnp.maximum(mx, _fold8(sc, jnp.maximum)))

    def fill2(jj, carry):
        return fill(2 * jj + 1, fill(2 * jj, carry))

    half = lax.shift_right_logical(i, 1)
    mm = lax.fori_loop(0, half, fill2, (jnp.full((8, TQ), jnp.inf, F32), jnp.full((8, TQ), -jnp.inf, F32)))
    mn8, mx8 = lax.fori_loop(2 * half, i, fill, mm)
    r0, sc = score_tile(i)
    ok = r0 + rowkv <= tq
    sc_s[pl.ds(r0, KV_TILE), :] = jnp.where(ok, sc, -jnp.inf)
    lo0 = jnp.min(jnp.minimum(mn8, _fold8(jnp.where(ok, sc, jnp.inf), jnp.minimum)), axis=0, keepdims=True)
    hi0 = jnp.max(jnp.maximum(mx8, _fold8(jnp.where(ok, sc, -jnp.inf), jnp.maximum)), axis=0, keepdims=True)

    def sweep(fn, n_out, side=None):
        def body(c, accs):
            r0 = pl.multiple_of(c * KV_TILE, KV_TILE)
            x = sc_s[pl.ds(r0, KV_TILE), :]
            if side is not None:
                side(r0, x)
            inds = fn(x, r0 + rowkv)
            out = []
            for a, ind in zip(accs, inds):
                lanes = [a, jnp.zeros_like(a), jnp.zeros_like(a), jnp.zeros_like(a)]
                for r in range(KV_TILE // 8):
                    hit = ind[8 * r:8 * r + 8]
                    lanes[r % 4] = jnp.where(hit, lanes[r % 4] + 1.0, lanes[r % 4])
                out.append((lanes[0] + lanes[1]) + (lanes[2] + lanes[3]))
            return tuple(out)
        def body2(cc, accs):
            return body(2 * cc + 1, body(2 * cc, accs))

        pairs = lax.shift_right_logical(n_kv, 1)
        accs = lax.fori_loop(0, pairs, body2, tuple(jnp.zeros((8, TQ), F32) for _ in range(n_out)))
        accs = lax.fori_loop(2 * pairs, n_kv, body, accs)
        return tuple(jnp.sum(a, axis=0, keepdims=True) for a in accs)

    def data_span(lo, hi):
        def body(c, carry):
            mn, mx = carry
            x = sc_s[pl.ds(pl.multiple_of(c * KV_TILE, KV_TILE), KV_TILE), :]
            mn = jnp.minimum(mn, _fold8(jnp.where(x >= lo, x, jnp.inf), jnp.minimum))
            mx = jnp.maximum(mx, _fold8(jnp.where(x <= hi, x, -jnp.inf), jnp.maximum))
            return mn, mx
        mn, mx = lax.fori_loop(0, n_kv, body, (jnp.full((8, TQ), jnp.inf, F32), jnp.full((8, TQ), -jnp.inf, F32)))
        return jnp.min(mn, axis=0, keepdims=True), jnp.max(mx, axis=0, keepdims=True)

    nvalid = (tq + 1).astype(F32)
    act0 = jnp.where(nvalid > kf, 1.0, 0.0)

    def bisect(_, c):
        lo, hi, clo, act = c
        on = act > 0.0
        mid = lo + (hi - lo) * 0.5
        exh = (mid <= lo) | (mid >= hi)
        p = jnp.where(exh, hi, mid)
        (cnt,) = sweep(lambda x, srow: (x >= p,), 1)
        up = on & (cnt >= kf)
        dn = on & (cnt < kf)
        clo = jnp.where(up, cnt, clo)
        act = jnp.where(on & (clo != kf) & jnp.logical_not(exh), 1.0, 0.0)
        return jnp.where(up, p, lo), jnp.where(dn, p, hi), clo, act

    def snap(c):
        lo, hi, clo, act = c
        on = act > 0.0
        dmin, dmax = data_span(lo, hi)
        act = jnp.where(on & (dmin < dmax), 1.0, 0.0)
        return jnp.where(on, dmin, lo), jnp.where(on, dmax, hi), clo, act

    def cond(c):
        return c[0] > 0.0

    def round_(c):
        state = lax.fori_loop(0, SNAP_EVERY - 1, bisect, snap(c[1:]))
        return (jnp.sum(state[3]),) + state

    state = lax.fori_loop(0, SNAP_FIRST, bisect, (lo0, hi0, nvalid, act0))
    _, thr, _, _, _ = lax.while_loop(cond, round_, (jnp.sum(state[3]),) + state)

    def write_all_ge(r0, x):
        bias_s[pl.ds(r0, KV_TILE), :] = jnp.where(x >= thr, 0.0, NEG)

    (cnt_ge,) = sweep(lambda x, srow: (x >= thr,), 1, side=write_all_ge)
    excess = cnt_ge > kf

    @pl.when(jnp.sum(jnp.where(excess, 1.0, 0.0)) > 0.0)
    def _():
        (cnt_g,) = sweep(lambda x, srow: (x > thr,), 1)
        need = kf - cnt_g

        def it(_, c):
            jl, jh = c
            jm = lax.shift_right_arithmetic(jl + jh, 1)
            (cnt,) = sweep(lambda x, srow: ((x == thr) & (srow <= jm),), 1)
            ge = cnt >= need
            return jnp.where(ge, jl, jm), jnp.where(ge, jm, jh)
        _, jh = lax.fori_loop(0, 12, it, (jnp.full((1, TQ), -1, I32), jnp.full((1, TQ), SEQ - 1, I32)))
        last = jnp.where(excess, jh, SEQ)

        def rewrite(c, carry):
            r0 = pl.multiple_of(c * KV_TILE, KV_TILE)
            x = sc_s[pl.ds(r0, KV_TILE), :]
            sel = (x > thr) | ((x == thr) & (r0 + rowkv <= last))
            bias_s[pl.ds(r0, KV_TILE), :] = jnp.where(sel, 0.0, NEG)
            return carry
        lax.fori_loop(0, n_kv, rewrite, 0)

    def bias(j):
        return bias_s[pl.ds(pl.multiple_of(j * KV_TILE, KV_TILE), KV_TILE), :]

    groups = [(kd.at[0, g], vdt.at[0, g], [slice(h * LANES, (h + 1) * LANES)
                                           for h in range(HPG * g, HPG * (g + 1))], bias, kn_s[g:g + 1, :])
              for g in range(N_KV)]
    _flash_heads(qd, _query_norms(qd), 0, groups, 0, n_kv, m_s, l_s, acc_s)
    for h in range(N_HEADS):
        c = slice(h * LANES, (h + 1) * LANES)
        yc[:, c] = (_flash_out(h, l_s, acc_s).T * _silu(cg[:, c])).astype(BF16)


def _dsa(z, qd, kd, vdt, iq, ik):
    nq = SEQ // TQ
    qspec = pl.BlockSpec((TQ, GROUP_W), lambda b, i: (b * nq + i, 0))
    return pl.pallas_call(
        _dsa_body,
        grid=(BATCH, nq),
        in_specs=[qspec,
                  pl.BlockSpec((1, N_KV, SEQ, LANES), lambda b, i: (b, 0, 0, 0)),
                  pl.BlockSpec((1, N_KV, SEQ // KV_TILE, LANES, KV_TILE), lambda b, i: (b, 0, 0, 0, 0)),
                  pl.BlockSpec((TQ, 2 * LANES * IDX_HEADS), lambda b, i: (b * nq + i, 0)),
                  pl.BlockSpec((1, SEQ, 2 * LANES), lambda b, i: (b, 0, 0)),
                  pl.BlockSpec((TQ, LANES), lambda b, i: (b * nq + i, COL_C_IW // LANES)),
                  pl.BlockSpec((TQ, GROUP_W), lambda b, i: (b * nq + i, COL_C_G // GROUP_W))],
        out_specs=qspec,
        out_shape=jax.ShapeDtypeStruct((BATCH * SEQ, GROUP_W), BF16),
        scratch_shapes=[pltpu.VMEM((SEQ, TQ), F32), pltpu.VMEM((SEQ, TQ), F32), pltpu.VMEM((N_HEADS, TQ), F32),
                        pltpu.VMEM((N_HEADS, TQ), F32), pltpu.VMEM((N_HEADS, HEAD_DIM, TQ), F32),
                        pltpu.VMEM((8, TQ), F32)],
        compiler_params=_params(2),
        name="dsa_attn",
    )(qd, kd, vdt, iq, ik, z, z)


def _segment_moves():
    src = {}
    o = 0
    names = ("a_x", "a_g", "b_q", "b_g", "b_kv", "b_gate", "c_q", "c_g", "c_kv", "c_iq", "c_ik", "c_iw",
             "d_in", "d_b", "d_c", "d_g")
    for n, s in zip(names, SPLIT_SIZES):
        src[n] = (o, s)
        o += s
    dst = dict(a_x=COL_A_X, a_g=COL_A_G, b_q=COL_B_Q, b_g=COL_B_G, c_q=COL_C_Q, c_g=COL_C_G,
               d_in=COL_D_IN, d_b=COL_D_B, d_c=COL_D_C, d_g=COL_D_G, b_kv=COL_B_KV, c_kv=COL_C_KV,
               c_iq=COL_C_IQ, b_gate=COL_B_GATE, c_ik=COL_C_IK, c_iw=COL_C_IW)
    return [(src[n][0], dst[n], src[n][1]) for n in names]


def _wprep_body(w_ref, o_ref):
    o_ref[COL_B_GATE:PROJ_W, :] = jnp.zeros((PROJ_W - COL_B_GATE, TR_W), BF16)
    for s, d, width in _segment_moves():
        for c in range(0, width, GROUP_W):
            n = min(GROUP_W, width - c)
            o_ref[d + c:d + c + n, :] = w_ref[s + c:s + c + n, :].astype(BF16)


def _proj_weight(wt, layer):
    _, n, k = wt.shape
    return pl.pallas_call(
        _wprep_body,
        grid=(k // TR_W,),
        in_specs=[pl.BlockSpec((None, n, TR_W), lambda i: (layer, 0, i))],
        out_specs=pl.BlockSpec((PROJ_W, TR_W), lambda i: (0, i)),
        out_shape=jax.ShapeDtypeStruct((PROJ_W, k), BF16),
        compiler_params=_params(1),
        name="w_in_layout",
    )(wt)


def _rope_tables(head_dim):
    r = head_dim // 4
    half = r // 2
    pos = jnp.arange(SEQ, dtype=jnp.int32)
    inv = ROPE_THETA ** (-jnp.arange(half, dtype=F32) * 2.0 / r)
    ang = pos.astype(F32)[:, None] * inv[None, :]
    cos, sin = jnp.cos(ang), jnp.sin(ang)
    one = jnp.ones((SEQ, head_dim - r), F32)
    zero = jnp.zeros((SEQ, head_dim - r), F32)
    zh = jnp.zeros((SEQ, half), F32)
    reps = LANES // head_dim
    return tuple(jnp.tile(t, (1, reps)) for t in (
        jnp.concatenate([cos, cos, one], axis=1),
        jnp.concatenate([-sin, zh, zero], axis=1),
        jnp.concatenate([zh, sin, zero], axis=1)))


def _overlap_t():
    cs = jnp.arange(N_CMP_PAD) * CMP_STRIDE
    ss = jnp.arange(N_SLC) * SLC_BLK
    ov = jnp.clip(jnp.minimum(cs[None, :] + CMP_LEN, ss[:, None] + SLC_BLK)
                  - jnp.maximum(cs[None, :], ss[:, None]), 0, None).astype(F32) / CMP_LEN
    return jnp.where(jnp.arange(N_CMP_PAD)[None, :] < N_CMP_PAD - 1, ov, 0.0).astype(BF16)


def kernel(x, norm_g, w_in, w_out, lru_conv_w, lru_conv_b, lru_wa, lru_ba, lru_wx, lru_bx, lru_lambda,
           nsa_q_gain, nsa_k_gain, cmp_pe_k, cmp_w1_k, cmp_w2_k, cmp_pe_v, cmp_w1_v, cmp_w2_v,
           dsa_q_gain, dsa_k_gain, sc_conv_w):
    b, s, d = x.shape
    assert (b, s, d) == (BATCH, SEQ, D_MODEL)
    tabs128 = _rope_tables(HEAD_DIM)
    tabs64 = _rope_tables(IDX_DIM)
    ovt = _overlap_t()
    x2 = x.reshape(b * s, d)
    w_in_t = jnp.swapaxes(w_in, 1, 2)

    def row(v):
        return v.reshape(1, -1)

    for l in range(DEPTH):
        z = _inproj(x2, row(norm_g[l]), _proj_weight(w_in_t, l))
        ya, yd = _mix_ad(z, lru_conv_w[l], row(lru_conv_b[l]), lru_wa[l].astype(BF16), row(lru_ba[l]),
                         lru_wx[l].astype(BF16), row(lru_bx[l]), row(lru_lambda[l]), sc_conv_w[l])
        qn, qr, ks, vst, kw, vwt, qd, kd, vdt, iq, ik = _prep(
            z, tabs128, tabs64, row(nsa_q_gain[l]), row(nsa_k_gain[l]), row(dsa_q_gain[l]), row(dsa_k_gain[l]))
        cmp_, cmpt = _compress(z, jnp.stack([cmp_pe_k[l], cmp_pe_v[l]]),
                               jnp.stack([cmp_w1_k[l], cmp_w1_v[l]]).astype(BF16),
                               jnp.stack([cmp_w2_k[l], cmp_w2_v[l]]).astype(BF16), row(nsa_k_gain[l]))
        yb = _nsa(z, qn, qr, cmp_, cmpt, ks, vst, kw, vwt, ovt)
        yc = _dsa(z, qd, kd, vdt, iq, ik)
        x2 = _outproj((ya, yb, yc, yd), w_out, l, x2)
    return x2.reshape(b, s, d)
```
